```python
import math
import jax, jax.numpy as jnp
from jax import lax
import numpy as np

D_MODEL = 2048
BATCH = 8
SEQ = 2048
DEPTH = 1
DEC_BATCH = 128
DEC_SEQ = 1
PAST_LEN = 16384
PAGE_SIZE = 128

N_META = 16
D_SSM = D_MODEL // 2
SSM_GROUP = 16
N_SSM_GROUPS = D_SSM // SSM_GROUP
SSM_STATE = 64
HEAD_DIM = 64
N_HEADS = (D_MODEL - D_SSM) // HEAD_DIM
N_KV_HEADS = 4
Q_PER_KV = N_HEADS // N_KV_HEADS
D_ATTN = N_HEADS * HEAD_DIM
D_KV = N_KV_HEADS * HEAD_DIM
D_IN = D_SSM + D_ATTN + 2 * D_KV
WINDOW = 128
BLOCK = 128
ROPE_THETA = 10000.0
D_FF = 5632
CONV_W = 3
EPS = 1e-6
NEG = -1e30

kernel_name = "hymba_s5_swa_convffn_step"


def rmsnorm(x, g):
    xf = x.astype(jnp.float32)
    y = xf * lax.rsqrt(jnp.mean(xf * xf, axis=-1, keepdims=True) + EPS) * g.astype(jnp.float32)
    return y.astype(x.dtype)


def rope(x, pos):
    half = HEAD_DIM // 2
    inv = ROPE_THETA ** (-jnp.arange(half, dtype=jnp.float32) * 2.0 / HEAD_DIM)
    ang = pos.astype(jnp.float32)[:, None] * inv[None, :]
    cos = jnp.cos(ang)[:, None, :]
    sin = jnp.sin(ang)[:, None, :]
    xf = x.astype(jnp.float32)
    x1, x2 = xf[..., :half], xf[..., half:]
    return jnp.concatenate([x1 * cos - x2 * sin, x1 * sin + x2 * cos], axis=-1).astype(x.dtype)


def project(h, w_in, q_norm_g, k_norm_g, pos):
    bn, t, _ = h.shape
    z = h @ w_in
    u = z[..., :D_SSM]
    q = z[..., D_SSM:D_SSM + D_ATTN].reshape(bn, t, N_HEADS, HEAD_DIM)
    k = z[..., D_SSM + D_ATTN:D_SSM + D_ATTN + D_KV].reshape(bn, t, N_KV_HEADS, HEAD_DIM)
    v = z[..., D_SSM + D_ATTN + D_KV:].reshape(bn, t, N_KV_HEADS, HEAD_DIM)
    q = rope(rmsnorm(q, q_norm_g), pos)
    k = rope(rmsnorm(k, k_norm_g), pos)
    return u, q, k, v


def _cscan_combine(e1, e2):
    a1r, a1i, b1r, b1i = e1
    a2r, a2i, b2r, b2i = e2
    return (a2r * a1r - a2i * a1i,
            a2r * a1i + a2i * a1r,
            a2r * b1r - a2i * b1i + b2r,
            a2r * b1i + a2i * b1r + b2i)


def s5_mixer(u, h0_re, h0_im, lam_re, lam_im, log_dt, b_re, b_im, c_re, c_im, d_skip, w_glu, b_glu):
    bn, t, _ = u.shape
    uf = u.astype(jnp.float32).reshape(bn, t, N_SSM_GROUPS, SSM_GROUP)
    dt = jnp.exp(log_dt.astype(jnp.float32))[:, None]
    lr = lam_re.astype(jnp.float32)
    li = lam_im.astype(jnp.float32)
    mag = jnp.exp(lr * dt)
    a_re = mag * jnp.cos(li * dt)
    a_im = mag * jnp.sin(li * dt)
    den = lr * lr + li * li
    am1 = a_re - 1.0
    coef_re = (am1 * lr + a_im * li) / den
    coef_im = (a_im * lr - am1 * li) / den
    bu_re = jnp.einsum('btgh,gph->btgp', uf, b_re.astype(jnp.float32))
    bu_im = jnp.einsum('btgh,gph->btgp', uf, b_im.astype(jnp.float32))
    x_re = coef_re * bu_re - coef_im * bu_im
    x_im = coef_re * bu_im + coef_im * bu_re
    h0r = h0_re.astype(jnp.float32)
    h0i = h0_im.astype(jnp.float32)
    x_re = x_re.at[:, 0].add(a_re * h0r - a_im * h0i)
    x_im = x_im.at[:, 0].add(a_re * h0i + a_im * h0r)
    a_re_b = jnp.broadcast_to(a_re, x_re.shape)
    a_im_b = jnp.broadcast_to(a_im, x_im.shape)
    _, _, s_re, s_im = lax.associative_scan(_cscan_combine, (a_re_b, a_im_b, x_re, x_im), axis=1)
    y = (jnp.einsum('btgp,ghp->btgh', s_re, c_re.astype(jnp.float32))
         - jnp.einsum('btgp,ghp->btgh', s_im, c_im.astype(jnp.float32)))
    y = y + d_skip.astype(jnp.float32) * uf
    y = jax.nn.gelu(y.reshape(bn, t, D_SSM)).astype(u.dtype)
    y = y * jax.nn.sigmoid(y @ w_glu + b_glu)
    return y, s_re[:, -1], s_im[:, -1]


def _sink_softmax(s, sinks_b):
    sink_col = jnp.broadcast_to(sinks_b, s.shape[:-1] + (1,))
    p = jax.nn.softmax(jnp.concatenate([s, sink_col], axis=-1), axis=-1)
    return p[..., :-1]


def swa_prompt(q, k, v, sinks):
    bn, t = q.shape[:2]
    pad = (-t) % BLOCK
    tp = t + pad
    nb = tp // BLOCK
    padw = ((0, 0), (pad, 0), (0, 0), (0, 0))
    qb = jnp.pad(q, padw).reshape(bn, nb, BLOCK, N_KV_HEADS, Q_PER_KV, HEAD_DIM)
    kb = jnp.pad(k, padw).reshape(bn, nb, BLOCK, N_KV_HEADS, HEAD_DIM)
    vb = jnp.pad(v, padw).reshape(bn, nb, BLOCK, N_KV_HEADS, HEAD_DIM)
    shift = ((0, 0), (1, 0), (0, 0), (0, 0), (0, 0))
    k2 = jnp.concatenate([jnp.pad(kb, shift)[:, :-1], kb], axis=2)
    v2 = jnp.concatenate([jnp.pad(vb, shift)[:, :-1], vb], axis=2)
    s = jnp.einsum('bnqkgd,bnskd->bnkgqs', qb, k2).astype(jnp.float32) * (HEAD_DIM ** -0.5)
    blk = jnp.arange(nb)[:, None] * BLOCK
    qi = blk + jnp.arange(BLOCK)[None, :]
    ki = blk - BLOCK + jnp.arange(2 * BLOCK)[None, :]
    diff = qi[:, :, None] - ki[:, None, :]
    mask = (diff >= 0) & (diff < WINDOW) & (ki[:, None, :] >= pad)
    s = jnp.where(mask[None, :, None, None], s, NEG)
    sk = sinks.astype(jnp.float32).reshape(N_KV_HEADS, Q_PER_KV)[None, None, :, :, None, None]
    p = _sink_softmax(s, sk)
    o = jnp.einsum('bnkgqs,bnskd->bnqkgd', p.astype(v.dtype), v2)
    return o.reshape(bn, tp, D_ATTN)[:, pad:]


def swa_sample(q, k, v, ck, cv, sinks):
    bn, s_len = q.shape[:2]
    k_all = jnp.concatenate([ck.astype(k.dtype), k], axis=1)
    v_all = jnp.concatenate([cv.astype(v.dtype), v], axis=1)
    qg = q.reshape(bn, s_len, N_KV_HEADS, Q_PER_KV, HEAD_DIM)
    s = jnp.einsum('bqkgd,bskd->bkgqs', qg, k_all).astype(jnp.float32) * (HEAD_DIM ** -0.5)
    qpos = PAST_LEN + jnp.arange(s_len)
    kpos = PAST_LEN - WINDOW + jnp.arange(WINDOW + s_len)
    diff = qpos[:, None] - kpos[None, :]
    mask = (diff >= 0) & (diff < WINDOW)
    s = jnp.where(mask, s, NEG)
    sk = sinks.astype(jnp.float32).reshape(N_KV_HEADS, Q_PER_KV)[None, :, :, None, None]
    p = _sink_softmax(s, sk)
    o = jnp.einsum('bkgqs,bskd->bqkgd', p.astype(v.dtype), v_all).reshape(bn, s_len, D_ATTN)
    return o, k_all[:, -WINDOW:], v_all[:, -WINDOW:]


def conv_ffn(h, buf, w_up, conv_w, conv_b, w_down):
    t = h.shape[1]
    up = h @ w_up
    xp = jnp.concatenate([buf.astype(up.dtype), up], axis=1)
    c = conv_b
    for j in range(CONV_W):
        c = c + conv_w[j] * xp[:, j:j + t]
    val, gate = c[..., :D_FF], c[..., D_FF:]
    out = (jax.nn.silu(gate) * val) @ w_down
    return out, xp[:, -(CONV_W - 1):]


def block(x, pos, attn_cache, h0_re, h0_im, conv_buf, norm_mix_g, w_in, q_norm_g, k_norm_g, attn_sinks,
          lam_re, lam_im, log_dt, ssm_b_re, ssm_b_im, ssm_c_re, ssm_c_im, ssm_d, w_glu, b_glu,
          ssm_out_g, attn_out_g, w_out, norm_ffn_g, w_up, conv_w, conv_b, w_down):
    h = rmsnorm(x, norm_mix_g)
    u, q, k, v = project(h, w_in, q_norm_g, k_norm_g, pos)
    y_ssm, s_re, s_im = s5_mixer(u, h0_re, h0_im, lam_re, lam_im, log_dt, ssm_b_re, ssm_b_im,
                                 ssm_c_re, ssm_c_im, ssm_d, w_glu, b_glu)
    if attn_cache is None:
        y_att = swa_prompt(q, k, v, attn_sinks)
        nk, nv = k[:, -WINDOW:], v[:, -WINDOW:]
    else:
        y_att, nk, nv = swa_sample(q, k, v, attn_cache[0], attn_cache[1], attn_sinks)
    mixed = jnp.concatenate([rmsnorm(y_ssm, ssm_out_g), rmsnorm(y_att.astype(x.dtype), attn_out_g)], axis=-1)
    x = x + mixed @ w_out
    f, new_buf = conv_ffn(rmsnorm(x, norm_ffn_g), conv_buf, w_up, conv_w, conv_b, w_down)
    x = x + f
    return x, nk, nv, s_re, s_im, new_buf


def setup_inputs(seed: int = 0) -> dict:
    key = jax.random.key(seed)
    ks = iter(jax.random.split(key, 40))
    f32 = jnp.float32

    def nrm(shape, scale):
        return jax.random.normal(next(ks), shape, f32) * scale

    G, P, H = N_SSM_GROUPS, SSM_STATE, SSM_GROUP
    d = {}
    d['x_prompt'] = nrm((BATCH, SEQ, D_MODEL), 1.0)
    d['x_sample'] = nrm((DEC_BATCH, DEC_SEQ, D_MODEL), 1.0)
    d['cache_k'] = nrm((DEPTH, DEC_BATCH, WINDOW, N_KV_HEADS, HEAD_DIM), 1.0)
    d['cache_v'] = nrm((DEPTH, DEC_BATCH, WINDOW, N_KV_HEADS, HEAD_DIM), 1.0)
    d['state_ssm_re'] = nrm((DEPTH, DEC_BATCH, G, P), 0.1)
    d['state_ssm_im'] = nrm((DEPTH, DEC_BATCH, G, P), 0.1)
    d['state_conv'] = nrm((DEPTH, DEC_BATCH, CONV_W - 1, 2 * D_FF), 1.0)
    d['meta_tokens'] = nrm((N_META, D_MODEL), 1.0)
    d['norm_mix_g'] = 1.0 + nrm((DEPTH, D_MODEL), 0.02)
    d['w_in'] = nrm((DEPTH, D_MODEL, D_IN), D_MODEL ** -0.5)
    d['q_norm_g'] = 1.0 + nrm((DEPTH, HEAD_DIM), 0.02)
    d['k_norm_g'] = 1.0 + nrm((DEPTH, HEAD_DIM), 0.02)
    d['attn_sinks'] = nrm((DEPTH, N_HEADS), 0.5)
    d['lam_re'] = -0.5 + nrm((DEPTH, G, P), 0.01)
    d['lam_im'] = math.pi * jnp.arange(P, dtype=f32)[None, None, :] + nrm((DEPTH, G, P), 0.01)
    d['log_dt'] = jax.random.uniform(next(ks), (DEPTH, G), f32, math.log(1e-3), math.log(1e-1))
    d['ssm_b_re'] = nrm((DEPTH, G, P, H), (2 * H) ** -0.5)
    d['ssm_b_im'] = nrm((DEPTH, G, P, H), (2 * H) ** -0.5)
    d['ssm_c_re'] = nrm((DEPTH, G, H, P), (2 * P) ** -0.5)
    d['ssm_c_im'] = nrm((DEPTH, G, H, P), (2 * P) ** -0.5)
    d['ssm_d'] = nrm((DEPTH, G, H), 1.0)
    d['w_glu'] = nrm((DEPTH, D_SSM, D_SSM), D_SSM ** -0.5)
    d['b_glu'] = nrm((DEPTH, D_SSM), 0.01)
    d['ssm_out_g'] = 1.0 + nrm((DEPTH, D_SSM), 0.02)
    d['attn_out_g'] = 1.0 + nrm((DEPTH, D_ATTN), 0.02)
    d['w_out'] = nrm((DEPTH, D_SSM + D_ATTN, D_MODEL), (D_SSM + D_ATTN) ** -0.5)
    d['norm_ffn_g'] = 1.0 + nrm((DEPTH, D_MODEL), 0.02)
    d['w_up'] = nrm((DEPTH, D_MODEL, 2 * D_FF), D_MODEL ** -0.5)
    d['conv_w'] = nrm((DEPTH, CONV_W, 2 * D_FF), CONV_W ** -0.5)
    d['conv_b'] = nrm((DEPTH, 2 * D_FF), 0.01)
    d['w_down'] = nrm((DEPTH, D_FF, D_MODEL), D_FF ** -0.5)
    return d


def reference(x_prompt, x_sample, cache_k, cache_v, state_ssm_re, state_ssm_im, state_conv,
              meta_tokens, norm_mix_g, w_in, q_norm_g, k_norm_g, attn_sinks, lam_re, lam_im, log_dt,
              ssm_b_re, ssm_b_im, ssm_c_re, ssm_c_im, ssm_d, w_glu, b_glu, ssm_out_g, attn_out_g,
              w_out, norm_ffn_g, w_up, conv_w, conv_b, w_down):
    xp = jnp.concatenate([jnp.broadcast_to(meta_tokens.astype(x_prompt.dtype)[None], (BATCH, N_META, D_MODEL)),
                          x_prompt], axis=1)
    pos_p = jnp.arange(N_META + SEQ)
    xs = x_sample
    pos_s = PAST_LEN + jnp.arange(DEC_SEQ)
    zero_re = jnp.zeros((BATCH, N_SSM_GROUPS, SSM_STATE), jnp.float32)
    zero_buf = jnp.zeros((BATCH, CONV_W - 1, 2 * D_FF), x_prompt.dtype)

    pk, pv, pre, pim, pconv = [], [], [], [], []
    sk, sv, sre, sim, sconv = [], [], [], [], []
    for l in range(DEPTH):
        w_l = (norm_mix_g[l], w_in[l], q_norm_g[l], k_norm_g[l], attn_sinks[l], lam_re[l], lam_im[l], log_dt[l],
               ssm_b_re[l], ssm_b_im[l], ssm_c_re[l], ssm_c_im[l], ssm_d[l], w_glu[l], b_glu[l],
               ssm_out_g[l], attn_out_g[l], w_out[l], norm_ffn_g[l], w_up[l], conv_w[l], conv_b[l], w_down[l])
        xp, k1, v1, r1, i1, c1 = block(xp, pos_p, None, zero_re, zero_re, zero_buf, *w_l)
        pk.append(k1); pv.append(v1); pre.append(r1); pim.append(i1); pconv.append(c1)
        xs, k2, v2, r2, i2, c2 = block(xs, pos_s, (cache_k[l], cache_v[l]), state_ssm_re[l], state_ssm_im[l],
                                       state_conv[l], *w_l)
        sk.append(k2); sv.append(v2); sre.append(r2); sim.append(i2); sconv.append(c2)

    y_prompt = xp[:, N_META:]
    y_sample = xs
    return (y_prompt, y_sample,
            jnp.stack(pk), jnp.stack(pv), jnp.stack(pre), jnp.stack(pim), jnp.stack(pconv),
            jnp.stack(sk), jnp.stack(sv), jnp.stack(sre), jnp.stack(sim), jnp.stack(sconv))
```

```python
import functools

import jax
import jax.numpy as jnp
from jax import lax
from jax.experimental import pallas as pl
from jax.experimental.pallas import tpu as pltpu

F32 = jnp.float32
BF16 = jnp.bfloat16

D_MODEL = 2048
D_SSM = 1024
SSM_GROUP = 16
N_SSM_GROUPS = 64
SSM_STATE = 64
D_STATE = N_SSM_GROUPS * SSM_STATE
HEAD_DIM = 64
N_HEADS = 16
N_KV_HEADS = 4
Q_PER_KV = 4
D_ATTN = 1024
D_KV = 256
D_IN = D_SSM + D_ATTN + 2 * D_KV
WINDOW = 128
BLOCK = 128
ROPE_THETA = 10000.0
D_FF = 5632
CONV_W = 3
EPS = 1e-6
NEG = -1e30
PAST_LEN = 16384

SUBLANES = 8
VMEM_LIMIT = 56 * 1024 * 1024
FF_TILE = 512
SCAN_LANES = 512
SSM_GROUPS_PER_DOT = 16


def _const_spec(shape):
    nd = len(shape)
    return pl.BlockSpec(shape, lambda *_: (0,) * nd, pipeline_mode=pl.Buffered(1))


def _rms(x, g):
    return x * lax.rsqrt(jnp.mean(x * x, axis=-1, keepdims=True) + EPS) * g


def _mm(a, b):
    return jnp.dot(a, b, preferred_element_type=F32)


def _in_proj_kernel(x_ref, g_ref, w_ref, cos_ref, sin_ref, qg_ref, kg_ref, ones_ref,
                    u_ref, q_ref, k_ref, v_ref):
    h = _rms(x_ref[...], g_ref[...]).astype(BF16)
    z = _mm(h, w_ref[...])
    u_ref[...] = z[:, :D_SSM]
    v_ref[...] = z[:, D_SSM + D_ATTN + D_KV:]
    cos = jnp.concatenate([cos_ref[...], cos_ref[...]], axis=1)
    sin = jnp.concatenate([sin_ref[...], sin_ref[...]], axis=1)
    lane = lax.broadcasted_iota(jnp.int32, cos.shape, 1)
    first_half = (lane % HEAD_DIM) < (HEAD_DIM // 2)
    ones = ones_ref[...]
    n_q = D_ATTN // 256
    for c in range(n_q + D_KV // 256):
        zc = z[:, D_SSM + 256 * c: D_SSM + 256 * (c + 1)]
        sq = zc * zc
        hi = sq.astype(BF16)
        lo = (sq - hi.astype(F32)).astype(BF16)
        ssum = _mm(hi, ones) + _mm(lo, ones)
        gain = qg_ref[...] if c < n_q else kg_ref[...]
        n = zc * lax.rsqrt(ssum * (1.0 / HEAD_DIM) + EPS) * gain
        partner = jnp.where(first_half, pltpu.roll(n, 256 - HEAD_DIM // 2, 1),
                            pltpu.roll(n, HEAD_DIM // 2, 1))
        r = n * cos + partner * sin
        if c < n_q:
            q_ref[:, 256 * c: 256 * (c + 1)] = (r * (HEAD_DIM ** -0.5)).astype(BF16)
        else:
            k_ref[...] = r


def _in_proj(x2, g, w_bf, cos, sin, qg, kg, ones_bd, *, nb, tm):
    n = x2.shape[0]
    t_len = n // nb
    nt = t_len // tm
    row = lambda b, t: (b * nt + t, 0)
    return pl.pallas_call(
        _in_proj_kernel,
        grid=(nb, nt),
        in_specs=[
            pl.BlockSpec((tm, D_MODEL), row),
            _const_spec((1, D_MODEL)),
            _const_spec((D_MODEL, D_IN)),
            pl.BlockSpec((tm, 128), lambda b, t: (t, 0)),
            pl.BlockSpec((tm, 128), lambda b, t: (t, 0)),
            _const_spec((1, 256)),
            _const_spec((1, 256)),
            _const_spec((256, 256)),
        ],
        out_specs=[
            pl.BlockSpec((tm, D_SSM), lambda b, t: (t, b)),
            pl.BlockSpec((tm, D_ATTN), row),
            pl.BlockSpec((tm, D_KV), row),
            pl.BlockSpec((tm, D_KV), row),
        ],
        out_shape=[
            jax.ShapeDtypeStruct((t_len, nb * D_SSM), F32),
            jax.ShapeDtypeStruct((n, D_ATTN), BF16),
            jax.ShapeDtypeStruct((n, D_KV), F32),
            jax.ShapeDtypeStruct((n, D_KV), F32),
        ],
        compiler_params=pltpu.CompilerParams(
            dimension_semantics=("arbitrary", "arbitrary"), vmem_limit_bytes=VMEM_LIMIT),
        name="in_proj",
    )(x2, g, w_bf, cos, sin, qg, kg, ones_bd)


def _ssm_kernel(u_ref, h0re_ref, h0im_ref, lamre_ref, lamim_ref, logdt_ref, bblk_ref,
                cre_ref, cim_ref, dskip_ref, wglu_ref, bglu_ref, g_ref,
                y_ref, sre_ref, sim_ref,
                are_s, aim_s, cfre_s, cfim_s, xre_s, xim_s, stre_s, stim_s, *, tt, nbg):
    step = pl.program_id(0)
    rows = tt * nbg * SUBLANES
    lanes_per_dot = SSM_GROUPS_PER_DOT * SSM_STATE
    chans_per_dot = SSM_GROUPS_PER_DOT * SSM_GROUP
    n_dots = N_SSM_GROUPS // SSM_GROUPS_PER_DOT

    @pl.when(step == 0)
    def _():
        lr = lamre_ref[...]
        li = lamim_ref[...]
        dt = jnp.exp(logdt_ref[...])
        mag = jnp.exp(lr * dt)
        a_re = mag * jnp.cos(li * dt)
        a_im = mag * jnp.sin(li * dt)
        den = lr * lr + li * li
        am1 = a_re - 1.0
        shape = (SUBLANES, D_STATE)
        are_s[...] = jnp.broadcast_to(a_re, shape)
        aim_s[...] = jnp.broadcast_to(a_im, shape)
        cfre_s[...] = jnp.broadcast_to((am1 * lr + a_im * li) / den, shape)
        cfim_s[...] = jnp.broadcast_to((a_im * lr - am1 * li) / den, shape)
        stre_s[...] = h0re_ref[...]
        stim_s[...] = h0im_ref[...]

    u = u_ref[...]
    ub = u.astype(BF16)
    for c in range(n_dots):
        bu = _mm(ub[:, chans_per_dot * c: chans_per_dot * (c + 1)], bblk_ref[c])
        b_re = bu[:, :lanes_per_dot]
        b_im = bu[:, lanes_per_dot:]
        ls = slice(lanes_per_dot * c, lanes_per_dot * (c + 1))
        cf_re = cfre_s[0:1, ls]
        cf_im = cfim_s[0:1, ls]
        xre_s[:, ls] = cf_re * b_re - cf_im * b_im
        xim_s[:, ls] = cf_re * b_im + cf_im * b_re

    for bg in range(nbg):
        rs = slice(SUBLANES * bg, SUBLANES * (bg + 1))
        for lb in range(D_STATE // SCAN_LANES):
            ls = slice(SCAN_LANES * lb, SCAN_LANES * (lb + 1))
            a_re = are_s[:, ls]
            a_im = aim_s[:, ls]

            def body(t, carry, bg=bg, ls=ls, a_re=a_re, a_im=a_im):
                s_re, s_im = carry
                r0 = (t * nbg + bg) * SUBLANES
                if not isinstance(r0, int):
                    r0 = pl.multiple_of(r0, SUBLANES)
                n_re = a_re * s_re - a_im * s_im + xre_s[pl.ds(r0, SUBLANES), ls]
                n_im = a_re * s_im + a_im * s_re + xim_s[pl.ds(r0, SUBLANES), ls]
                xre_s[pl.ds(r0, SUBLANES), ls] = n_re
                xim_s[pl.ds(r0, SUBLANES), ls] = n_im
                return n_re, n_im

            carry = (stre_s[rs, ls], stim_s[rs, ls])
            if tt == 1:
                carry = body(0, carry)
            else:
                carry = lax.fori_loop(0, tt, body, carry, unroll=8)
            stre_s[rs, ls] = carry[0]
            stim_s[rs, ls] = carry[1]

    ys = []
    for c in range(n_dots):
        ls = slice(lanes_per_dot * c, lanes_per_dot * (c + 1))
        cs = slice(chans_per_dot * c, chans_per_dot * (c + 1))
        yc = _mm(xre_s[:, ls].astype(BF16), cre_ref[c]) - _mm(xim_s[:, ls].astype(BF16), cim_ref[c])
        ys.append(jax.nn.gelu(yc + dskip_ref[:, cs] * u[:, cs]))
    y = jnp.concatenate(ys, axis=1)
    y = y * jax.nn.sigmoid(_mm(y.astype(BF16), wglu_ref[...]) + bglu_ref[...])
    y_ref[...] = _rms(y, g_ref[...]).astype(BF16)

    @pl.when(step == pl.num_programs(0) - 1)
    def _():
        sre_ref[...] = stre_s[...]
        sim_ref[...] = stim_s[...]


def _ssm(u_tm, h0re, h0im, lamre, lamim, logdt, bblk, cre, cim, dskip, wglu, bglu, g, *, nbatch, tt):
    n = u_tm.shape[0]
    t_len = n // nbatch
    nbg = nbatch // SUBLANES
    rows = tt * nbatch
    n_dots = N_SSM_GROUPS // SSM_GROUPS_PER_DOT
    kern = functools.partial(_ssm_kernel, tt=tt, nbg=nbg)
    return pl.pallas_call(
        kern,
        grid=(t_len // tt,),
        in_specs=[
            pl.BlockSpec((rows, D_SSM), lambda s: (s, 0)),
            _const_spec((nbatch, D_STATE)),
            _const_spec((nbatch, D_STATE)),
            _const_spec((1, D_STATE)),
            _const_spec((1, D_STATE)),
            _const_spec((1, D_STATE)),
            _const_spec(bblk.shape),
            _const_spec(cre.shape),
            _const_spec(cim.shape),
            _const_spec((1, D_SSM)),
            _const_spec((D_SSM, D_SSM)),
            _const_spec((1, D_SSM)),
            _const_spec((1, D_SSM)),
        ],
        out_specs=[
            pl.BlockSpec((rows, D_SSM), lambda s: (s, 0)),
            pl.BlockSpec((nbatch, D_STATE), lambda s: (0, 0)),
            pl.BlockSpec((nbatch, D_STATE), lambda s: (0, 0)),
        ],
        out_shape=[
            jax.ShapeDtypeStruct((n, D_SSM), BF16),
            jax.ShapeDtypeStruct((nbatch, D_STATE), F32),
            jax.ShapeDtypeStruct((nbatch, D_STATE), F32),
        ],
        scratch_shapes=[pltpu.VMEM((SUBLANES, D_STATE), F32)] * 4
        + [pltpu.VMEM((rows, D_STATE), F32)] * 2
        + [pltpu.VMEM((nbatch, D_STATE), F32)] * 2,
        compiler_params=pltpu.CompilerParams(
            dimension_semantics=("arbitrary",), vmem_limit_bytes=VMEM_LIMIT),
        name="ssm",
    )(u_tm, h0re, h0im, lamre, lamim, logdt, bblk, cre, cim, dskip, wglu, bglu, g)


def _attn_seq_kernel(sink_ref, q_ref, kc_ref, kp_ref, k0_ref, vc_ref, vp_ref, v0_ref, g_ref, o_ref,
                     *, first_block, n_meta):
    i = pl.program_id(1)
    is_first = i == 0
    k2 = jnp.concatenate([jnp.where(is_first, k0_ref[0], kp_ref[...]), kc_ref[...]], axis=0).astype(BF16)
    v2 = jnp.concatenate([jnp.where(is_first, v0_ref[0], vp_ref[...]), vc_ref[...]], axis=0).astype(BF16)
    gb = i + first_block
    col_min = jnp.where(gb == 0, 2 * BLOCK - n_meta, jnp.where(gb == 1, BLOCK - n_meta, 0))
    r = lax.broadcasted_iota(jnp.int32, (BLOCK, 2 * BLOCK), 0)
    c = lax.broadcasted_iota(jnp.int32, (BLOCK, 2 * BLOCK), 1)
    mask = (c > r) & (c <= r + WINDOW) & (c >= col_min)
    outs = []
    for kv in range(N_KV_HEADS):
        kg = k2[:, HEAD_DIM * kv: HEAD_DIM * (kv + 1)]
        vg = v2[:, HEAD_DIM * kv: HEAD_DIM * (kv + 1)]
        for j in range(Q_PER_KV):
            hd = kv * Q_PER_KV + j
            qh = q_ref[:, HEAD_DIM * hd: HEAD_DIM * (hd + 1)]
            s = lax.dot_general(qh, kg, (((1,), (1,)), ((), ())), preferred_element_type=F32)
            s = jnp.where(mask, s, NEG)
            sink = sink_ref[hd]
            m = jnp.maximum(jnp.max(s, axis=-1, keepdims=True), sink)
            p = jnp.exp(s - m)
            den = jnp.sum(p, axis=-1, keepdims=True) + jnp.exp(sink - m)
            outs.append(_mm((p / den).astype(BF16), vg))
    o_ref[...] = _rms(jnp.concatenate(outs, axis=1), g_ref[...]).astype(BF16)


def _attn_seq(sinks, q, k, v, k0, v0, g, *, nb, n_meta, first_block):
    n = q.shape[0]
    nblk = n // nb // BLOCK
    cur = lambda b, i: (b * nblk + i, 0)
    prev = lambda b, i: (b * nblk + jnp.maximum(i - 1, 0), 0)
    init = lambda b, i: (b, 0, 0)
    kern = functools.partial(_attn_seq_kernel, first_block=first_block, n_meta=n_meta)
    return pl.pallas_call(
        kern,
        grid=(nb, nblk),
        in_specs=[
            pl.BlockSpec(memory_space=pltpu.SMEM),
            pl.BlockSpec((BLOCK, D_ATTN), cur),
            pl.BlockSpec((BLOCK, D_KV), cur),
            pl.BlockSpec((BLOCK, D_KV), prev),
            pl.BlockSpec((1, BLOCK, D_KV), init),
            pl.BlockSpec((BLOCK, D_KV), cur),
            pl.BlockSpec((BLOCK, D_KV), prev),
            pl.BlockSpec((1, BLOCK, D_KV), init),
            _const_spec((1, D_ATTN)),
        ],
        out_specs=pl.BlockSpec((BLOCK, D_ATTN), cur),
        out_shape=jax.ShapeDtypeStruct((n, D_ATTN), BF16),
        compiler_params=pltpu.CompilerParams(
            dimension_semantics=("arbitrary", "arbitrary"), vmem_limit_bytes=VMEM_LIMIT),
        name="attn_seq",
    )(sinks, q, k, k, k0, v, v, v0, g)


def _attn_step_kernel(q_ref, kn_ref, vn_ref, ck_ref, cv_ref, sink_ref, g_ref, o_ref, ok_ref, ov_ref):
    ok_ref[:, 0:WINDOW - 1, :] = ck_ref[:, 1:WINDOW, :]
    ok_ref[:, WINDOW - 1:WINDOW, :] = kn_ref[...]
    ov_ref[:, 0:WINDOW - 1, :] = cv_ref[:, 1:WINDOW, :]
    ov_ref[:, WINDOW - 1:WINDOW, :] = vn_ref[...]
    kb = ok_ref[...].astype(BF16)
    vb = ov_ref[...].astype(BF16)
    q = q_ref[...]
    q4 = jnp.concatenate([q] * N_KV_HEADS, axis=-1)
    hrow = lax.broadcasted_iota(jnp.int32, (N_HEADS, D_KV), 0)
    hcol = lax.broadcasted_iota(jnp.int32, (N_HEADS, D_KV), 1)
    own = (hcol // HEAD_DIM) == (hrow // Q_PER_KV)
    qe = jnp.where(own[None], q4, jnp.zeros_like(q4))
    s = jnp.einsum('bhd,bwd->bhw', qe, kb, preferred_element_type=F32)
    sink = sink_ref[...][None]
    m = jnp.maximum(jnp.max(s, axis=-1, keepdims=True), sink)
    p = jnp.exp(s - m)
    den = jnp.sum(p, axis=-1, keepdims=True) + jnp.exp(sink - m)
    o = jnp.einsum('bhw,bwd->bhd', (p / den).astype(BF16), vb, preferred_element_type=F32)
    o = jnp.where(own[None], o, 0.0)
    o = (o[..., 0:HEAD_DIM] + o[..., HEAD_DIM:2 * HEAD_DIM]
         + o[..., 2 * HEAD_DIM:3 * HEAD_DIM] + o[..., 3 * HEAD_DIM:4 * HEAD_DIM])
    ms = jnp.sum(jnp.sum(o * o, axis=2, keepdims=True), axis=1, keepdims=True) * (1.0 / D_ATTN)
    o_ref[...] = (o * lax.rsqrt(ms + EPS) * g_ref[...][None]).astype(BF16)


def _attn_step(q3, kn, vn, ck, cv, sinks, g, *, bb):
    nseq = q3.shape[0]
    blk3 = lambda i: (i, 0, 0)
    return pl.pallas_call(
        _attn_step_kernel,
        grid=(nseq // bb,),
        in_specs=[
            pl.BlockSpec((bb, N_HEADS, HEAD_DIM), blk3),
            pl.BlockSpec((bb, 1, D_KV), blk3),
            pl.BlockSpec((bb, 1, D_KV), blk3),
            pl.BlockSpec((bb, WINDOW, D_KV), blk3),
            pl.BlockSpec((bb, WINDOW, D_KV), blk3),
            _const_spec((N_HEADS, 1)),
            _const_spec((N_HEADS, HEAD_DIM)),
        ],
        out_specs=[
            pl.BlockSpec((bb, N_HEADS, HEAD_DIM), blk3),
            pl.BlockSpec((bb, WINDOW, D_KV), blk3),
            pl.BlockSpec((bb, WINDOW, D_KV), blk3),
        ],
        out_shape=[
            jax.ShapeDtypeStruct((nseq, N_HEADS, HEAD_DIM), BF16),
            jax.ShapeDtypeStruct((nseq, WINDOW, D_KV), F32),
            jax.ShapeDtypeStruct((nseq, WINDOW, D_KV), F32),
        ],
        compiler_params=pltpu.CompilerParams(
            dimension_semantics=("arbitrary",), vmem_limit_bytes=VMEM_LIMIT),
        name="attn_step",
    )(q3, kn, vn, ck, cv, sinks, g)


def _out_proj_kernel(x_ref, ys_ref, ya_ref, w_ref, o_ref):
    o_ref[...] = (x_ref[...] + _mm(ys_ref[...], w_ref[0:D_SSM, :])
                  + _mm(ya_ref[...], w_ref[D_SSM:D_SSM + D_ATTN, :]))


def _out_proj(x2, ys_tm, ya, w_bf, *, nb, tm):
    n = x2.shape[0]
    nt = n // nb // tm
    row = lambda b, t: (b * nt + t, 0)
    return pl.pallas_call(
        _out_proj_kernel,
        grid=(nb, nt),
        in_specs=[
            pl.BlockSpec((tm, D_MODEL), row),
            pl.BlockSpec((tm, D_SSM), lambda b, t: (t, b)),
            pl.BlockSpec((tm, D_ATTN), row),
            _const_spec((D_SSM + D_ATTN, D_MODEL)),
        ],
        out_specs=pl.BlockSpec((tm, D_MODEL), row),
        out_shape=jax.ShapeDtypeStruct((n, D_MODEL), F32),
        compiler_params=pltpu.CompilerParams(
            dimension_semantics=("arbitrary", "arbitrary"), vmem_limit_bytes=VMEM_LIMIT),
        name="out_proj",
    )(x2, ys_tm, ya, w_bf)


def _ffn_seq_kernel(x_ref, g_ref, wv_ref, wg_ref, cwv_ref, cwg_ref, cbv_ref, cbg_ref, wd_ref,
                    bufv_ref, bufg_ref, y_ref, nbv_ref, nbg_ref,
                    h_s, upv_s, upg_s, *, tm, tf):
    t = pl.program_id(1)
    j = pl.program_id(2)
    chan = pl.ds(pl.multiple_of(j * tf, tf), tf)

    @pl.when(j == 0)
    def _():
        x = x_ref[...]
        h_s[...] = _rms(x, g_ref[...]).astype(BF16)
        y_ref[...] = x

    @pl.when(t == 0)
    def _():
        upv_s[0:SUBLANES, :] = bufv_ref[0]
        upg_s[0:SUBLANES, :] = bufg_ref[0]

    @pl.when(t > 0)
    def _():
        upv_s[0:SUBLANES, :] = nbv_ref[0, :, chan]
        upg_s[0:SUBLANES, :] = nbg_ref[0, :, chan]

    h = h_s[...]

    def conv(up_s, w_ref, cw_ref, cb_ref):
        up = _mm(h, w_ref[...])
        up_s[SUBLANES:SUBLANES + tm, :] = up
        return (cb_ref[...] + cw_ref[0:1, :] * up_s[SUBLANES - 2:SUBLANES - 2 + tm, :]
                + cw_ref[1:2, :] * up_s[SUBLANES - 1:SUBLANES - 1 + tm, :] + cw_ref[2:3, :] * up)

    val = conv(upv_s, wv_ref, cwv_ref, cbv_ref)
    gate = conv(upg_s, wg_ref, cwg_ref, cbg_ref)
    act = (jax.nn.silu(gate) * val).astype(BF16)
    y_ref[...] += _mm(act, wd_ref[...])
    nbv_ref[0, :, chan] = upv_s[tm:tm + SUBLANES, :]
    nbg_ref[0, :, chan] = upg_s[tm:tm + SUBLANES, :]


def _ffn_seq(x1, g, wup_bf, conv_w, conv_b, wdown_bf, bufv, bufg, *, nb, tm):
    n = x1.shape[0]
    nt = n // nb // tm
    tf = FF_TILE
    nj = D_FF // tf
    row = lambda b, t, j: (b * nt + t, 0)
    kern = functools.partial(_ffn_seq_kernel, tm=tm, tf=tf)
    return pl.pallas_call(
        kern,
        grid=(nb, nt, nj),
        in_specs=[
            pl.BlockSpec((tm, D_MODEL), row),
            _const_spec((1, D_MODEL)),
            pl.BlockSpec((D_MODEL, tf), lambda b, t, j: (0, j)),
            pl.BlockSpec((D_MODEL, tf), lambda b, t, j: (0, nj + j)),
            pl.BlockSpec((CONV_W, tf), lambda b, t, j: (0, j)),
            pl.BlockSpec((CONV_W, tf), lambda b, t, j: (0, nj + j)),
            pl.BlockSpec((1, tf), lambda b, t, j: (0, j)),
            pl.BlockSpec((1, tf), lambda b, t, j: (0, nj + j)),
            pl.BlockSpec((tf, D_MODEL), lambda b, t, j: (j, 0)),
            pl.BlockSpec((1, SUBLANES, tf), lambda b, t, j: (b, 0, j)),
            pl.BlockSpec((1, SUBLANES, tf), lambda b, t, j: (b, 0, j)),
        ],
        out_specs=[
            pl.BlockSpec((tm, D_MODEL), row),
            pl.BlockSpec((1, SUBLANES, D_FF), lambda b, t, j: (b, 0, 0)),
            pl.BlockSpec((1, SUBLANES, D_FF), lambda b, t, j: (b, 0, 0)),
        ],
        out_shape=[
            jax.ShapeDtypeStruct((n, D_MODEL), F32),
            jax.ShapeDtypeStruct((nb, SUBLANES, D_FF), F32),
            jax.ShapeDtypeStruct((nb, SUBLANES, D_FF), F32),
        ],
        scratch_shapes=[
            pltpu.VMEM((tm, D_MODEL), BF16),
            pltpu.VMEM((tm + SUBLANES, tf), F32),
            pltpu.VMEM((tm + SUBLANES, tf), F32),
        ],
        compiler_params=pltpu.CompilerParams(
            dimension_semantics=("arbitrary", "arbitrary", "arbitrary"), vmem_limit_bytes=VMEM_LIMIT),
        name="ffn_seq",
    )(x1, g, wup_bf, wup_bf, conv_w, conv_w, conv_b, conv_b, wdown_bf, bufv, bufg)


def _ffn_step_kernel(x_ref, g_ref, wv_ref, wg_ref, cwv_ref, cwg_ref, cbv_ref, cbg_ref, wd_ref,
                     buf_ref, y_ref, nbuf_ref, h_s):
    j = pl.program_id(0)

    @pl.when(j == 0)
    def _():
        x = x_ref[...]
        h_s[...] = _rms(x, g_ref[...]).astype(BF16)
        y_ref[...] = x

    h = h_s[...]
    upv = _mm(h, wv_ref[...])
    upg = _mm(h, wg_ref[...])
    val = cbv_ref[...] + cwv_ref[0:1, :] * buf_ref[:, 0, :] + cwv_ref[1:2, :] * buf_ref[:, 2, :] + cwv_ref[2:3, :] * upv
    gate = cbg_ref[...] + cwg_ref[0:1, :] * buf_ref[:, 1, :] + cwg_ref[1:2, :] * buf_ref[:, 3, :] + cwg_ref[2:3, :] * upg
    act = (jax.nn.silu(gate) * val).astype(BF16)
    y_ref[...] += _mm(act, wd_ref[...])
    nbuf_ref[:, 0, :] = buf_ref[:, 2, :]
    nbuf_ref[:, 1, :] = buf_ref[:, 3, :]
    nbuf_ref[:, 2, :] = upv
    nbuf_ref[:, 3, :] = upg


def _ffn_step(x1, g, wup_bf, conv_w, conv_b, wdown_bf, buf4):
    nseq = x1.shape[0]
    tf = FF_TILE
    nj = D_FF // tf
    return pl.pallas_call(
        _ffn_step_kernel,
        grid=(nj,),
        in_specs=[
            _const_spec((nseq, D_MODEL)),
            _const_spec((1, D_MODEL)),
            pl.BlockSpec((D_MODEL, tf), lambda j: (0, j)),
            pl.BlockSpec((D_MODEL, tf), lambda j: (0, nj + j)),
            pl.BlockSpec((CONV_W, tf), lambda j: (0, j)),
            pl.BlockSpec((CONV_W, tf), lambda j: (0, nj + j)),
            pl.BlockSpec((1, tf), lambda j: (0, j)),
            pl.BlockSpec((1, tf), lambda j: (0, nj + j)),
            pl.BlockSpec((tf, D_MODEL), lambda j: (j, 0)),
            pl.BlockSpec((nseq, 2 * (CONV_W - 1), tf), lambda j: (0, 0, j)),
        ],
        out_specs=[
            pl.BlockSpec((nseq, D_MODEL), lambda j: (0, 0)),
            pl.BlockSpec((nseq, 2 * (CONV_W - 1), tf), lambda j: (0, 0, j)),
        ],
        out_shape=[
            jax.ShapeDtypeStruct((nseq, D_MODEL), F32),
            jax.ShapeDtypeStruct((nseq, 2 * (CONV_W - 1), D_FF), F32),
        ],
        scratch_shapes=[pltpu.VMEM((nseq, D_MODEL), BF16)],
        compiler_params=pltpu.CompilerParams(
            dimension_semantics=("arbitrary",), vmem_limit_bytes=VMEM_LIMIT),
        name="ffn_step",
    )(x1, g, wup_bf, wup_bf, conv_w, conv_w, conv_b, conv_b, wdown_bf, buf4)


def _rope_tables(pos):
    half = HEAD_DIM // 2
    inv = ROPE_THETA ** (-jnp.arange(half, dtype=F32) * 2.0 / HEAD_DIM)
    ang = pos.astype(F32)[:, None] * inv[None, :]
    c = jnp.cos(ang)
    s = jnp.sin(ang)
    return jnp.concatenate([c, c, c, c], axis=1), jnp.concatenate([-s, s, -s, s], axis=1)


def _block_diag(w):
    nd, ng, r, c = w.shape
    eye = jnp.eye(ng, dtype=w.dtype)
    return jnp.einsum('dgrc,gh->dgrhc', w, eye).reshape(nd, ng * r, ng * c)


def kernel(x_prompt, x_sample, cache_k, cache_v, state_ssm_re, state_ssm_im, state_conv, meta_tokens,
           norm_mix_g, w_in, q_norm_g, k_norm_g, attn_sinks, lam_re, lam_im, log_dt, ssm_b_re, ssm_b_im,
           ssm_c_re, ssm_c_im, ssm_d, w_glu, b_glu, ssm_out_g, attn_out_g, w_out, norm_ffn_g, w_up,
           conv_w, conv_b, w_down):
    depth = w_in.shape[0]
    assert depth == 1
    nb, seq, _ = x_prompt.shape
    nseq = x_sample.shape[0]
    n_meta = meta_tokens.shape[0]
    assert x_sample.shape[1] == 1 and n_meta == 16 and nb == SUBLANES
    assert seq % 512 == 0 and nseq % SUBLANES == 0
    l = 0
    gpd = SSM_GROUPS_PER_DOT
    n_dots = N_SSM_GROUPS // gpd

    w_in_bf = w_in[l].astype(BF16)
    w_glu_bf = w_glu[l].astype(BF16)
    w_out_bf = w_out[l].astype(BF16)
    w_up_bf = w_up[l].astype(BF16)
    w_down_bf = w_down[l].astype(BF16)
    g_mix = norm_mix_g[l][None, :]
    g_ffn = norm_ffn_g[l][None, :]
    qg = jnp.tile(q_norm_g[l], 256 // HEAD_DIM)[None, :]
    kg = jnp.tile(k_norm_g[l], 256 // HEAD_DIM)[None, :]
    lane_head = jnp.arange(256) // HEAD_DIM
    ones_bd = (lane_head[:, None] == lane_head[None, :]).astype(BF16)
    sinks = attn_sinks[l]
    lamre = lam_re[l].reshape(1, D_STATE)
    lamim = lam_im[l].reshape(1, D_STATE)
    logdt = jnp.repeat(log_dt[l], SSM_STATE)[None, :]
    b_re_t = jnp.swapaxes(ssm_b_re[l], 1, 2).reshape(n_dots, gpd, SSM_GROUP, SSM_STATE)
    b_im_t = jnp.swapaxes(ssm_b_im[l], 1, 2).reshape(n_dots, gpd, SSM_GROUP, SSM_STATE)
    bblk = jnp.concatenate([_block_diag(b_re_t), _block_diag(b_im_t)], axis=2).astype(BF16)
    cre = _block_diag(jnp.swapaxes(ssm_c_re[l], 1, 2).reshape(n_dots, gpd, SSM_STATE, SSM_GROUP)).astype(BF16)
    cim = _block_diag(jnp.swapaxes(ssm_c_im[l], 1, 2).reshape(n_dots, gpd, SSM_STATE, SSM_GROUP)).astype(BF16)
    dskip = ssm_d[l].reshape(1, D_SSM)
    bglu = b_glu[l][None, :]
    g_ssm = ssm_out_g[l][None, :]
    g_att = attn_out_g[l][None, :]
    cw = conv_w[l]
    cb = conv_b[l][None, :]

    ssm_w = (lamre, lamim, logdt, bblk, cre, cim, dskip, w_glu_bf, bglu, g_ssm)

    pad = BLOCK - n_meta
    xm = jnp.concatenate([jnp.zeros((pad, D_MODEL), F32), meta_tokens.astype(F32)], axis=0)
    cos_m, sin_m = _rope_tables(jnp.maximum(jnp.arange(BLOCK) - pad, 0))
    u_m, q_m, k_m, v_m = _in_proj(xm, g_mix, w_in_bf, cos_m, sin_m, qg, kg, ones_bd, nb=1, tm=BLOCK)
    u_m8 = jnp.broadcast_to(u_m[:, None, :], (BLOCK, SUBLANES, D_SSM)).reshape(BLOCK * SUBLANES, D_SSM)
    zero_state = jnp.zeros((SUBLANES, D_STATE), F32)
    ys_m8, hre_m, him_m = _ssm(u_m8, zero_state, zero_state, *ssm_w, nbatch=SUBLANES, tt=32)
    ys_m = ys_m8.reshape(BLOCK, SUBLANES, D_SSM)[:, 0, :]
    zero_kv = jnp.zeros((1, BLOCK, D_KV), F32)
    ya_m = _attn_seq(sinks, q_m, k_m, v_m, zero_kv, zero_kv, g_att, nb=1, n_meta=n_meta, first_block=0)
    x1_m = _out_proj(xm, ys_m, ya_m, w_out_bf, nb=1, tm=BLOCK)
    zero_buf = jnp.zeros((1, SUBLANES, D_FF), F32)
    _, bufv_m, bufg_m = _ffn_seq(x1_m, g_ffn, w_up_bf, cw, cb, w_down_bf, zero_buf, zero_buf, nb=1, tm=BLOCK)

    tm = 256
    xp = x_prompt.reshape(nb * seq, D_MODEL)
    cos_p, sin_p = _rope_tables(n_meta + jnp.arange(seq))
    u_p, q_p, k_p, v_p = _in_proj(xp, g_mix, w_in_bf, cos_p, sin_p, qg, kg, ones_bd, nb=nb, tm=tm)
    ys_p, hre_p, him_p = _ssm(u_p.reshape(seq * nb, D_SSM), hre_m, him_m, *ssm_w, nbatch=nb, tt=32)
    k0 = jnp.broadcast_to(k_m[None], (nb, BLOCK, D_KV))
    v0 = jnp.broadcast_to(v_m[None], (nb, BLOCK, D_KV))
    ya_p = _attn_seq(sinks, q_p, k_p, v_p, k0, v0, g_att, nb=nb, n_meta=n_meta, first_block=1)
    x1_p = _out_proj(xp, ys_p.reshape(seq, nb * D_SSM), ya_p, w_out_bf, nb=nb, tm=tm)
    bufv0 = jnp.broadcast_to(bufv_m, (nb, SUBLANES, D_FF))
    bufg0 = jnp.broadcast_to(bufg_m, (nb, SUBLANES, D_FF))
    y_p, bufv_p, bufg_p = _ffn_seq(x1_p, g_ffn, w_up_bf, cw, cb, w_down_bf, bufv0, bufg0, nb=nb, tm=512)

    y_prompt = y_p.reshape(nb, seq, D_MODEL)
    k_p3 = k_p.reshape(nb, seq, N_KV_HEADS, HEAD_DIM)
    v_p3 = v_p.reshape(nb, seq, N_KV_HEADS, HEAD_DIM)
    prompt_k = k_p3[:, seq - WINDOW:][None]
    prompt_v = v_p3[:, seq - WINDOW:][None]
    prompt_re = hre_p.reshape(nb, N_SSM_GROUPS, SSM_STATE)[None]
    prompt_im = him_p.reshape(nb, N_SSM_GROUPS, SSM_STATE)[None]
    prompt_conv = jnp.concatenate([bufv_p[:, SUBLANES - 2:], bufg_p[:, SUBLANES - 2:]], axis=-1)[None]

    xs = x_sample.reshape(nseq, D_MODEL)
    cos_s, sin_s = _rope_tables(jnp.full((nseq,), PAST_LEN, jnp.int32))
    u_s, q_s, k_s, v_s = _in_proj(xs, g_mix, w_in_bf, cos_s, sin_s, qg, kg, ones_bd, nb=1, tm=nseq)
    ys_s, hre_s, him_s = _ssm(u_s, state_ssm_re[l].reshape(nseq, D_STATE), state_ssm_im[l].reshape(nseq, D_STATE),
                              *ssm_w, nbatch=nseq, tt=1)
    ya_s, nk_s, nv_s = _attn_step(
        q_s.reshape(nseq, N_HEADS, HEAD_DIM), k_s.reshape(nseq, 1, D_KV), v_s.reshape(nseq, 1, D_KV),
        cache_k[l].reshape(nseq, WINDOW, D_KV), cache_v[l].reshape(nseq, WINDOW, D_KV),
        sinks[:, None], attn_out_g[l].reshape(N_HEADS, HEAD_DIM), bb=8)
    x1_s = _out_proj(xs, ys_s, ya_s.reshape(nseq, D_ATTN), w_out_bf, nb=1, tm=nseq)
    buf4 = state_conv[l].reshape(nseq, 2 * (CONV_W - 1), D_FF)
    y_s, nbuf4 = _ffn_step(x1_s, g_ffn, w_up_bf, cw, cb, w_down_bf, buf4)

    y_sample = y_s.reshape(nseq, 1, D_MODEL)
    sample_k = nk_s.reshape(nseq, WINDOW, N_KV_HEADS, HEAD_DIM)[None]
    sample_v = nv_s.reshape(nseq, WINDOW, N_KV_HEADS, HEAD_DIM)[None]
    sample_re = hre_s.reshape(nseq, N_SSM_GROUPS, SSM_STATE)[None]
    sample_im = him_s.reshape(nseq, N_SSM_GROUPS, SSM_STATE)[None]
    sample_conv = nbuf4.reshape(nseq, CONV_W - 1, 2 * D_FF)[None]

    return (y_prompt, y_sample, prompt_k, prompt_v, prompt_re, prompt_im, prompt_conv,
            sample_k, sample_v, sample_re, sample_im, sample_conv)
```

```python
import functools

import jax
import jax.numpy as jnp
from jax import lax
from jax.experimental import pallas as pl
from jax.experimental.pallas import tpu as pltpu

F32 = jnp.float32
BF16 = jnp.bfloat16

D_MODEL = 2048
D_SSM = 1024
SSM_GROUP = 16
N_SSM_GROUPS = 64
SSM_STATE = 64
D_STATE = N_SSM_GROUPS * SSM_STATE
HEAD_DIM = 64
N_HEADS = 16
N_KV_HEADS = 4
Q_PER_KV = 4
D_ATTN = 1024
D_KV = 256
D_IN = D_SSM + D_ATTN + 2 * D_KV
WINDOW = 128
BLOCK = 128
ROPE_THETA = 10000.0
D_FF = 5632
CONV_W = 3
EPS = 1e-6
NEG = -1e30
PAST_LEN = 16384

SUBLANES = 8
LANES = 128
VMEM_LIMIT = 56 * 1024 * 1024
FF_TILE = 512
FFN_ROW_SPLIT = 2
IN_SUB_ROWS = 256
SCAN_LANES = 512
SCAN_STEPS = 32
SSM_GROUPS_PER_DOT = 16


def _const_spec(shape):
    nd = len(shape)
    return pl.BlockSpec(shape, lambda *_: (0,) * nd, pipeline_mode=pl.Buffered(1))


def _rms(x, g):
    return x * lax.rsqrt(jnp.mean(x * x, axis=-1, keepdims=True) + EPS) * g


def _mm(a, b):
    return jnp.dot(a, b, preferred_element_type=F32)


def _in_proj_kernel(x_ref, g_ref, w_ref, cos_ref, sin_ref, qg_ref, kg_ref, ones_ref,
                    u_ref, q_ref, k_ref, v_ref, *, nb, tq, n_split):
    rs = nb * tq // n_split
    cos1 = jnp.concatenate([cos_ref[...], cos_ref[...]], axis=1)
    sin1 = jnp.concatenate([sin_ref[...], sin_ref[...]], axis=1)
    ones = ones_ref[...]
    n_q = D_ATTN // 256
    for s in range(n_split):
        segs = []
        for b in range(nb):
            lo = max(s * rs, b * tq)
            hi = min((s + 1) * rs, (b + 1) * tq)
            if lo < hi:
                segs.append((b, lo - b * tq, hi - lo, lo - s * rs))
        x = jnp.concatenate([x_ref[b, t0:t0 + n, :] for b, t0, n, _ in segs], axis=0)
        cos = jnp.concatenate([cos1[t0:t0 + n, :] for _, t0, n, _ in segs], axis=0)
        sin = jnp.concatenate([sin1[t0:t0 + n, :] for _, t0, n, _ in segs], axis=0)
        lane = lax.broadcasted_iota(jnp.int32, cos.shape, 1)
        first_half = (lane % HEAD_DIM) < (HEAD_DIM // 2)
        h = _rms(x, g_ref[...]).astype(BF16)
        z = _mm(h, w_ref[...])
        for b, t0, n, r0 in segs:
            u_ref[t0:t0 + n, b, :] = z[r0:r0 + n, :D_SSM]
            v_ref[b, t0:t0 + n, :] = z[r0:r0 + n, D_SSM + D_ATTN + D_KV:]
        for c in range(n_q + D_KV // 256):
            zc = z[:, D_SSM + 256 * c: D_SSM + 256 * (c + 1)]
            sq = zc * zc
            hi = sq.astype(BF16)
            lo = (sq - hi.astype(F32)).astype(BF16)
            ssum = _mm(hi, ones) + _mm(lo, ones)
            gain = qg_ref[...] if c < n_q else kg_ref[...]
            nrm = zc * lax.rsqrt(ssum * (1.0 / HEAD_DIM) + EPS) * gain
            partner = jnp.where(first_half, pltpu.roll(nrm, 256 - HEAD_DIM // 2, 1),
                                pltpu.roll(nrm, HEAD_DIM // 2, 1))
            r = nrm * cos + partner * sin
            for b, t0, n, r0 in segs:
                if c < n_q:
                    q_ref[b, t0:t0 + n, 256 * c: 256 * (c + 1)] = (
                        r[r0:r0 + n, :] * (HEAD_DIM ** -0.5)).astype(BF16)
                else:
                    k_ref[b, t0:t0 + n, :] = r[r0:r0 + n, :]


def _in_proj(x3, g, w_bf, cos, sin, qg, kg, ones_bd, *, tq):
    nb, t_len, _ = x3.shape
    blk = lambda t: (0, t, 0)
    kern = functools.partial(_in_proj_kernel, nb=nb, tq=tq, n_split=max(1, nb * tq // IN_SUB_ROWS))
    return pl.pallas_call(
        kern,
        grid=(t_len // tq,),
        in_specs=[
            pl.BlockSpec((nb, tq, D_MODEL), blk),
            _const_spec((1, D_MODEL)),
            _const_spec((D_MODEL, D_IN)),
            pl.BlockSpec((tq, LANES), lambda t: (t, 0)),
            pl.BlockSpec((tq, LANES), lambda t: (t, 0)),
            _const_spec((1, 256)),
            _const_spec((1, 256)),
            _const_spec((256, 256)),
        ],
        out_specs=[
            pl.BlockSpec((tq, nb, D_SSM), lambda t: (t, 0, 0)),
            pl.BlockSpec((nb, tq, D_ATTN), blk),
            pl.BlockSpec((nb, tq, D_KV), blk),
            pl.BlockSpec((nb, tq, D_KV), blk),
        ],
        out_shape=[
            jax.ShapeDtypeStruct((t_len, nb, D_SSM), F32),
            jax.ShapeDtypeStruct((nb, t_len, D_ATTN), BF16),
            jax.ShapeDtypeStruct((nb, t_len, D_KV), F32),
            jax.ShapeDtypeStruct((nb, t_len, D_KV), F32),
        ],
        compiler_params=pltpu.CompilerParams(
            dimension_semantics=("arbitrary",), vmem_limit_bytes=VMEM_LIMIT),
        name="in_proj",
    )(x3, g, w_bf, cos, sin, qg, kg, ones_bd)


def _ssm_kernel(u_ref, h0re_ref, h0im_ref, lamre_ref, lamim_ref, logdt_ref, bblk_ref,
                cre_ref, cim_ref, dskip_ref, wglu_ref, bglu_ref, g_ref, perm_ref,
                y_ref, sre_ref, sim_ref,
                are_s, aim_s, cfre_s, cfim_s, xre_s, xim_s, stre_s, stim_s, *, tt, nbg):
    step = pl.program_id(0)
    lanes_per_dot = SSM_GROUPS_PER_DOT * SSM_STATE
    chans_per_dot = SSM_GROUPS_PER_DOT * SSM_GROUP
    n_dots = N_SSM_GROUPS // SSM_GROUPS_PER_DOT

    @pl.when(step == 0)
    def _():
        lr = lamre_ref[...]
        li = lamim_ref[...]
        dt = jnp.exp(logdt_ref[...])
        mag = jnp.exp(lr * dt)
        a_re = mag * jnp.cos(li * dt)
        a_im = mag * jnp.sin(li * dt)
        den = lr * lr + li * li
        am1 = a_re - 1.0
        shape = (SUBLANES, D_STATE)
        are_s[...] = jnp.broadcast_to(a_re, shape)
        aim_s[...] = jnp.broadcast_to(a_im, shape)
        cfre_s[...] = jnp.broadcast_to((am1 * lr + a_im * li) / den, shape)
        cfim_s[...] = jnp.broadcast_to((a_im * lr - am1 * li) / den, shape)
        stre_s[...] = h0re_ref[...]
        stim_s[...] = h0im_ref[...]

    u = u_ref[...]
    ub = u.astype(BF16)
    for c in range(n_dots):
        bu = _mm(ub[:, chans_per_dot * c: chans_per_dot * (c + 1)], bblk_ref[c])
        b_re = bu[:, :lanes_per_dot]
        b_im = bu[:, lanes_per_dot:]
        ls = slice(lanes_per_dot * c, lanes_per_dot * (c + 1))
        cf_re = cfre_s[0:1, ls]
        cf_im = cfim_s[0:1, ls]
        xre_s[:, ls] = cf_re * b_re - cf_im * b_im
        xim_s[:, ls] = cf_re * b_im + cf_im * b_re

    for bg in range(nbg):
        rs = slice(SUBLANES * bg, SUBLANES * (bg + 1))
        for lb in range(D_STATE // SCAN_LANES):
            ls = slice(SCAN_LANES * lb, SCAN_LANES * (lb + 1))
            a_re = are_s[:, ls]
            a_im = aim_s[:, ls]

            def body(t, carry, bg=bg, ls=ls, a_re=a_re, a_im=a_im):
                s_re, s_im = carry
                r0 = (t * nbg + bg) * SUBLANES
                if not isinstance(r0, int):
                    r0 = pl.multiple_of(r0, SUBLANES)
                n_re = a_re * s_re - a_im * s_im + xre_s[pl.ds(r0, SUBLANES), ls]
                n_im = a_re * s_im + a_im * s_re + xim_s[pl.ds(r0, SUBLANES), ls]
                xre_s[pl.ds(r0, SUBLANES), ls] = n_re
                xim_s[pl.ds(r0, SUBLANES), ls] = n_im
                return n_re, n_im

            carry = (stre_s[rs, ls], stim_s[rs, ls])
            if tt == 1:
                carry = body(0, carry)
            else:
                carry = lax.fori_loop(0, tt, body, carry, unroll=8)
            stre_s[rs, ls] = carry[0]
            stim_s[rs, ls] = carry[1]

    ys = []
    for c in range(n_dots):
        ls = slice(lanes_per_dot * c, lanes_per_dot * (c + 1))
        cs = slice(chans_per_dot * c, chans_per_dot * (c + 1))
        yc = _mm(xre_s[:, ls].astype(BF16), cre_ref[c]) - _mm(xim_s[:, ls].astype(BF16), cim_ref[c])
        ys.append(jax.nn.gelu(yc + dskip_ref[:, cs] * u[:, cs]))
    y = jnp.concatenate(ys, axis=1)
    y = y * jax.nn.sigmoid(_mm(y.astype(BF16), wglu_ref[...]) + bglu_ref[...])
    yn = _rms(y, g_ref[...]).astype(BF16)
    if tt == 1:
        y_ref[0] = yn
    else:
        y_ref[...] = _mm(perm_ref[...], yn).astype(BF16).reshape(y_ref.shape)

    @pl.when(step == pl.num_programs(0) - 1)
    def _():
        sre_ref[...] = stre_s[...]
        sim_ref[...] = stim_s[...]


def _ssm(u_tm, h0re, h0im, lamre, lamim, logdt, bblk, cre, cim, dskip, wglu, bglu, g, *, nbatch, tt):
    n = u_tm.shape[0]
    t_len = n // nbatch
    nbg = nbatch // SUBLANES
    rows = tt * nbatch
    n_grp = nbatch if tt > 1 else 1
    r = jnp.arange(rows)
    src = (r % tt) * nbatch + r // tt
    perm = (src[:, None] == r[None, :]).astype(BF16)
    kern = functools.partial(_ssm_kernel, tt=tt, nbg=nbg)
    return pl.pallas_call(
        kern,
        grid=(t_len // tt,),
        in_specs=[
            pl.BlockSpec((rows, D_SSM), lambda s: (s, 0)),
            _const_spec((nbatch, D_STATE)),
            _const_spec((nbatch, D_STATE)),
            _const_spec((1, D_STATE)),
            _const_spec((1, D_STATE)),
            _const_spec((1, D_STATE)),
            _const_spec(bblk.shape),
            _const_spec(cre.shape),
            _const_spec(cim.shape),
            _const_spec((1, D_SSM)),
            _const_spec((D_SSM, D_SSM)),
            _const_spec((1, D_SSM)),
            _const_spec((1, D_SSM)),
            _const_spec((rows, rows)),
        ],
        out_specs=[
            pl.BlockSpec((n_grp, rows // n_grp, D_SSM), lambda s: (0, s, 0)),
            pl.BlockSpec((nbatch, D_STATE), lambda s: (0, 0)),
            pl.BlockSpec((nbatch, D_STATE), lambda s: (0, 0)),
        ],
        out_shape=[
            jax.ShapeDtypeStruct((n_grp, n // n_grp, D_SSM), BF16),
            jax.ShapeDtypeStruct((nbatch, D_STATE), F32),
            jax.ShapeDtypeStruct((nbatch, D_STATE), F32),
        ],
        scratch_shapes=[pltpu.VMEM((SUBLANES, D_STATE), F32)] * 4
        + [pltpu.VMEM((rows, D_STATE), F32)] * 2
        + [pltpu.VMEM((nbatch, D_STATE), F32)] * 2,
        compiler_params=pltpu.CompilerParams(
            dimension_semantics=("arbitrary",), vmem_limit_bytes=VMEM_LIMIT),
        name="ssm",
    )(u_tm, h0re, h0im, lamre, lamim, logdt, bblk, cre, cim, dskip, wglu, bglu, g, perm)


def _attn_seq_kernel(sink_ref, q_ref, kc_ref, kp_ref, k0_ref, vc_ref, vp_ref, v0_ref, g_ref, o_ref,
                     *, first_block, n_meta):
    i = pl.program_id(1)
    is_first = i == 0
    k2 = jnp.concatenate([jnp.where(is_first, k0_ref[0], kp_ref[0]), kc_ref[0]], axis=0)
    v2 = jnp.concatenate([jnp.where(is_first, v0_ref[0], vp_ref[0]), vc_ref[0]], axis=0)
    gb = i + first_block
    col_min = jnp.where(gb == 0, 2 * BLOCK - n_meta, jnp.where(gb == 1, BLOCK - n_meta, 0))
    r = lax.broadcasted_iota(jnp.int32, (BLOCK, 4 * BLOCK), 0)
    c = lax.broadcasted_iota(jnp.int32, (BLOCK, 4 * BLOCK), 1) % (2 * BLOCK)
    mask = (c > r) & (c <= r + WINDOW) & (c >= col_min)
    lo_half = lax.broadcasted_iota(jnp.int32, (2 * BLOCK, LANES), 1) < HEAD_DIM

    def pair_blocks(x2, kv):
        xc = x2[:, LANES * (kv // 2): LANES * (kv // 2 + 1)]
        xr = pltpu.roll(xc, HEAD_DIM, 1)
        zero = jnp.zeros_like(xc)
        if kv % 2 == 0:
            low, high = jnp.where(lo_half, xc, zero), jnp.where(lo_half, zero, xr)
        else:
            low, high = jnp.where(lo_half, xr, zero), jnp.where(lo_half, zero, xc)
        return jnp.concatenate([low, high], axis=0).astype(BF16)

    n_pairs = N_HEADS // 2
    pairs_per_kv = Q_PER_KV // 2
    kbd = [pair_blocks(k2, kv) for kv in range(N_KV_HEADS)]
    row_hi = lax.broadcasted_iota(jnp.int32, (4 * BLOCK, LANES), 0) >= 2 * BLOCK
    lane_hi = lax.broadcasted_iota(jnp.int32, (4 * BLOCK, LANES), 1) >= HEAD_DIM
    ones_cols = (row_hi == lane_hi).astype(BF16)
    vbd = [jnp.concatenate([pair_blocks(v2, kv), ones_cols], axis=1) for kv in range(N_KV_HEADS)]
    halves = (slice(0, 2 * BLOCK), slice(2 * BLOCK, 4 * BLOCK))
    s = [jnp.where(mask, lax.dot_general(q_ref[0, :, LANES * pr: LANES * (pr + 1)], kbd[pr // pairs_per_kv],
                                         (((1,), (1,)), ((), ())), preferred_element_type=F32), NEG)
         for pr in range(n_pairs)]
    m = [[jnp.maximum(jnp.max(s[pr][:, halves[hh]], axis=-1, keepdims=True), sink_ref[2 * pr + hh])
          for hh in range(2)] for pr in range(n_pairs)]
    p = [jnp.concatenate([jnp.exp(s[pr][:, halves[hh]] - m[pr][hh]) for hh in range(2)], axis=1).astype(BF16)
         for pr in range(n_pairs)]
    out_hi = lax.broadcasted_iota(jnp.int32, (BLOCK, LANES), 1) >= HEAD_DIM
    outs = []
    for pr in range(n_pairs):
        ov = _mm(p[pr], vbd[pr // pairs_per_kv])
        sink_w = jnp.where(out_hi, jnp.exp(sink_ref[2 * pr + 1] - m[pr][1]), jnp.exp(sink_ref[2 * pr] - m[pr][0]))
        outs.append(ov[:, :LANES] / (ov[:, LANES:] + sink_w))
    o_ref[0] = _rms(jnp.concatenate(outs, axis=1), g_ref[...]).astype(BF16)


def _attn_seq(sinks, q, k, v, k0, v0, g, *, n_meta, first_block):
    nb, t_len, _ = q.shape
    nblk = t_len // BLOCK
    cur = lambda b, i: (b, i, 0)
    prev = lambda b, i: (b, jnp.maximum(i - 1, 0), 0)
    init = lambda b, i: (b, 0, 0)
    kern = functools.partial(_attn_seq_kernel, first_block=first_block, n_meta=n_meta)
    return pl.pallas_call(
        kern,
        grid=(nb, nblk),
        in_specs=[
            pl.BlockSpec(memory_space=pltpu.SMEM),
            pl.BlockSpec((1, BLOCK, D_ATTN), cur),
            pl.BlockSpec((1, BLOCK, D_KV), cur),
            pl.BlockSpec((1, BLOCK, D_KV), prev),
            pl.BlockSpec((1, BLOCK, D_KV), init),
            pl.BlockSpec((1, BLOCK, D_KV), cur),
            pl.BlockSpec((1, BLOCK, D_KV), prev),
            pl.BlockSpec((1, BLOCK, D_KV), init),
            _const_spec((1, D_ATTN)),
        ],
        out_specs=pl.BlockSpec((1, BLOCK, D_ATTN), cur),
        out_shape=jax.ShapeDtypeStruct((nb, t_len, D_ATTN), BF16),
        compiler_params=pltpu.CompilerParams(
            dimension_semantics=("arbitrary", "arbitrary"), vmem_limit_bytes=VMEM_LIMIT),
        name="attn_seq",
    )(sinks, q, k, k, k0, v, v, v0, g)


def _attn_step_kernel(q_ref, kn_ref, vn_ref, ck_ref, cv_ref, sink_ref, g_ref, o_ref, ok_ref, ov_ref):
    ok_ref[:, 0:WINDOW - 1, :] = ck_ref[:, 1:WINDOW, :]
    ok_ref[:, WINDOW - 1:WINDOW, :] = kn_ref[...]
    ov_ref[:, 0:WINDOW - 1, :] = cv_ref[:, 1:WINDOW, :]
    ov_ref[:, WINDOW - 1:WINDOW, :] = vn_ref[...]
    kb = ok_ref[...].astype(BF16)
    vb = ov_ref[...].astype(BF16)
    q = q_ref[...]
    q4 = jnp.concatenate([q] * N_KV_HEADS, axis=-1)
    hrow = lax.broadcasted_iota(jnp.int32, (N_HEADS, D_KV), 0)
    hcol = lax.broadcasted_iota(jnp.int32, (N_HEADS, D_KV), 1)
    own = (hcol // HEAD_DIM) == (hrow // Q_PER_KV)
    qe = jnp.where(own[None], q4, jnp.zeros_like(q4))
    s = jnp.einsum('bhd,bwd->bhw', qe, kb, preferred_element_type=F32)
    sink = sink_ref[...][None]
    m = jnp.maximum(jnp.max(s, axis=-1, keepdims=True), sink)
    p = jnp.exp(s - m)
    den = jnp.sum(p, axis=-1, keepdims=True) + jnp.exp(sink - m)
    o = jnp.einsum('bhw,bwd->bhd', (p / den).astype(BF16), vb, preferred_element_type=F32)
    o = jnp.where(own[None], o, 0.0)
    o = (o[..., 0:HEAD_DIM] + o[..., HEAD_DIM:2 * HEAD_DIM]
         + o[..., 2 * HEAD_DIM:3 * HEAD_DIM] + o[..., 3 * HEAD_DIM:4 * HEAD_DIM])
    ms = jnp.sum(jnp.sum(o * o, axis=2, keepdims=True), axis=1, keepdims=True) * (1.0 / D_ATTN)
    o_ref[...] = (o * lax.rsqrt(ms + EPS) * g_ref[...][None]).astype(BF16)


def _attn_step(q3, kn, vn, ck, cv, sinks, g, *, bb):
    nseq = q3.shape[0]
    blk3 = lambda i: (i, 0, 0)
    return pl.pallas_call(
        _attn_step_kernel,
        grid=(nseq // bb,),
        in_specs=[
            pl.BlockSpec((bb, N_HEADS, HEAD_DIM), blk3),
            pl.BlockSpec((bb, 1, D_KV), blk3),
            pl.BlockSpec((bb, 1, D_KV), blk3),
            pl.BlockSpec((bb, WINDOW, D_KV), blk3),
            pl.BlockSpec((bb, WINDOW, D_KV), blk3),
            _const_spec((N_HEADS, 1)),
            _const_spec((N_HEADS, HEAD_DIM)),
        ],
        out_specs=[
            pl.BlockSpec((bb, N_HEADS, HEAD_DIM), blk3),
            pl.BlockSpec((bb, WINDOW, D_KV), blk3),
            pl.BlockSpec((bb, WINDOW, D_KV), blk3),
        ],
        out_shape=[
            jax.ShapeDtypeStruct((nseq, N_HEADS, HEAD_DIM), BF16),
            jax.ShapeDtypeStruct((nseq, WINDOW, D_KV), F32),
            jax.ShapeDtypeStruct((nseq, WINDOW, D_KV), F32),
        ],
        compiler_params=pltpu.CompilerParams(
            dimension_semantics=("arbitrary",), vmem_limit_bytes=VMEM_LIMIT),
        name="attn_step",
    )(q3, kn, vn, ck, cv, sinks, g)


def _out_proj_kernel(x_ref, ys_ref, ya_ref, w_ref, o_ref):
    o_ref[0] = (x_ref[0] + _mm(ys_ref[0], w_ref[0:D_SSM, :])
                + _mm(ya_ref[0], w_ref[D_SSM:D_SSM + D_ATTN, :]))


def _out_proj(x3, ys, ya, w_bf, *, tm):
    nb, t_len, _ = x3.shape
    blk = lambda b, t: (b, t, 0)
    return pl.pallas_call(
        _out_proj_kernel,
        grid=(nb, t_len // tm),
        in_specs=[
            pl.BlockSpec((1, tm, D_MODEL), blk),
            pl.BlockSpec((1, tm, D_SSM), blk),
            pl.BlockSpec((1, tm, D_ATTN), blk),
            _const_spec((D_SSM + D_ATTN, D_MODEL)),
        ],
        out_specs=pl.BlockSpec((1, tm, D_MODEL), blk),
        out_shape=jax.ShapeDtypeStruct((nb, t_len, D_MODEL), F32),
        compiler_params=pltpu.CompilerParams(
            dimension_semantics=("arbitrary", "arbitrary"), vmem_limit_bytes=VMEM_LIMIT),
        name="out_proj",
    )(x3, ys, ya, w_bf)


def _ffn_seq_kernel(x_ref, g_ref, wv_ref, wg_ref, cwv_ref, cwg_ref, cbv_ref, cbg_ref, wd_ref,
                    bufv_ref, bufg_ref, y_ref, nbv_ref, nbg_ref,
                    h_s, upv_s, upg_s, *, tm, tf):
    t = pl.program_id(1)
    j = pl.program_id(2)
    chan = pl.ds(pl.multiple_of(j * tf, tf), tf)

    @pl.when(j == 0)
    def _():
        x = x_ref[...]
        h_s[...] = _rms(x, g_ref[...]).astype(BF16)
        y_ref[...] = x

    @pl.when(t == 0)
    def _():
        upv_s[0:SUBLANES, :] = bufv_ref[0]
        upg_s[0:SUBLANES, :] = bufg_ref[0]

    @pl.when(t > 0)
    def _():
        upv_s[0:SUBLANES, :] = nbv_ref[0, :, chan]
        upg_s[0:SUBLANES, :] = nbg_ref[0, :, chan]

    rm = tm // FFN_ROW_SPLIT

    def up_proj(up_s, w_ref, r0):
        up_s[SUBLANES + r0:SUBLANES + r0 + rm, :] = _mm(h_s[r0:r0 + rm, :], w_ref[...])

    def conv(up_s, cw_ref, cb_ref, r0):
        return (cb_ref[...] + cw_ref[0:1, :] * up_s[SUBLANES - 2 + r0:SUBLANES - 2 + r0 + rm, :]
                + cw_ref[1:2, :] * up_s[SUBLANES - 1 + r0:SUBLANES - 1 + r0 + rm, :]
                + cw_ref[2:3, :] * up_s[SUBLANES + r0:SUBLANES + r0 + rm, :])

    def gate_down(r0):
        val = conv(upv_s, cwv_ref, cbv_ref, r0)
        gate = conv(upg_s, cwg_ref, cbg_ref, r0)
        act = (jax.nn.silu(gate) * val).astype(BF16)
        y_ref[r0:r0 + rm, :] += _mm(act, wd_ref[...])

    for r in range(FFN_ROW_SPLIT):
        up_proj(upv_s, wv_ref, r * rm)
        up_proj(upg_s, wg_ref, r * rm)
        if r > 0:
            gate_down((r - 1) * rm)
    gate_down((FFN_ROW_SPLIT - 1) * rm)
    nbv_ref[0, :, chan] = upv_s[tm:tm + SUBLANES, :]
    nbg_ref[0, :, chan] = upg_s[tm:tm + SUBLANES, :]


def _ffn_seq(x1, g, wup_bf, conv_w, conv_b, wdown_bf, bufv, bufg, *, nb, tm):
    n = x1.shape[0]
    nt = n // nb // tm
    tf = FF_TILE
    nj = D_FF // tf
    row = lambda b, t, j: (b * nt + t, 0)
    kern = functools.partial(_ffn_seq_kernel, tm=tm, tf=tf)
    return pl.pallas_call(
        kern,
        grid=(nb, nt, nj),
        in_specs=[
            pl.BlockSpec((tm, D_MODEL), row),
            _const_spec((1, D_MODEL)),
            pl.BlockSpec((D_MODEL, tf), lambda b, t, j: (0, j)),
            pl.BlockSpec((D_MODEL, tf), lambda b, t, j: (0, nj + j)),
            pl.BlockSpec((CONV_W, tf), lambda b, t, j: (0, j)),
            pl.BlockSpec((CONV_W, tf), lambda b, t, j: (0, nj + j)),
            pl.BlockSpec((1, tf), lambda b, t, j: (0, j)),
            pl.BlockSpec((1, tf), lambda b, t, j: (0, nj + j)),
            pl.BlockSpec((tf, D_MODEL), lambda b, t, j: (j, 0)),
            pl.BlockSpec((1, SUBLANES, tf), lambda b, t, j: (b, 0, j)),
            pl.BlockSpec((1, SUBLANES, tf), lambda b, t, j: (b, 0, j)),
        ],
        out_specs=[
            pl.BlockSpec((tm, D_MODEL), row),
            pl.BlockSpec((1, SUBLANES, D_FF), lambda b, t, j: (b, 0, 0)),
            pl.BlockSpec((1, SUBLANES, D_FF), lambda b, t, j: (b, 0, 0)),
        ],
        out_shape=[
            jax.ShapeDtypeStruct((n, D_MODEL), F32),
            jax.ShapeDtypeStruct((nb, SUBLANES, D_FF), F32),
            jax.ShapeDtypeStruct((nb, SUBLANES, D_FF), F32),
        ],
        scratch_shapes=[
            pltpu.VMEM((tm, D_MODEL), BF16),
            pltpu.VMEM((tm + SUBLANES, tf), F32),
            pltpu.VMEM((tm + SUBLANES, tf), F32),
        ],
        compiler_params=pltpu.CompilerParams(
            dimension_semantics=("arbitrary", "arbitrary", "arbitrary"), vmem_limit_bytes=VMEM_LIMIT),
        name="ffn_seq",
    )(x1, g, wup_bf, wup_bf, conv_w, conv_w, conv_b, conv_b, wdown_bf, bufv, bufg)


def _ffn_step_kernel(x_ref, g_ref, wv_ref, wg_ref, cwv_ref, cwg_ref, cbv_ref, cbg_ref, wd_ref,
                     buf_ref, y_ref, nbuf_ref, h_s):
    j = pl.program_id(0)

    @pl.when(j == 0)
    def _():
        x = x_ref[...]
        h_s[...] = _rms(x, g_ref[...]).astype(BF16)
        y_ref[...] = x

    h = h_s[...]
    upv = _mm(h, wv_ref[...])
    upg = _mm(h, wg_ref[...])
    val = cbv_ref[...] + cwv_ref[0:1, :] * buf_ref[:, 0, :] + cwv_ref[1:2, :] * buf_ref[:, 2, :] + cwv_ref[2:3, :] * upv
    gate = cbg_ref[...] + cwg_ref[0:1, :] * buf_ref[:, 1, :] + cwg_ref[1:2, :] * buf_ref[:, 3, :] + cwg_ref[2:3, :] * upg
    act = (jax.nn.silu(gate) * val).astype(BF16)
    y_ref[...] += _mm(act, wd_ref[...])
    nbuf_ref[:, 0, :] = buf_ref[:, 2, :]
    nbuf_ref[:, 1, :] = buf_ref[:, 3, :]
    nbuf_ref[:, 2, :] = upv
    nbuf_ref[:, 3, :] = upg


def _ffn_step(x1, g, wup_bf, conv_w, conv_b, wdown_bf, buf4):
    nseq = x1.shape[0]
    tf = FF_TILE
    nj = D_FF // tf
    return pl.pallas_call(
        _ffn_step_kernel,
        grid=(nj,),
        in_specs=[
            _const_spec((nseq, D_MODEL)),
            _const_spec((1, D_MODEL)),
            pl.BlockSpec((D_MODEL, tf), lambda j: (0, j)),
            pl.BlockSpec((D_MODEL, tf), lambda j: (0, nj + j)),
            pl.BlockSpec((CONV_W, tf), lambda j: (0, j)),
            pl.BlockSpec((CONV_W, tf), lambda j: (0, nj + j)),
            pl.BlockSpec((1, tf), lambda j: (0, j)),
            pl.BlockSpec((1, tf), lambda j: (0, nj + j)),
            pl.BlockSpec((tf, D_MODEL), lambda j: (j, 0)),
            pl.BlockSpec((nseq, 2 * (CONV_W - 1), tf), lambda j: (0, 0, j)),
        ],
        out_specs=[
            pl.BlockSpec((nseq, D_MODEL), lambda j: (0, 0)),
            pl.BlockSpec((nseq, 2 * (CONV_W - 1), tf), lambda j: (0, 0, j)),
        ],
        out_shape=[
            jax.ShapeDtypeStruct((nseq, D_MODEL), F32),
            jax.ShapeDtypeStruct((nseq, 2 * (CONV_W - 1), D_FF), F32),
        ],
        scratch_shapes=[pltpu.VMEM((nseq, D_MODEL), BF16)],
        compiler_params=pltpu.CompilerParams(
            dimension_semantics=("arbitrary",), vmem_limit_bytes=VMEM_LIMIT),
        name="ffn_step",
    )(x1, g, wup_bf, wup_bf, conv_w, conv_w, conv_b, conv_b, wdown_bf, buf4)


def _rope_tables(pos):
    half = HEAD_DIM // 2
    inv = ROPE_THETA ** (-jnp.arange(half, dtype=F32) * 2.0 / HEAD_DIM)
    ang = pos.astype(F32)[:, None] * inv[None, :]
    c = jnp.cos(ang)
    s = jnp.sin(ang)
    return jnp.concatenate([c, c, c, c], axis=1), jnp.concatenate([-s, s, -s, s], axis=1)


def _block_diag(w):
    nd, ng, r, c = w.shape
    eye = jnp.eye(ng, dtype=w.dtype)
    return jnp.einsum('dgrc,gh->dgrhc', w, eye).reshape(nd, ng * r, ng * c)


def kernel(x_prompt, x_sample, cache_k, cache_v, state_ssm_re, state_ssm_im, state_conv, meta_tokens,
           norm_mix_g, w_in, q_norm_g, k_norm_g, attn_sinks, lam_re, lam_im, log_dt, ssm_b_re, ssm_b_im,
           ssm_c_re, ssm_c_im, ssm_d, w_glu, b_glu, ssm_out_g, attn_out_g, w_out, norm_ffn_g, w_up,
           conv_w, conv_b, w_down):
    depth = w_in.shape[0]
    assert depth == 1
    nb, seq, _ = x_prompt.shape
    nseq = x_sample.shape[0]
    n_meta = meta_tokens.shape[0]
    assert x_sample.shape[1] == 1 and n_meta == 16 and nb == SUBLANES
    assert seq % 512 == 0 and nseq % SUBLANES == 0
    l = 0
    gpd = SSM_GROUPS_PER_DOT
    n_dots = N_SSM_GROUPS // gpd

    w_in_bf = w_in[l].astype(BF16)
    w_glu_bf = w_glu[l].astype(BF16)
    w_out_bf = w_out[l].astype(BF16)
    w_up_bf = w_up[l].astype(BF16)
    w_down_bf = w_down[l].astype(BF16)
    g_mix = norm_mix_g[l][None, :]
    g_ffn = norm_ffn_g[l][None, :]
    qg = jnp.tile(q_norm_g[l], 256 // HEAD_DIM)[None, :]
    kg = jnp.tile(k_norm_g[l], 256 // HEAD_DIM)[None, :]
    lane_head = jnp.arange(256) // HEAD_DIM
    ones_bd = (lane_head[:, None] == lane_head[None, :]).astype(BF16)
    sinks = attn_sinks[l]
    lamre = lam_re[l].reshape(1, D_STATE)
    lamim = lam_im[l].reshape(1, D_STATE)
    logdt = jnp.repeat(log_dt[l], SSM_STATE)[None, :]
    b_re_t = jnp.swapaxes(ssm_b_re[l], 1, 2).reshape(n_dots, gpd, SSM_GROUP, SSM_STATE)
    b_im_t = jnp.swapaxes(ssm_b_im[l], 1, 2).reshape(n_dots, gpd, SSM_GROUP, SSM_STATE)
    bblk = jnp.concatenate([_block_diag(b_re_t), _block_diag(b_im_t)], axis=2).astype(BF16)
    cre = _block_diag(jnp.swapaxes(ssm_c_re[l], 1, 2).reshape(n_dots, gpd, SSM_STATE, SSM_GROUP)).astype(BF16)
    cim = _block_diag(jnp.swapaxes(ssm_c_im[l], 1, 2).reshape(n_dots, gpd, SSM_STATE, SSM_GROUP)).astype(BF16)
    dskip = ssm_d[l].reshape(1, D_SSM)
    bglu = b_glu[l][None, :]
    g_ssm = ssm_out_g[l][None, :]
    g_att = attn_out_g[l][None, :]
    cw = conv_w[l]
    cb = conv_b[l][None, :]

    ssm_w = (lamre, lamim, logdt, bblk, cre, cim, dskip, w_glu_bf, bglu, g_ssm)

    pad = BLOCK - n_meta
    xm = jnp.concatenate([jnp.zeros((pad, D_MODEL), F32), meta_tokens.astype(F32)], axis=0)[None]
    cos_m, sin_m = _rope_tables(jnp.maximum(jnp.arange(BLOCK) - pad, 0))
    u_m, q_m, k_m, v_m = _in_proj(xm, g_mix, w_in_bf, cos_m, sin_m, qg, kg, ones_bd, tq=BLOCK)
    u_m8 = jnp.broadcast_to(u_m, (BLOCK, SUBLANES, D_SSM)).reshape(BLOCK * SUBLANES, D_SSM)
    zero_state = jnp.zeros((SUBLANES, D_STATE), F32)
    ys_m8, hre_m, him_m = _ssm(u_m8, zero_state, zero_state, *ssm_w, nbatch=SUBLANES, tt=SCAN_STEPS)
    zero_kv = jnp.zeros((1, BLOCK, D_KV), F32)
    ya_m = _attn_seq(sinks, q_m, k_m, v_m, zero_kv, zero_kv, g_att, n_meta=n_meta, first_block=0)
    x1_m = _out_proj(xm, ys_m8[0:1], ya_m, w_out_bf, tm=BLOCK)
    zero_buf = jnp.zeros((1, SUBLANES, D_FF), F32)
    _, bufv_m, bufg_m = _ffn_seq(x1_m[0], g_ffn, w_up_bf, cw, cb, w_down_bf, zero_buf, zero_buf, nb=1, tm=BLOCK)

    cos_p, sin_p = _rope_tables(n_meta + jnp.arange(seq))
    u_p, q_p, k_p, v_p = _in_proj(x_prompt, g_mix, w_in_bf, cos_p, sin_p, qg, kg, ones_bd, tq=128)
    ys_p, hre_p, him_p = _ssm(u_p.reshape(seq * nb, D_SSM), hre_m, him_m, *ssm_w, nbatch=nb, tt=SCAN_STEPS)
    k0 = jnp.broadcast_to(k_m, (nb, BLOCK, D_KV))
    v0 = jnp.broadcast_to(v_m, (nb, BLOCK, D_KV))
    ya_p = _attn_seq(sinks, q_p, k_p, v_p, k0, v0, g_att, n_meta=n_meta, first_block=1)
    x1_p = _out_proj(x_prompt, ys_p, ya_p, w_out_bf, tm=512)
    bufv0 = jnp.broadcast_to(bufv_m, (nb, SUBLANES, D_FF))
    bufg0 = jnp.broadcast_to(bufg_m, (nb, SUBLANES, D_FF))
    y_p, bufv_p, bufg_p = _ffn_seq(x1_p.reshape(nb * seq, D_MODEL), g_ffn, w_up_bf, cw, cb, w_down_bf,
                                   bufv0, bufg0, nb=nb, tm=512)

    y_prompt = y_p.reshape(nb, seq, D_MODEL)
    prompt_k = k_p[:, seq - WINDOW:].reshape(nb, WINDOW, N_KV_HEADS, HEAD_DIM)[None]
    prompt_v = v_p[:, seq - WINDOW:].reshape(nb, WINDOW, N_KV_HEADS, HEAD_DIM)[None]
    prompt_re = hre_p.reshape(nb, N_SSM_GROUPS, SSM_STATE)[None]
    prompt_im = him_p.reshape(nb, N_SSM_GROUPS, SSM_STATE)[None]
    prompt_conv = jnp.concatenate([bufv_p[:, SUBLANES - 2:], bufg_p[:, SUBLANES - 2:]], axis=-1)[None]

    xs = x_sample.reshape(1, nseq, D_MODEL)
    cos_s, sin_s = _rope_tables(jnp.full((nseq,), PAST_LEN, jnp.int32))
    u_s, q_s, k_s, v_s = _in_proj(xs, g_mix, w_in_bf, cos_s, sin_s, qg, kg, ones_bd, tq=nseq)
    ys_s, hre_s, him_s = _ssm(u_s.reshape(nseq, D_SSM), state_ssm_re[l].reshape(nseq, D_STATE),
                              state_ssm_im[l].reshape(nseq, D_STATE), *ssm_w, nbatch=nseq, tt=1)
    ya_s, nk_s, nv_s = _attn_step(
        q_s.reshape(nseq, N_HEADS, HEAD_DIM), k_s.reshape(nseq, 1, D_KV), v_s.reshape(nseq, 1, D_KV),
        cache_k[l].reshape(nseq, WINDOW, D_KV), cache_v[l].reshape(nseq, WINDOW, D_KV),
        sinks[:, None], attn_out_g[l].reshape(N_HEADS, HEAD_DIM), bb=8)
    x1_s = _out_proj(xs, ys_s, ya_s.reshape(1, nseq, D_ATTN), w_out_bf, tm=nseq)
    buf4 = state_conv[l].reshape(nseq, 2 * (CONV_W - 1), D_FF)
    y_s, nbuf4 = _ffn_step(x1_s[0], g_ffn, w_up_bf, cw, cb, w_down_bf, buf4)

    y_sample = y_s.reshape(nseq, 1, D_MODEL)
    sample_k = nk_s.reshape(nseq, WINDOW, N_KV_HEADS, HEAD_DIM)[None]
    sample_v = nv_s.reshape(nseq, WINDOW, N_KV_HEADS, HEAD_DIM)[None]
    sample_re = hre_s.reshape(nseq, N_SSM_GROUPS, SSM_STATE)[None]
    sample_im = him_s.reshape(nseq, N_SSM_GROUPS, SSM_STATE)[None]
    sample_conv = nbuf4.reshape(nseq, CONV_W - 1, 2 * D_FF)[None]

    return (y_prompt, y_sample, prompt_k, prompt_v, prompt_re, prompt_im, prompt_conv,
            sample_k, sample_v, sample_re, sample_im, sample_conv)
```

```python
import functools

import jax
import jax.numpy as jnp
from jax import lax
from jax.experimental import pallas as pl
from jax.experimental.pallas import tpu as pltpu

F32 = jnp.float32
BF16 = jnp.bfloat16

D_MODEL = 2048
D_SSM = 1024
SSM_GROUP = 16
N_SSM_GROUPS = 64
SSM_STATE = 64
D_STATE = N_SSM_GROUPS * SSM_STATE
HEAD_DIM = 64
N_HEADS = 16
N_KV_HEADS = 4
Q_PER_KV = 4
D_ATTN = 1024
D_KV = 256
D_IN = D_SSM + D_ATTN + 2 * D_KV
WINDOW = 128
BLOCK = 128
ROPE_THETA = 10000.0
D_FF = 5632
CONV_W = 3
EPS = 1e-6
NEG = -1e30
PAST_LEN = 16384

SUBLANES = 8
LANES = 128
VMEM_LIMIT = 60 * 1024 * 1024
FF_TILE = 512
FFN_SUB_ROWS = 256
IN_SUB_ROWS = 256
SCAN_LANES = 512
SCAN_STEPS = 32
SSM_GROUPS_PER_DOT = 16


def _const_spec(shape):
    nd = len(shape)
    return pl.BlockSpec(shape, lambda *_: (0,) * nd, pipeline_mode=pl.Buffered(1))


def _rms(x, g):
    return x * lax.rsqrt(jnp.mean(x * x, axis=-1, keepdims=True) + EPS) * g


def _mm(a, b):
    return jnp.dot(a, b, preferred_element_type=F32)


def _in_proj_kernel(x_ref, g_ref, w_ref, cos_ref, sin_ref, qg_ref, kg_ref, ones_ref,
                    u_ref, q_ref, k_ref, v_ref, *, nb, tq, n_split):
    rs = nb * tq // n_split
    cos1 = jnp.concatenate([cos_ref[...], cos_ref[...]], axis=1)
    sin1 = jnp.concatenate([sin_ref[...], sin_ref[...]], axis=1)
    ones = ones_ref[...]
    n_q = D_ATTN // 256
    for s in range(n_split):
        segs = []
        for b in range(nb):
            lo = max(s * rs, b * tq)
            hi = min((s + 1) * rs, (b + 1) * tq)
            if lo < hi:
                segs.append((b, lo - b * tq, hi - lo, lo - s * rs))
        x = jnp.concatenate([x_ref[b, t0:t0 + n, :] for b, t0, n, _ in segs], axis=0)
        cos = jnp.concatenate([cos1[t0:t0 + n, :] for _, t0, n, _ in segs], axis=0)
        sin = jnp.concatenate([sin1[t0:t0 + n, :] for _, t0, n, _ in segs], axis=0)
        lane = lax.broadcasted_iota(jnp.int32, cos.shape, 1)
        first_half = (lane % HEAD_DIM) < (HEAD_DIM // 2)
        h = _rms(x, g_ref[...]).astype(BF16)
        z = _mm(h, w_ref[...])
        for b, t0, n, r0 in segs:
            u_ref[t0:t0 + n, b, :] = z[r0:r0 + n, :D_SSM]
            v_ref[b, t0:t0 + n, :] = z[r0:r0 + n, D_SSM + D_ATTN + D_KV:]
        for c in range(n_q + D_KV // 256):
            zc = z[:, D_SSM + 256 * c: D_SSM + 256 * (c + 1)]
            sq = zc * zc
            hi = sq.astype(BF16)
            lo = (sq - hi.astype(F32)).astype(BF16)
            ssum = _mm(hi, ones) + _mm(lo, ones)
            gain = qg_ref[...] if c < n_q else kg_ref[...]
            nrm = zc * lax.rsqrt(ssum * (1.0 / HEAD_DIM) + EPS) * gain
            partner = jnp.where(first_half, pltpu.roll(nrm, 256 - HEAD_DIM // 2, 1),
                                pltpu.roll(nrm, HEAD_DIM // 2, 1))
            r = nrm * cos + partner * sin
            for b, t0, n, r0 in segs:
                if c < n_q:
                    q_ref[b, t0:t0 + n, 256 * c: 256 * (c + 1)] = (
                        r[r0:r0 + n, :] * (HEAD_DIM ** -0.5)).astype(BF16)
                else:
                    k_ref[b, t0:t0 + n, :] = r[r0:r0 + n, :]


def _in_proj(x3, g, w_bf, cos, sin, qg, kg, ones_bd, *, tq):
    nb, t_len, _ = x3.shape
    blk = lambda t: (0, t, 0)
    kern = functools.partial(_in_proj_kernel, nb=nb, tq=tq, n_split=max(1, nb * tq // IN_SUB_ROWS))
    return pl.pallas_call(
        kern,
        grid=(t_len // tq,),
        in_specs=[
            pl.BlockSpec((nb, tq, D_MODEL), blk),
            _const_spec((1, D_MODEL)),
            _const_spec((D_MODEL, D_IN)),
            pl.BlockSpec((tq, LANES), lambda t: (t, 0)),
            pl.BlockSpec((tq, LANES), lambda t: (t, 0)),
            _const_spec((1, 256)),
            _const_spec((1, 256)),
            _const_spec((256, 256)),
        ],
        out_specs=[
            pl.BlockSpec((tq, nb, D_SSM), lambda t: (t, 0, 0)),
            pl.BlockSpec((nb, tq, D_ATTN), blk),
            pl.BlockSpec((nb, tq, D_KV), blk),
            pl.BlockSpec((nb, tq, D_KV), blk),
        ],
        out_shape=[
            jax.ShapeDtypeStruct((t_len, nb, D_SSM), F32),
            jax.ShapeDtypeStruct((nb, t_len, D_ATTN), BF16),
            jax.ShapeDtypeStruct((nb, t_len, D_KV), F32),
            jax.ShapeDtypeStruct((nb, t_len, D_KV), F32),
        ],
        compiler_params=pltpu.CompilerParams(
            dimension_semantics=("arbitrary",), vmem_limit_bytes=VMEM_LIMIT),
        name="in_proj",
    )(x3, g, w_bf, cos, sin, qg, kg, ones_bd)


def _ssm_kernel(u_ref, h0re_ref, h0im_ref, lamre_ref, lamim_ref, logdt_ref, bblk_ref,
                cre_ref, cim_ref, dskip_ref, wglu_ref, bglu_ref, g_ref, perm_ref,
                y_ref, sre_ref, sim_ref,
                are_s, aim_s, bbar_s, xre_s, xim_s, stre_s, stim_s, *, tt, nbg):
    step = pl.program_id(0)
    lanes_per_dot = SSM_GROUPS_PER_DOT * SSM_STATE
    chans_per_dot = SSM_GROUPS_PER_DOT * SSM_GROUP
    n_dots = N_SSM_GROUPS // SSM_GROUPS_PER_DOT
    state_lanes = [slice(lanes_per_dot * c, lanes_per_dot * (c + 1)) for c in range(n_dots)]
    chans = [slice(chans_per_dot * c, chans_per_dot * (c + 1)) for c in range(n_dots)]

    @pl.when(step == 0)
    def _():
        lr = lamre_ref[...]
        li = lamim_ref[...]
        dt = jnp.exp(logdt_ref[...])
        mag = jnp.exp(lr * dt)
        a_re = mag * jnp.cos(li * dt)
        a_im = mag * jnp.sin(li * dt)
        den = lr * lr + li * li
        am1 = a_re - 1.0
        cf_re = (am1 * lr + a_im * li) / den
        cf_im = (a_im * lr - am1 * li) / den
        shape = (SUBLANES, D_STATE)
        are_s[...] = jnp.broadcast_to(a_re, shape)
        aim_s[...] = jnp.broadcast_to(a_im, shape)
        for c in range(n_dots):
            b_re = bblk_ref[c, :, :lanes_per_dot]
            b_im = bblk_ref[c, :, lanes_per_dot:]
            c_re = cf_re[:, state_lanes[c]]
            c_im = cf_im[:, state_lanes[c]]
            bbar_s[c, :, :lanes_per_dot] = (c_re * b_re - c_im * b_im).astype(BF16)
            bbar_s[c, :, lanes_per_dot:] = (c_re * b_im + c_im * b_re).astype(BF16)
        stre_s[...] = h0re_ref[...]
        stim_s[...] = h0im_ref[...]

    u = u_ref[...]
    ub = u.astype(BF16)

    def project(c):
        bu = _mm(ub[:, chans[c]], bbar_s[c])
        xre_s[:, state_lanes[c]] = bu[:, :lanes_per_dot]
        xim_s[:, state_lanes[c]] = bu[:, lanes_per_dot:]

    def scan(c):
        blocks = [slice(lanes_per_dot * c + SCAN_LANES * i, lanes_per_dot * c + SCAN_LANES * (i + 1))
                  for i in range(lanes_per_dot // SCAN_LANES)]
        for bg in range(nbg):
            rs = slice(SUBLANES * bg, SUBLANES * (bg + 1))
            a = [(are_s[:, ls], aim_s[:, ls]) for ls in blocks]
            st = [(stre_s[rs, ls], stim_s[rs, ls]) for ls in blocks]
            for t in range(tt):
                rt = slice((t * nbg + bg) * SUBLANES, (t * nbg + bg + 1) * SUBLANES)
                for i, ls in enumerate(blocks):
                    (a_re, a_im), (s_re, s_im) = a[i], st[i]
                    n_re = a_re * s_re - a_im * s_im + xre_s[rt, ls]
                    n_im = a_re * s_im + a_im * s_re + xim_s[rt, ls]
                    xre_s[rt, ls] = n_re
                    xim_s[rt, ls] = n_im
                    st[i] = (n_re, n_im)
            for i, ls in enumerate(blocks):
                stre_s[rs, ls] = st[i][0]
                stim_s[rs, ls] = st[i][1]

    def readout(c):
        yc = (_mm(xre_s[:, state_lanes[c]].astype(BF16), cre_ref[c])
              - _mm(xim_s[:, state_lanes[c]].astype(BF16), cim_ref[c]))
        return jax.nn.gelu(yc + dskip_ref[:, chans[c]] * u[:, chans[c]])

    ys = [None] * n_dots
    project(0)
    for c in range(n_dots):
        if c + 1 < n_dots:
            project(c + 1)
        scan(c)
        if c > 0:
            ys[c - 1] = readout(c - 1)
    ys[n_dots - 1] = readout(n_dots - 1)
    y = jnp.concatenate(ys, axis=1)
    y = y * jax.nn.sigmoid(_mm(y.astype(BF16), wglu_ref[...]) + bglu_ref[...])
    yn = _rms(y, g_ref[...]).astype(BF16)
    if tt == 1:
        y_ref[0] = yn
    else:
        y_ref[...] = _mm(perm_ref[...], yn).astype(BF16).reshape(y_ref.shape)

    @pl.when(step == pl.num_programs(0) - 1)
    def _():
        sre_ref[...] = stre_s[...]
        sim_ref[...] = stim_s[...]


def _ssm(u_tm, h0re, h0im, lamre, lamim, logdt, bblk, cre, cim, dskip, wglu, bglu, g, *, nbatch, tt):
    n = u_tm.shape[0]
    t_len = n // nbatch
    nbg = nbatch // SUBLANES
    rows = tt * nbatch
    n_grp = nbatch if tt > 1 else 1
    r = jnp.arange(rows)
    src = (r % tt) * nbatch + r // tt
    perm = (src[:, None] == r[None, :]).astype(BF16)
    kern = functools.partial(_ssm_kernel, tt=tt, nbg=nbg)
    return pl.pallas_call(
        kern,
        grid=(t_len // tt,),
        in_specs=[
            pl.BlockSpec((rows, D_SSM), lambda s: (s, 0)),
            _const_spec((nbatch, D_STATE)),
            _const_spec((nbatch, D_STATE)),
            _const_spec((1, D_STATE)),
            _const_spec((1, D_STATE)),
            _const_spec((1, D_STATE)),
            _const_spec(bblk.shape),
            _const_spec(cre.shape),
            _const_spec(cim.shape),
            _const_spec((1, D_SSM)),
            _const_spec((D_SSM, D_SSM)),
            _const_spec((1, D_SSM)),
            _const_spec((1, D_SSM)),
            _const_spec((rows, rows)),
        ],
        out_specs=[
            pl.BlockSpec((n_grp, rows // n_grp, D_SSM), lambda s: (0, s, 0)),
            pl.BlockSpec((nbatch, D_STATE), lambda s: (0, 0)),
            pl.BlockSpec((nbatch, D_STATE), lambda s: (0, 0)),
        ],
        out_shape=[
            jax.ShapeDtypeStruct((n_grp, n // n_grp, D_SSM), BF16),
            jax.ShapeDtypeStruct((nbatch, D_STATE), F32),
            jax.ShapeDtypeStruct((nbatch, D_STATE), F32),
        ],
        scratch_shapes=[pltpu.VMEM((SUBLANES, D_STATE), F32)] * 2
        + [pltpu.VMEM(bblk.shape, BF16)]
        + [pltpu.VMEM((rows, D_STATE), F32)] * 2
        + [pltpu.VMEM((nbatch, D_STATE), F32)] * 2,
        compiler_params=pltpu.CompilerParams(
            dimension_semantics=("arbitrary",), vmem_limit_bytes=VMEM_LIMIT),
        name="ssm",
    )(u_tm, h0re, h0im, lamre, lamim, logdt, bblk, cre, cim, dskip, wglu, bglu, g, perm)


def _attn_seq_kernel(sink_ref, q_ref, kc_ref, kp_ref, k0_ref, vc_ref, vp_ref, v0_ref, g_ref, o_ref,
                     *, first_block, n_meta):
    i = pl.program_id(1)
    is_first = i == 0
    k2 = jnp.concatenate([jnp.where(is_first, k0_ref[0], kp_ref[0]), kc_ref[0]], axis=0)
    v2 = jnp.concatenate([jnp.where(is_first, v0_ref[0], vp_ref[0]), vc_ref[0]], axis=0)
    gb = i + first_block
    col_min = jnp.where(gb == 0, 2 * BLOCK - n_meta, jnp.where(gb == 1, BLOCK - n_meta, 0))
    r = lax.broadcasted_iota(jnp.int32, (BLOCK, 4 * BLOCK), 0)
    c = lax.broadcasted_iota(jnp.int32, (BLOCK, 4 * BLOCK), 1) % (2 * BLOCK)
    mask = (c > r) & (c <= r + WINDOW) & (c >= col_min)
    lo_half = lax.broadcasted_iota(jnp.int32, (2 * BLOCK, LANES), 1) < HEAD_DIM

    def pair_blocks(x2, kv):
        xc = x2[:, LANES * (kv // 2): LANES * (kv // 2 + 1)]
        xr = pltpu.roll(xc, HEAD_DIM, 1)
        zero = jnp.zeros_like(xc)
        if kv % 2 == 0:
            low, high = jnp.where(lo_half, xc, zero), jnp.where(lo_half, zero, xr)
        else:
            low, high = jnp.where(lo_half, xr, zero), jnp.where(lo_half, zero, xc)
        return jnp.concatenate([low, high], axis=0).astype(BF16)

    n_pairs = N_HEADS // 2
    pairs_per_kv = Q_PER_KV // 2
    kbd = [pair_blocks(k2, kv) for kv in range(N_KV_HEADS)]
    row_hi = lax.broadcasted_iota(jnp.int32, (4 * BLOCK, LANES), 0) >= 2 * BLOCK
    lane_hi = lax.broadcasted_iota(jnp.int32, (4 * BLOCK, LANES), 1) >= HEAD_DIM
    ones_cols = (row_hi == lane_hi).astype(BF16)
    vbd = [jnp.concatenate([pair_blocks(v2, kv), ones_cols], axis=1) for kv in range(N_KV_HEADS)]
    halves = (slice(0, 2 * BLOCK), slice(2 * BLOCK, 4 * BLOCK))
    s = [jnp.where(mask, lax.dot_general(q_ref[0, :, LANES * pr: LANES * (pr + 1)], kbd[pr // pairs_per_kv],
                                         (((1,), (1,)), ((), ())), preferred_element_type=F32), NEG)
         for pr in range(n_pairs)]
    m = [[jnp.maximum(jnp.max(s[pr][:, halves[hh]], axis=-1, keepdims=True), sink_ref[2 * pr + hh])
          for hh in range(2)] for pr in range(n_pairs)]
    p = [jnp.concatenate([jnp.exp(s[pr][:, halves[hh]] - m[pr][hh]) for hh in range(2)], axis=1).astype(BF16)
         for pr in range(n_pairs)]
    out_hi = lax.broadcasted_iota(jnp.int32, (BLOCK, LANES), 1) >= HEAD_DIM
    outs = []
    for pr in range(n_pairs):
        ov = _mm(p[pr], vbd[pr // pairs_per_kv])
        sink_w = jnp.where(out_hi, jnp.exp(sink_ref[2 * pr + 1] - m[pr][1]), jnp.exp(sink_ref[2 * pr] - m[pr][0]))
        outs.append(ov[:, :LANES] / (ov[:, LANES:] + sink_w))
    o_ref[0] = _rms(jnp.concatenate(outs, axis=1), g_ref[...]).astype(BF16)


def _attn_seq(sinks, q, k, v, k0, v0, g, *, n_meta, first_block):
    nb, t_len, _ = q.shape
    nblk = t_len // BLOCK
    cur = lambda b, i: (b, i, 0)
    prev = lambda b, i: (b, jnp.maximum(i - 1, 0), 0)
    init = lambda b, i: (b, 0, 0)
    kern = functools.partial(_attn_seq_kernel, first_block=first_block, n_meta=n_meta)
    return pl.pallas_call(
        kern,
        grid=(nb, nblk),
        in_specs=[
            pl.BlockSpec(memory_space=pltpu.SMEM),
            pl.BlockSpec((1, BLOCK, D_ATTN), cur),
            pl.BlockSpec((1, BLOCK, D_KV), cur),
            pl.BlockSpec((1, BLOCK, D_KV), prev),
            pl.BlockSpec((1, BLOCK, D_KV), init),
            pl.BlockSpec((1, BLOCK, D_KV), cur),
            pl.BlockSpec((1, BLOCK, D_KV), prev),
            pl.BlockSpec((1, BLOCK, D_KV), init),
            _const_spec((1, D_ATTN)),
        ],
        out_specs=pl.BlockSpec((1, BLOCK, D_ATTN), cur),
        out_shape=jax.ShapeDtypeStruct((nb, t_len, D_ATTN), BF16),
        compiler_params=pltpu.CompilerParams(
            dimension_semantics=("arbitrary", "arbitrary"), vmem_limit_bytes=VMEM_LIMIT),
        name="attn_seq",
    )(sinks, q, k, k, k0, v, v, v0, g)


def _attn_step_kernel(q_ref, kn_ref, vn_ref, ck_ref, cv_ref, sink_ref, g_ref, o_ref, ok_ref, ov_ref):
    ok_ref[:, 0:WINDOW - 1, :] = ck_ref[:, 1:WINDOW, :]
    ok_ref[:, WINDOW - 1:WINDOW, :] = kn_ref[...]
    ov_ref[:, 0:WINDOW - 1, :] = cv_ref[:, 1:WINDOW, :]
    ov_ref[:, WINDOW - 1:WINDOW, :] = vn_ref[...]
    kb = ok_ref[...].astype(BF16)
    vb = ov_ref[...].astype(BF16)
    q = q_ref[...]
    q4 = jnp.concatenate([q] * N_KV_HEADS, axis=-1)
    hrow = lax.broadcasted_iota(jnp.int32, (N_HEADS, D_KV), 0)
    hcol = lax.broadcasted_iota(jnp.int32, (N_HEADS, D_KV), 1)
    own = (hcol // HEAD_DIM) == (hrow // Q_PER_KV)
    qe = jnp.where(own[None], q4, jnp.zeros_like(q4))
    s = jnp.einsum('bhd,bwd->bhw', qe, kb, preferred_element_type=F32)
    sink = sink_ref[...][None]
    m = jnp.maximum(jnp.max(s, axis=-1, keepdims=True), sink)
    p = jnp.exp(s - m)
    den = jnp.sum(p, axis=-1, keepdims=True) + jnp.exp(sink - m)
    o = jnp.einsum('bhw,bwd->bhd', (p / den).astype(BF16), vb, preferred_element_type=F32)
    o = jnp.where(own[None], o, 0.0)
    o = (o[..., 0:HEAD_DIM] + o[..., HEAD_DIM:2 * HEAD_DIM]
         + o[..., 2 * HEAD_DIM:3 * HEAD_DIM] + o[..., 3 * HEAD_DIM:4 * HEAD_DIM])
    ms = jnp.sum(jnp.sum(o * o, axis=2, keepdims=True), axis=1, keepdims=True) * (1.0 / D_ATTN)
    o_ref[...] = (o * lax.rsqrt(ms + EPS) * g_ref[...][None]).astype(BF16)


def _attn_step(q3, kn, vn, ck, cv, sinks, g, *, bb):
    nseq = q3.shape[0]
    blk3 = lambda i: (i, 0, 0)
    return pl.pallas_call(
        _attn_step_kernel,
        grid=(nseq // bb,),
        in_specs=[
            pl.BlockSpec((bb, N_HEADS, HEAD_DIM), blk3),
            pl.BlockSpec((bb, 1, D_KV), blk3),
            pl.BlockSpec((bb, 1, D_KV), blk3),
            pl.BlockSpec((bb, WINDOW, D_KV), blk3),
            pl.BlockSpec((bb, WINDOW, D_KV), blk3),
            _const_spec((N_HEADS, 1)),
            _const_spec((N_HEADS, HEAD_DIM)),
        ],
        out_specs=[
            pl.BlockSpec((bb, N_HEADS, HEAD_DIM), blk3),
            pl.BlockSpec((bb, WINDOW, D_KV), blk3),
            pl.BlockSpec((bb, WINDOW, D_KV), blk3),
        ],
        out_shape=[
            jax.ShapeDtypeStruct((nseq, N_HEADS, HEAD_DIM), BF16),
            jax.ShapeDtypeStruct((nseq, WINDOW, D_KV), F32),
            jax.ShapeDtypeStruct((nseq, WINDOW, D_KV), F32),
        ],
        compiler_params=pltpu.CompilerParams(
            dimension_semantics=("arbitrary",), vmem_limit_bytes=VMEM_LIMIT),
        name="attn_step",
    )(q3, kn, vn, ck, cv, sinks, g)


def _out_proj_kernel(x_ref, ys_ref, ya_ref, w_ref, o_ref):
    o_ref[0] = (x_ref[0] + _mm(ys_ref[0], w_ref[0:D_SSM, :])
                + _mm(ya_ref[0], w_ref[D_SSM:D_SSM + D_ATTN, :]))


def _out_proj(x3, ys, ya, w_bf, *, tm):
    nb, t_len, _ = x3.shape
    blk = lambda b, t: (b, t, 0)
    return pl.pallas_call(
        _out_proj_kernel,
        grid=(nb, t_len // tm),
        in_specs=[
            pl.BlockSpec((1, tm, D_MODEL), blk),
            pl.BlockSpec((1, tm, D_SSM), blk),
            pl.BlockSpec((1, tm, D_ATTN), blk),
            _const_spec((D_SSM + D_ATTN, D_MODEL)),
        ],
        out_specs=pl.BlockSpec((1, tm, D_MODEL), blk),
        out_shape=jax.ShapeDtypeStruct((nb, t_len, D_MODEL), F32),
        compiler_params=pltpu.CompilerParams(
            dimension_semantics=("arbitrary", "arbitrary"), vmem_limit_bytes=VMEM_LIMIT),
        name="out_proj",
    )(x3, ys, ya, w_bf)


def _ffn_seq_kernel(x_ref, g_ref, wv_ref, wg_ref, cwv_ref, cwg_ref, cbv_ref, cbg_ref, wd_ref,
                    bufv_ref, bufg_ref, y_ref, nbv_ref, nbg_ref,
                    h_s, upv_s, upg_s, *, tm, tf):
    t = pl.program_id(1)
    j = pl.program_id(2)
    chan = pl.ds(pl.multiple_of(j * tf, tf), tf)

    @pl.when(j == 0)
    def _():
        x = x_ref[...]
        h_s[...] = _rms(x, g_ref[...]).astype(BF16)
        y_ref[...] = x

    @pl.when(t == 0)
    def _():
        upv_s[0:SUBLANES, :] = bufv_ref[0]
        upg_s[0:SUBLANES, :] = bufg_ref[0]

    @pl.when(t > 0)
    def _():
        upv_s[0:SUBLANES, :] = nbv_ref[0, :, chan]
        upg_s[0:SUBLANES, :] = nbg_ref[0, :, chan]

    rm = min(tm, FFN_SUB_ROWS)
    n_sub = tm // rm

    def up_proj(up_s, w_ref, r0):
        up_s[SUBLANES + r0:SUBLANES + r0 + rm, :] = _mm(h_s[r0:r0 + rm, :], w_ref[...])

    def conv(up_s, cw_ref, cb_ref, r0):
        return (cb_ref[...] + cw_ref[0:1, :] * up_s[SUBLANES - 2 + r0:SUBLANES - 2 + r0 + rm, :]
                + cw_ref[1:2, :] * up_s[SUBLANES - 1 + r0:SUBLANES - 1 + r0 + rm, :]
                + cw_ref[2:3, :] * up_s[SUBLANES + r0:SUBLANES + r0 + rm, :])

    def gate_down(r0):
        val = conv(upv_s, cwv_ref, cbv_ref, r0)
        gate = conv(upg_s, cwg_ref, cbg_ref, r0)
        act = (jax.nn.silu(gate) * val).astype(BF16)
        y_ref[r0:r0 + rm, :] += _mm(act, wd_ref[...])

    for r in range(n_sub):
        up_proj(upv_s, wv_ref, r * rm)
        up_proj(upg_s, wg_ref, r * rm)
        if r > 0:
            gate_down((r - 1) * rm)
    gate_down((n_sub - 1) * rm)
    nbv_ref[0, :, chan] = upv_s[tm:tm + SUBLANES, :]
    nbg_ref[0, :, chan] = upg_s[tm:tm + SUBLANES, :]


def _ffn_seq(x1, g, wup_bf, conv_w, conv_b, wdown_bf, bufv, bufg, *, nb, tm):
    n = x1.shape[0]
    nt = n // nb // tm
    tf = FF_TILE
    nj = D_FF // tf
    row = lambda b, t, j: (b * nt + t, 0)
    kern = functools.partial(_ffn_seq_kernel, tm=tm, tf=tf)
    return pl.pallas_call(
        kern,
        grid=(nb, nt, nj),
        in_specs=[
            pl.BlockSpec((tm, D_MODEL), row),
            _const_spec((1, D_MODEL)),
            pl.BlockSpec((D_MODEL, tf), lambda b, t, j: (0, j)),
            pl.BlockSpec((D_MODEL, tf), lambda b, t, j: (0, nj + j)),
            pl.BlockSpec((CONV_W, tf), lambda b, t, j: (0, j)),
            pl.BlockSpec((CONV_W, tf), lambda b, t, j: (0, nj + j)),
            pl.BlockSpec((1, tf), lambda b, t, j: (0, j)),
            pl.BlockSpec((1, tf), lambda b, t, j: (0, nj + j)),
            pl.BlockSpec((tf, D_MODEL), lambda b, t, j: (j, 0)),
            pl.BlockSpec((1, SUBLANES, tf), lambda b, t, j: (b, 0, j)),
            pl.BlockSpec((1, SUBLANES, tf), lambda b, t, j: (b, 0, j)),
        ],
        out_specs=[
            pl.BlockSpec((tm, D_MODEL), row),
            pl.BlockSpec((1, SUBLANES, D_FF), lambda b, t, j: (b, 0, 0)),
            pl.BlockSpec((1, SUBLANES, D_FF), lambda b, t, j: (b, 0, 0)),
        ],
        out_shape=[
            jax.ShapeDtypeStruct((n, D_MODEL), F32),
            jax.ShapeDtypeStruct((nb, SUBLANES, D_FF), F32),
            jax.ShapeDtypeStruct((nb, SUBLANES, D_FF), F32),
        ],
        scratch_shapes=[
            pltpu.VMEM((tm, D_MODEL), BF16),
            pltpu.VMEM((tm + SUBLANES, tf), F32),
            pltpu.VMEM((tm + SUBLANES, tf), F32),
        ],
        compiler_params=pltpu.CompilerParams(
            dimension_semantics=("arbitrary", "arbitrary", "arbitrary"), vmem_limit_bytes=VMEM_LIMIT),
        name="ffn_seq",
    )(x1, g, wup_bf, wup_bf, conv_w, conv_w, conv_b, conv_b, wdown_bf, bufv, bufg)


def _ffn_step_kernel(x_ref, g_ref, wv_ref, wg_ref, cwv_ref, cwg_ref, cbv_ref, cbg_ref, wd_ref,
                     buf_ref, y_ref, nbuf_ref, h_s):
    j = pl.program_id(0)

    @pl.when(j == 0)
    def _():
        x = x_ref[...]
        h_s[...] = _rms(x, g_ref[...]).astype(BF16)
        y_ref[...] = x

    h = h_s[...]
    upv = _mm(h, wv_ref[...])
    upg = _mm(h, wg_ref[...])
    val = cbv_ref[...] + cwv_ref[0:1, :] * buf_ref[:, 0, :] + cwv_ref[1:2, :] * buf_ref[:, 2, :] + cwv_ref[2:3, :] * upv
    gate = cbg_ref[...] + cwg_ref[0:1, :] * buf_ref[:, 1, :] + cwg_ref[1:2, :] * buf_ref[:, 3, :] + cwg_ref[2:3, :] * upg
    act = (jax.nn.silu(gate) * val).astype(BF16)
    y_ref[...] += _mm(act, wd_ref[...])
    nbuf_ref[:, 0, :] = buf_ref[:, 2, :]
    nbuf_ref[:, 1, :] = buf_ref[:, 3, :]
    nbuf_ref[:, 2, :] = upv
    nbuf_ref[:, 3, :] = upg


def _ffn_step(x1, g, wup_bf, conv_w, conv_b, wdown_bf, buf4):
    nseq = x1.shape[0]
    tf = FF_TILE
    nj = D_FF // tf
    return pl.pallas_call(
        _ffn_step_kernel,
        grid=(nj,),
        in_specs=[
            _const_spec((nseq, D_MODEL)),
            _const_spec((1, D_MODEL)),
            pl.BlockSpec((D_MODEL, tf), lambda j: (0, j)),
            pl.BlockSpec((D_MODEL, tf), lambda j: (0, nj + j)),
            pl.BlockSpec((CONV_W, tf), lambda j: (0, j)),
            pl.BlockSpec((CONV_W, tf), lambda j: (0, nj + j)),
            pl.BlockSpec((1, tf), lambda j: (0, j)),
            pl.BlockSpec((1, tf), lambda j: (0, nj + j)),
            pl.BlockSpec((tf, D_MODEL), lambda j: (j, 0)),
            pl.BlockSpec((nseq, 2 * (CONV_W - 1), tf), lambda j: (0, 0, j)),
        ],
        out_specs=[
            pl.BlockSpec((nseq, D_MODEL), lambda j: (0, 0)),
            pl.BlockSpec((nseq, 2 * (CONV_W - 1), tf), lambda j: (0, 0, j)),
        ],
        out_shape=[
            jax.ShapeDtypeStruct((nseq, D_MODEL), F32),
            jax.ShapeDtypeStruct((nseq, 2 * (CONV_W - 1), D_FF), F32),
        ],
        scratch_shapes=[pltpu.VMEM((nseq, D_MODEL), BF16)],
        compiler_params=pltpu.CompilerParams(
            dimension_semantics=("arbitrary",), vmem_limit_bytes=VMEM_LIMIT),
        name="ffn_step",
    )(x1, g, wup_bf, wup_bf, conv_w, conv_w, conv_b, conv_b, wdown_bf, buf4)


def _rope_tables(pos):
    half = HEAD_DIM // 2
    inv = ROPE_THETA ** (-jnp.arange(half, dtype=F32) * 2.0 / HEAD_DIM)
    ang = pos.astype(F32)[:, None] * inv[None, :]
    c = jnp.cos(ang)
    s = jnp.sin(ang)
    return jnp.concatenate([c, c, c, c], axis=1), jnp.concatenate([-s, s, -s, s], axis=1)


def _block_diag(w):
    nd, ng, r, c = w.shape
    eye = jnp.eye(ng, dtype=w.dtype)
    return jnp.einsum('dgrc,gh->dgrhc', w, eye).reshape(nd, ng * r, ng * c)


def kernel(x_prompt, x_sample, cache_k, cache_v, state_ssm_re, state_ssm_im, state_conv, meta_tokens,
           norm_mix_g, w_in, q_norm_g, k_norm_g, attn_sinks, lam_re, lam_im, log_dt, ssm_b_re, ssm_b_im,
           ssm_c_re, ssm_c_im, ssm_d, w_glu, b_glu, ssm_out_g, attn_out_g, w_out, norm_ffn_g, w_up,
           conv_w, conv_b, w_down):
    depth = w_in.shape[0]
    assert depth == 1
    nb, seq, _ = x_prompt.shape
    nseq = x_sample.shape[0]
    n_meta = meta_tokens.shape[0]
    assert x_sample.shape[1] == 1 and n_meta == 16 and nb == SUBLANES
    assert seq % 1024 == 0 and nseq % SUBLANES == 0
    l = 0
    gpd = SSM_GROUPS_PER_DOT
    n_dots = N_SSM_GROUPS // gpd

    w_in_bf = w_in[l].astype(BF16)
    w_glu_bf = w_glu[l].astype(BF16)
    w_out_bf = w_out[l].astype(BF16)
    w_up_bf = w_up[l].astype(BF16)
    w_down_bf = w_down[l].astype(BF16)
    g_mix = norm_mix_g[l][None, :]
    g_ffn = norm_ffn_g[l][None, :]
    qg = jnp.tile(q_norm_g[l], 256 // HEAD_DIM)[None, :]
    kg = jnp.tile(k_norm_g[l], 256 // HEAD_DIM)[None, :]
    lane_head = jnp.arange(256) // HEAD_DIM
    ones_bd = (lane_head[:, None] == lane_head[None, :]).astype(BF16)
    sinks = attn_sinks[l]
    lamre = lam_re[l].reshape(1, D_STATE)
    lamim = lam_im[l].reshape(1, D_STATE)
    logdt = jnp.repeat(log_dt[l], SSM_STATE)[None, :]
    b_re_t = jnp.swapaxes(ssm_b_re[l], 1, 2).reshape(n_dots, gpd, SSM_GROUP, SSM_STATE)
    b_im_t = jnp.swapaxes(ssm_b_im[l], 1, 2).reshape(n_dots, gpd, SSM_GROUP, SSM_STATE)
    bblk = jnp.concatenate([_block_diag(b_re_t), _block_diag(b_im_t)], axis=2)
    cre = _block_diag(jnp.swapaxes(ssm_c_re[l], 1, 2).reshape(n_dots, gpd, SSM_STATE, SSM_GROUP)).astype(BF16)
    cim = _block_diag(jnp.swapaxes(ssm_c_im[l], 1, 2).reshape(n_dots, gpd, SSM_STATE, SSM_GROUP)).astype(BF16)
    dskip = ssm_d[l].reshape(1, D_SSM)
    bglu = b_glu[l][None, :]
    g_ssm = ssm_out_g[l][None, :]
    g_att = attn_out_g[l][None, :]
    cw = conv_w[l]
    cb = conv_b[l][None, :]

    ssm_w = (lamre, lamim, logdt, bblk, cre, cim, dskip, w_glu_bf, bglu, g_ssm)

    pad = BLOCK - n_meta
    xm = jnp.concatenate([jnp.zeros((pad, D_MODEL), F32), meta_tokens.astype(F32)], axis=0)[None]
    cos_m, sin_m = _rope_tables(jnp.maximum(jnp.arange(BLOCK) - pad, 0))
    u_m, q_m, k_m, v_m = _in_proj(xm, g_mix, w_in_bf, cos_m, sin_m, qg, kg, ones_bd, tq=BLOCK)
    u_m8 = jnp.broadcast_to(u_m, (BLOCK, SUBLANES, D_SSM)).reshape(BLOCK * SUBLANES, D_SSM)
    zero_state = jnp.zeros((SUBLANES, D_STATE), F32)
    ys_m8, hre_m, him_m = _ssm(u_m8, zero_state, zero_state, *ssm_w, nbatch=SUBLANES, tt=SCAN_STEPS)
    zero_kv = jnp.zeros((1, BLOCK, D_KV), F32)
    ya_m = _attn_seq(sinks, q_m, k_m, v_m, zero_kv, zero_kv, g_att, n_meta=n_meta, first_block=0)
    x1_m = _out_proj(xm, ys_m8[0:1], ya_m, w_out_bf, tm=BLOCK)
    zero_buf = jnp.zeros((1, SUBLANES, D_FF), F32)
    _, bufv_m, bufg_m = _ffn_seq(x1_m[0], g_ffn, w_up_bf, cw, cb, w_down_bf, zero_buf, zero_buf, nb=1, tm=BLOCK)

    cos_p, sin_p = _rope_tables(n_meta + jnp.arange(seq))
    u_p, q_p, k_p, v_p = _in_proj(x_prompt, g_mix, w_in_bf, cos_p, sin_p, qg, kg, ones_bd, tq=128)
    ys_p, hre_p, him_p = _ssm(u_p.reshape(seq * nb, D_SSM), hre_m, him_m, *ssm_w, nbatch=nb, tt=SCAN_STEPS)
    k0 = jnp.broadcast_to(k_m, (nb, BLOCK, D_KV))
    v0 = jnp.broadcast_to(v_m, (nb, BLOCK, D_KV))
    ya_p = _attn_seq(sinks, q_p, k_p, v_p, k0, v0, g_att, n_meta=n_meta, first_block=1)
    x1_p = _out_proj(x_prompt, ys_p, ya_p, w_out_bf, tm=512)
    bufv0 = jnp.broadcast_to(bufv_m, (nb, SUBLANES, D_FF))
    bufg0 = jnp.broadcast_to(bufg_m, (nb, SUBLANES, D_FF))
    y_p, bufv_p, bufg_p = _ffn_seq(x1_p.reshape(nb * seq, D_MODEL), g_ffn, w_up_bf, cw, cb, w_down_bf,
                                   bufv0, bufg0, nb=nb, tm=1024)

    y_prompt = y_p.reshape(nb, seq, D_MODEL)
    prompt_k = k_p[:, seq - WINDOW:].reshape(nb, WINDOW, N_KV_HEADS, HEAD_DIM)[None]
    prompt_v = v_p[:, seq - WINDOW:].reshape(nb, WINDOW, N_KV_HEADS, HEAD_DIM)[None]
    prompt_re = hre_p.reshape(nb, N_SSM_GROUPS, SSM_STATE)[None]
    prompt_im = him_p.reshape(nb, N_SSM_GROUPS, SSM_STATE)[None]
    prompt_conv = jnp.concatenate([bufv_p[:, SUBLANES - 2:], bufg_p[:, SUBLANES - 2:]], axis=-1)[None]

    xs = x_sample.reshape(1, nseq, D_MODEL)
    cos_s, sin_s = _rope_tables(jnp.full((nseq,), PAST_LEN, jnp.int32))
    u_s, q_s, k_s, v_s = _in_proj(xs, g_mix, w_in_bf, cos_s, sin_s, qg, kg, ones_bd, tq=nseq)
    ys_s, hre_s, him_s = _ssm(u_s.reshape(nseq, D_SSM), state_ssm_re[l].reshape(nseq, D_STATE),
                              state_ssm_im[l].reshape(nseq, D_STATE), *ssm_w, nbatch=nseq, tt=1)
    ya_s, nk_s, nv_s = _attn_step(
        q_s.reshape(nseq, N_HEADS, HEAD_DIM), k_s.reshape(nseq, 1, D_KV), v_s.reshape(nseq, 1, D_KV),
        cache_k[l].reshape(nseq, WINDOW, D_KV), cache_v[l].reshape(nseq, WINDOW, D_KV),
        sinks[:, None], attn_out_g[l].reshape(N_HEADS, HEAD_DIM), bb=8)
    x1_s = _out_proj(xs, ys_s, ya_s.reshape(1, nseq, D_ATTN), w_out_bf, tm=nseq)
    buf4 = state_conv[l].reshape(nseq, 2 * (CONV_W - 1), D_FF)
    y_s, nbuf4 = _ffn_step(x1_s[0], g_ffn, w_up_bf, cw, cb, w_down_bf, buf4)

    y_sample = y_s.reshape(nseq, 1, D_MODEL)
    sample_k = nk_s.reshape(nseq, WINDOW, N_KV_HEADS, HEAD_DIM)[None]
    sample_v = nv_s.reshape(nseq, WINDOW, N_KV_HEADS, HEAD_DIM)[None]
    sample_re = hre_s.reshape(nseq, N_SSM_GROUPS, SSM_STATE)[None]
    sample_im = him_s.reshape(nseq, N_SSM_GROUPS, SSM_STATE)[None]
    sample_conv = nbuf4.reshape(nseq, CONV_W - 1, 2 * D_FF)[None]

    return (y_prompt, y_sample, prompt_k, prompt_v, prompt_re, prompt_im, prompt_conv,
            sample_k, sample_v, sample_re, sample_im, sample_conv)
```

```python
import functools

import jax
import jax.numpy as jnp
from jax import lax
from jax.experimental import pallas as pl
from jax.experimental.pallas import tpu as pltpu

F32 = jnp.float32
BF16 = jnp.bfloat16

D_MODEL = 2048
D_SSM = 1024
SSM_GROUP = 16
N_SSM_GROUPS = 64
SSM_STATE = 64
D_STATE = N_SSM_GROUPS * SSM_STATE
HEAD_DIM = 64
N_HEADS = 16
N_KV_HEADS = 4
Q_PER_KV = 4
D_ATTN = 1024
D_KV = 256
D_IN = D_SSM + D_ATTN + 2 * D_KV
WINDOW = 128
BLOCK = 128
ROPE_THETA = 10000.0
D_FF = 5632
CONV_W = 3
EPS = 1e-6
NEG = -1e30
PAST_LEN = 16384

SUBLANES = 8
LANES = 128
VMEM_LIMIT = 60 * 1024 * 1024
FF_TILE = 512
FFN_SUB_ROWS = 256
IN_SUB_ROWS = 256
ATTN_Q_BLOCKS = 2
SCAN_LANES = 512
SCAN_STEPS = 32
SSM_GROUPS_PER_DOT = 16


def _const_spec(shape):
    nd = len(shape)
    return pl.BlockSpec(shape, lambda *_: (0,) * nd, pipeline_mode=pl.Buffered(1))


def _rms(x, g):
    return x * lax.rsqrt(jnp.mean(x * x, axis=-1, keepdims=True) + EPS) * g


def _mm(a, b):
    return jnp.dot(a, b, preferred_element_type=F32)


def _in_proj_kernel(x_ref, g_ref, w_ref, cos_ref, sin_ref, qg_ref, kg_ref, ones_ref,
                    u_ref, q_ref, k_ref, v_ref, *, nb, tq, n_split):
    rs = nb * tq // n_split
    cos1 = jnp.concatenate([cos_ref[...], cos_ref[...]], axis=1)
    sin1 = jnp.concatenate([sin_ref[...], sin_ref[...]], axis=1)
    ones = ones_ref[...]
    n_q = D_ATTN // 256
    for s in range(n_split):
        segs = []
        for b in range(nb):
            lo = max(s * rs, b * tq)
            hi = min((s + 1) * rs, (b + 1) * tq)
            if lo < hi:
                segs.append((b, lo - b * tq, hi - lo, lo - s * rs))
        x = jnp.concatenate([x_ref[b, t0:t0 + n, :] for b, t0, n, _ in segs], axis=0)
        cos = jnp.concatenate([cos1[t0:t0 + n, :] for _, t0, n, _ in segs], axis=0)
        sin = jnp.concatenate([sin1[t0:t0 + n, :] for _, t0, n, _ in segs], axis=0)
        lane = lax.broadcasted_iota(jnp.int32, cos.shape, 1)
        first_half = (lane % HEAD_DIM) < (HEAD_DIM // 2)
        h = _rms(x, g_ref[...]).astype(BF16)
        z = _mm(h, w_ref[...])
        for b, t0, n, r0 in segs:
            u_ref[t0:t0 + n, b, :] = z[r0:r0 + n, :D_SSM]
            v_ref[b, t0:t0 + n, :] = z[r0:r0 + n, D_SSM + D_ATTN + D_KV:]
        for c in range(n_q + D_KV // 256):
            zc = z[:, D_SSM + 256 * c: D_SSM + 256 * (c + 1)]
            sq = zc * zc
            hi = sq.astype(BF16)
            lo = (sq - hi.astype(F32)).astype(BF16)
            ssum = _mm(hi, ones) + _mm(lo, ones)
            gain = qg_ref[...] if c < n_q else kg_ref[...]
            nrm = zc * lax.rsqrt(ssum * (1.0 / HEAD_DIM) + EPS) * gain
            partner = jnp.where(first_half, pltpu.roll(nrm, 256 - HEAD_DIM // 2, 1),
                                pltpu.roll(nrm, HEAD_DIM // 2, 1))
            r = nrm * cos + partner * sin
            for b, t0, n, r0 in segs:
                if c < n_q:
                    q_ref[b, t0:t0 + n, 256 * c: 256 * (c + 1)] = (
                        r[r0:r0 + n, :] * (HEAD_DIM ** -0.5)).astype(BF16)
                else:
                    k_ref[b, t0:t0 + n, :] = r[r0:r0 + n, :]


def _in_proj(x3, g, w_bf, cos, sin, qg, kg, ones_bd, *, tq):
    nb, t_len, _ = x3.shape
    blk = lambda t: (0, t, 0)
    kern = functools.partial(_in_proj_kernel, nb=nb, tq=tq, n_split=max(1, nb * tq // IN_SUB_ROWS))
    return pl.pallas_call(
        kern,
        grid=(t_len // tq,),
        in_specs=[
            pl.BlockSpec((nb, tq, D_MODEL), blk),
            _const_spec((1, D_MODEL)),
            _const_spec((D_MODEL, D_IN)),
            pl.BlockSpec((tq, LANES), lambda t: (t, 0)),
            pl.BlockSpec((tq, LANES), lambda t: (t, 0)),
            _const_spec((1, 256)),
            _const_spec((1, 256)),
            _const_spec((256, 256)),
        ],
        out_specs=[
            pl.BlockSpec((tq, nb, D_SSM), lambda t: (t, 0, 0)),
            pl.BlockSpec((nb, tq, D_ATTN), blk),
            pl.BlockSpec((nb, tq, D_KV), blk),
            pl.BlockSpec((nb, tq, D_KV), blk),
        ],
        out_shape=[
            jax.ShapeDtypeStruct((t_len, nb, D_SSM), F32),
            jax.ShapeDtypeStruct((nb, t_len, D_ATTN), BF16),
            jax.ShapeDtypeStruct((nb, t_len, D_KV), F32),
            jax.ShapeDtypeStruct((nb, t_len, D_KV), F32),
        ],
        compiler_params=pltpu.CompilerParams(
            dimension_semantics=("arbitrary",), vmem_limit_bytes=VMEM_LIMIT),
        name="in_proj",
    )(x3, g, w_bf, cos, sin, qg, kg, ones_bd)


def _ssm_kernel(u_ref, h0re_ref, h0im_ref, lamre_ref, lamim_ref, logdt_ref, bblk_ref,
                cre_ref, cim_ref, dskip_ref, wglu_ref, bglu_ref, g_ref, perm_ref,
                y_ref, sre_ref, sim_ref,
                are_s, aim_s, bbar_s, xre_s, xim_s, stre_s, stim_s, *, tt, nbg):
    step = pl.program_id(0)
    lanes_per_dot = SSM_GROUPS_PER_DOT * SSM_STATE
    chans_per_dot = SSM_GROUPS_PER_DOT * SSM_GROUP
    n_dots = N_SSM_GROUPS // SSM_GROUPS_PER_DOT
    state_lanes = [slice(lanes_per_dot * c, lanes_per_dot * (c + 1)) for c in range(n_dots)]
    chans = [slice(chans_per_dot * c, chans_per_dot * (c + 1)) for c in range(n_dots)]

    @pl.when(step == 0)
    def _():
        lr = lamre_ref[...]
        li = lamim_ref[...]
        dt = jnp.exp(logdt_ref[...])
        mag = jnp.exp(lr * dt)
        a_re = mag * jnp.cos(li * dt)
        a_im = mag * jnp.sin(li * dt)
        den = lr * lr + li * li
        am1 = a_re - 1.0
        cf_re = (am1 * lr + a_im * li) / den
        cf_im = (a_im * lr - am1 * li) / den
        shape = (SUBLANES, D_STATE)
        are_s[...] = jnp.broadcast_to(a_re, shape)
        aim_s[...] = jnp.broadcast_to(a_im, shape)
        for c in range(n_dots):
            b_re = bblk_ref[c, :, :lanes_per_dot]
            b_im = bblk_ref[c, :, lanes_per_dot:]
            c_re = cf_re[:, state_lanes[c]]
            c_im = cf_im[:, state_lanes[c]]
            bbar_s[c, :, :lanes_per_dot] = (c_re * b_re - c_im * b_im).astype(BF16)
            bbar_s[c, :, lanes_per_dot:] = (c_re * b_im + c_im * b_re).astype(BF16)
        stre_s[...] = h0re_ref[...]
        stim_s[...] = h0im_ref[...]

    u = u_ref[...]
    ub = u.astype(BF16)

    def project(c):
        bu = _mm(ub[:, chans[c]], bbar_s[c])
        xre_s[:, state_lanes[c]] = bu[:, :lanes_per_dot]
        xim_s[:, state_lanes[c]] = bu[:, lanes_per_dot:]

    def scan(c):
        blocks = [slice(lanes_per_dot * c + SCAN_LANES * i, lanes_per_dot * c + SCAN_LANES * (i + 1))
                  for i in range(lanes_per_dot // SCAN_LANES)]
        for bg in range(nbg):
            rs = slice(SUBLANES * bg, SUBLANES * (bg + 1))
            a = [(are_s[:, ls], aim_s[:, ls]) for ls in blocks]
            st = [(stre_s[rs, ls], stim_s[rs, ls]) for ls in blocks]
            for t in range(tt):
                rt = slice((t * nbg + bg) * SUBLANES, (t * nbg + bg + 1) * SUBLANES)
                for i, ls in enumerate(blocks):
                    (a_re, a_im), (s_re, s_im) = a[i], st[i]
                    n_re = a_re * s_re - a_im * s_im + xre_s[rt, ls]
                    n_im = a_re * s_im + a_im * s_re + xim_s[rt, ls]
                    xre_s[rt, ls] = n_re
                    xim_s[rt, ls] = n_im
                    st[i] = (n_re, n_im)
            for i, ls in enumerate(blocks):
                stre_s[rs, ls] = st[i][0]
                stim_s[rs, ls] = st[i][1]

    def readout(c):
        yc = (_mm(xre_s[:, state_lanes[c]].astype(BF16), cre_ref[c])
              - _mm(xim_s[:, state_lanes[c]].astype(BF16), cim_ref[c]))
        return jax.nn.gelu(yc + dskip_ref[:, chans[c]] * u[:, chans[c]])

    ys = [None] * n_dots
    project(0)
    for c in range(n_dots):
        if c + 1 < n_dots:
            project(c + 1)
        scan(c)
        if c > 0:
            ys[c - 1] = readout(c - 1)
    ys[n_dots - 1] = readout(n_dots - 1)
    y = jnp.concatenate(ys, axis=1)
    y = y * jax.nn.sigmoid(_mm(y.astype(BF16), wglu_ref[...]) + bglu_ref[...])
    yn = _rms(y, g_ref[...]).astype(BF16)
    if tt == 1:
        y_ref[0] = yn
    else:
        y_ref[...] = _mm(perm_ref[...], yn).astype(BF16).reshape(y_ref.shape)

    @pl.when(step == pl.num_programs(0) - 1)
    def _():
        sre_ref[...] = stre_s[...]
        sim_ref[...] = stim_s[...]


def _ssm(u_tm, h0re, h0im, lamre, lamim, logdt, bblk, cre, cim, dskip, wglu, bglu, g, *, nbatch, tt):
    n = u_tm.shape[0]
    t_len = n // nbatch
    nbg = nbatch // SUBLANES
    rows = tt * nbatch
    n_grp = nbatch if tt > 1 else 1
    r = jnp.arange(rows)
    src = (r % tt) * nbatch + r // tt
    perm = (src[:, None] == r[None, :]).astype(BF16)
    kern = functools.partial(_ssm_kernel, tt=tt, nbg=nbg)
    return pl.pallas_call(
        kern,
        grid=(t_len // tt,),
        in_specs=[
            pl.BlockSpec((rows, D_SSM), lambda s: (s, 0)),
            _const_spec((nbatch, D_STATE)),
            _const_spec((nbatch, D_STATE)),
            _const_spec((1, D_STATE)),
            _const_spec((1, D_STATE)),
            _const_spec((1, D_STATE)),
            _const_spec(bblk.shape),
            _const_spec(cre.shape),
            _const_spec(cim.shape),
            _const_spec((1, D_SSM)),
            _const_spec((D_SSM, D_SSM)),
            _const_spec((1, D_SSM)),
            _const_spec((1, D_SSM)),
            _const_spec((rows, rows)),
        ],
        out_specs=[
            pl.BlockSpec((n_grp, rows // n_grp, D_SSM), lambda s: (0, s, 0)),
            pl.BlockSpec((nbatch, D_STATE), lambda s: (0, 0)),
            pl.BlockSpec((nbatch, D_STATE), lambda s: (0, 0)),
        ],
        out_shape=[
            jax.ShapeDtypeStruct((n_grp, n // n_grp, D_SSM), BF16),
            jax.ShapeDtypeStruct((nbatch, D_STATE), F32),
            jax.ShapeDtypeStruct((nbatch, D_STATE), F32),
        ],
        scratch_shapes=[pltpu.VMEM((SUBLANES, D_STATE), F32)] * 2
        + [pltpu.VMEM(bblk.shape, BF16)]
        + [pltpu.VMEM((rows, D_STATE), F32)] * 2
        + [pltpu.VMEM((nbatch, D_STATE), F32)] * 2,
        compiler_params=pltpu.CompilerParams(
            dimension_semantics=("arbitrary",), vmem_limit_bytes=VMEM_LIMIT),
        name="ssm",
    )(u_tm, h0re, h0im, lamre, lamim, logdt, bblk, cre, cim, dskip, wglu, bglu, g, perm)


def _attn_seq_kernel(sink_ref, q_ref, kc_ref, kp_ref, k0_ref, vc_ref, vp_ref, v0_ref, g_ref, o_ref,
                     *, first_block, n_meta, qb):
    i = pl.program_id(1)
    is_first = i == 0
    r = lax.broadcasted_iota(jnp.int32, (BLOCK, 4 * BLOCK), 0)
    c = lax.broadcasted_iota(jnp.int32, (BLOCK, 4 * BLOCK), 1) % (2 * BLOCK)
    in_window = (c > r) & (c <= r + WINDOW)
    lo_half = lax.broadcasted_iota(jnp.int32, (2 * BLOCK, LANES), 1) < HEAD_DIM

    def pair_blocks(x2, kv):
        xc = x2[:, LANES * (kv // 2): LANES * (kv // 2 + 1)]
        xr = pltpu.roll(xc, HEAD_DIM, 1)
        zero = jnp.zeros_like(xc)
        if kv % 2 == 0:
            low, high = jnp.where(lo_half, xc, zero), jnp.where(lo_half, zero, xr)
        else:
            low, high = jnp.where(lo_half, xr, zero), jnp.where(lo_half, zero, xc)
        return jnp.concatenate([low, high], axis=0).astype(BF16)

    n_pairs = N_HEADS // 2
    pairs_per_kv = Q_PER_KV // 2
    row_hi = lax.broadcasted_iota(jnp.int32, (4 * BLOCK, LANES), 0) >= 2 * BLOCK
    lane_hi = lax.broadcasted_iota(jnp.int32, (4 * BLOCK, LANES), 1) >= HEAD_DIM
    ones_cols = (row_hi == lane_hi).astype(BF16)
    kbd, vbd, mask, rows = [], [], [], []
    for sb in range(qb):
        rows.append(slice(BLOCK * sb, BLOCK * (sb + 1)))
        if sb == 0:
            k_prev = jnp.where(is_first, k0_ref[0], kp_ref[0])
            v_prev = jnp.where(is_first, v0_ref[0], vp_ref[0])
        else:
            k_prev, v_prev = kc_ref[0, rows[sb - 1], :], vc_ref[0, rows[sb - 1], :]
        k2 = jnp.concatenate([k_prev, kc_ref[0, rows[sb], :]], axis=0)
        v2 = jnp.concatenate([v_prev, vc_ref[0, rows[sb], :]], axis=0)
        gb = i * qb + sb + first_block
        col_min = jnp.where(gb == 0, 2 * BLOCK - n_meta, jnp.where(gb == 1, BLOCK - n_meta, 0))
        mask.append(in_window & (c >= col_min))
        kbd.append([pair_blocks(k2, kv) for kv in range(N_KV_HEADS)])
        vbd.append([jnp.concatenate([pair_blocks(v2, kv), ones_cols], axis=1) for kv in range(N_KV_HEADS)])
    halves = (slice(0, 2 * BLOCK), slice(2 * BLOCK, 4 * BLOCK))
    probs = [(sb, pr) for sb in range(qb) for pr in range(n_pairs)]
    s = {(sb, pr): jnp.where(mask[sb], lax.dot_general(
        q_ref[0, rows[sb], LANES * pr: LANES * (pr + 1)], kbd[sb][pr // pairs_per_kv],
        (((1,), (1,)), ((), ())), preferred_element_type=F32), NEG) for sb, pr in probs}
    m = {(sb, pr): [jnp.maximum(jnp.max(s[sb, pr][:, halves[hh]], axis=-1, keepdims=True), sink_ref[2 * pr + hh])
                    for hh in range(2)] for sb, pr in probs}
    p = {(sb, pr): jnp.concatenate([jnp.exp(s[sb, pr][:, halves[hh]] - m[sb, pr][hh]) for hh in range(2)],
                                   axis=1).astype(BF16) for sb, pr in probs}
    out_hi = lax.broadcasted_iota(jnp.int32, (BLOCK, LANES), 1) >= HEAD_DIM
    outs = {}
    for sb, pr in probs:
        ov = _mm(p[sb, pr], vbd[sb][pr // pairs_per_kv])
        sink_w = jnp.where(out_hi, jnp.exp(sink_ref[2 * pr + 1] - m[sb, pr][1]),
                           jnp.exp(sink_ref[2 * pr] - m[sb, pr][0]))
        outs[sb, pr] = ov[:, :LANES] / (ov[:, LANES:] + sink_w)
    for sb in range(qb):
        o = jnp.concatenate([outs[sb, pr] for pr in range(n_pairs)], axis=1)
        o_ref[0, rows[sb], :] = _rms(o, g_ref[...]).astype(BF16)


def _attn_seq(sinks, q, k, v, k0, v0, g, *, n_meta, first_block):
    nb, t_len, _ = q.shape
    nblk = t_len // BLOCK
    qb = ATTN_Q_BLOCKS if nblk % ATTN_Q_BLOCKS == 0 else 1
    cur = lambda b, i: (b, i, 0)
    prev = lambda b, i: (b, jnp.maximum(i * qb - 1, 0), 0)
    init = lambda b, i: (b, 0, 0)
    kern = functools.partial(_attn_seq_kernel, first_block=first_block, n_meta=n_meta, qb=qb)
    return pl.pallas_call(
        kern,
        grid=(nb, nblk // qb),
        in_specs=[
            pl.BlockSpec(memory_space=pltpu.SMEM),
            pl.BlockSpec((1, qb * BLOCK, D_ATTN), cur),
            pl.BlockSpec((1, qb * BLOCK, D_KV), cur),
            pl.BlockSpec((1, BLOCK, D_KV), prev),
            pl.BlockSpec((1, BLOCK, D_KV), init),
            pl.BlockSpec((1, qb * BLOCK, D_KV), cur),
            pl.BlockSpec((1, BLOCK, D_KV), prev),
            pl.BlockSpec((1, BLOCK, D_KV), init),
            _const_spec((1, D_ATTN)),
        ],
        out_specs=pl.BlockSpec((1, qb * BLOCK, D_ATTN), cur),
        out_shape=jax.ShapeDtypeStruct((nb, t_len, D_ATTN), BF16),
        compiler_params=pltpu.CompilerParams(
            dimension_semantics=("arbitrary", "arbitrary"), vmem_limit_bytes=VMEM_LIMIT),
        name="attn_seq",
    )(sinks, q, k, k, k0, v, v, v0, g)


def _attn_step_kernel(q_ref, kn_ref, vn_ref, ck_ref, cv_ref, sink_ref, g_ref, o_ref, ok_ref, ov_ref):
    n_rows = WINDOW * N_KV_HEADS
    ok_ref[:, 0:n_rows - N_KV_HEADS, :] = ck_ref[:, N_KV_HEADS:n_rows, :]
    ok_ref[:, n_rows - N_KV_HEADS:n_rows, :] = kn_ref[...]
    ov_ref[:, 0:n_rows - N_KV_HEADS, :] = cv_ref[:, N_KV_HEADS:n_rows, :]
    ov_ref[:, n_rows - N_KV_HEADS:n_rows, :] = vn_ref[...]
    kb = ok_ref[...].astype(BF16)
    vb = ov_ref[...].astype(BF16)
    q = q_ref[...]
    s = jnp.einsum('bhd,bcd->bhc', q, kb, preferred_element_type=F32)
    hrow = lax.broadcasted_iota(jnp.int32, (N_HEADS, n_rows), 0)
    hcol = lax.broadcasted_iota(jnp.int32, (N_HEADS, n_rows), 1)
    own = (hcol % N_KV_HEADS) == (hrow // Q_PER_KV)
    s = jnp.where(own[None], s, NEG)
    sink = sink_ref[...][None]
    m = jnp.maximum(jnp.max(s, axis=-1, keepdims=True), sink)
    p = jnp.exp(s - m)
    den = jnp.sum(p, axis=-1, keepdims=True) + jnp.exp(sink - m)
    o = jnp.einsum('bhc,bcd->bhd', (p / den).astype(BF16), vb, preferred_element_type=F32)
    ms = jnp.sum(jnp.sum(o * o, axis=2, keepdims=True), axis=1, keepdims=True) * (1.0 / D_ATTN)
    o_ref[...] = (o * lax.rsqrt(ms + EPS) * g_ref[...][None]).astype(BF16)


def _attn_step(q3, kn, vn, ck, cv, sinks, g, *, bb):
    nseq = q3.shape[0]
    blk3 = lambda i: (i, 0, 0)
    return pl.pallas_call(
        _attn_step_kernel,
        grid=(nseq // bb,),
        in_specs=[
            pl.BlockSpec((bb, N_HEADS, HEAD_DIM), blk3),
            pl.BlockSpec((bb, N_KV_HEADS, HEAD_DIM), blk3),
            pl.BlockSpec((bb, N_KV_HEADS, HEAD_DIM), blk3),
            pl.BlockSpec((bb, WINDOW * N_KV_HEADS, HEAD_DIM), blk3),
            pl.BlockSpec((bb, WINDOW * N_KV_HEADS, HEAD_DIM), blk3),
            _const_spec((N_HEADS, 1)),
            _const_spec((N_HEADS, HEAD_DIM)),
        ],
        out_specs=[
            pl.BlockSpec((bb, N_HEADS, HEAD_DIM), blk3),
            pl.BlockSpec((bb, WINDOW * N_KV_HEADS, HEAD_DIM), blk3),
            pl.BlockSpec((bb, WINDOW * N_KV_HEADS, HEAD_DIM), blk3),
        ],
        out_shape=[
            jax.ShapeDtypeStruct((nseq, N_HEADS, HEAD_DIM), BF16),
            jax.ShapeDtypeStruct((nseq, WINDOW * N_KV_HEADS, HEAD_DIM), F32),
            jax.ShapeDtypeStruct((nseq, WINDOW * N_KV_HEADS, HEAD_DIM), F32),
        ],
        compiler_params=pltpu.CompilerParams(
            dimension_semantics=("arbitrary",), vmem_limit_bytes=VMEM_LIMIT),
        name="attn_step",
    )(q3, kn, vn, ck, cv, sinks, g)


def _out_proj_kernel(x_ref, ys_ref, ya_ref, w_ref, o_ref):
    o_ref[0] = (x_ref[0] + _mm(ys_ref[0], w_ref[0:D_SSM, :])
                + _mm(ya_ref[0], w_ref[D_SSM:D_SSM + D_ATTN, :]))


def _out_proj(x3, ys, ya, w_bf, *, tm):
    nb, t_len, _ = x3.shape
    blk = lambda b, t: (b, t, 0)
    return pl.pallas_call(
        _out_proj_kernel,
        grid=(nb, t_len // tm),
        in_specs=[
            pl.BlockSpec((1, tm, D_MODEL), blk),
            pl.BlockSpec((1, tm, D_SSM), blk),
            pl.BlockSpec((1, tm, D_ATTN), blk),
            _const_spec((D_SSM + D_ATTN, D_MODEL)),
        ],
        out_specs=pl.BlockSpec((1, tm, D_MODEL), blk),
        out_shape=jax.ShapeDtypeStruct((nb, t_len, D_MODEL), F32),
        compiler_params=pltpu.CompilerParams(
            dimension_semantics=("arbitrary", "arbitrary"), vmem_limit_bytes=VMEM_LIMIT),
        name="out_proj",
    )(x3, ys, ya, w_bf)


def _ffn_seq_kernel(x_ref, g_ref, wv_ref, wg_ref, cwv_ref, cwg_ref, cbv_ref, cbg_ref, wd_ref,
                    bufv_ref, bufg_ref, y_ref, nbv_ref, nbg_ref,
                    h_s, upv_s, upg_s, *, tm, tf):
    t = pl.program_id(1)
    j = pl.program_id(2)
    chan = pl.ds(pl.multiple_of(j * tf, tf), tf)

    @pl.when(j == 0)
    def _():
        x = x_ref[...]
        h_s[...] = _rms(x, g_ref[...]).astype(BF16)
        y_ref[...] = x

    @pl.when(t == 0)
    def _():
        upv_s[0:SUBLANES, :] = bufv_ref[0]
        upg_s[0:SUBLANES, :] = bufg_ref[0]

    @pl.when(t > 0)
    def _():
        upv_s[0:SUBLANES, :] = nbv_ref[0, :, chan]
        upg_s[0:SUBLANES, :] = nbg_ref[0, :, chan]

    rm = min(tm, FFN_SUB_ROWS)
    n_sub = tm // rm

    def up_proj(up_s, w_ref, r0):
        up_s[SUBLANES + r0:SUBLANES + r0 + rm, :] = _mm(h_s[r0:r0 + rm, :], w_ref[...])

    def conv(up_s, cw_ref, cb_ref, r0):
        return (cb_ref[...] + cw_ref[0:1, :] * up_s[SUBLANES - 2 + r0:SUBLANES - 2 + r0 + rm, :]
                + cw_ref[1:2, :] * up_s[SUBLANES - 1 + r0:SUBLANES - 1 + r0 + rm, :]
                + cw_ref[2:3, :] * up_s[SUBLANES + r0:SUBLANES + r0 + rm, :])

    def gate_down(r0):
        val = conv(upv_s, cwv_ref, cbv_ref, r0)
        gate = conv(upg_s, cwg_ref, cbg_ref, r0)
        act = (jax.nn.silu(gate) * val).astype(BF16)
        y_ref[r0:r0 + rm, :] += _mm(act, wd_ref[...])

    for r in range(n_sub):
        up_proj(upv_s, wv_ref, r * rm)
        up_proj(upg_s, wg_ref, r * rm)
        if r > 0:
            gate_down((r - 1) * rm)
    gate_down((n_sub - 1) * rm)
    nbv_ref[0, :, chan] = upv_s[tm:tm + SUBLANES, :]
    nbg_ref[0, :, chan] = upg_s[tm:tm + SUBLANES, :]


def _ffn_seq(x1, g, wup_bf, conv_w, conv_b, wdown_bf, bufv, bufg, *, nb, tm):
    n = x1.shape[0]
    nt = n // nb // tm
    tf = FF_TILE
    nj = D_FF // tf
    row = lambda b, t, j: (b * nt + t, 0)
    kern = functools.partial(_ffn_seq_kernel, tm=tm, tf=tf)
    return pl.pallas_call(
        kern,
        grid=(nb, nt, nj),
        in_specs=[
            pl.BlockSpec((tm, D_MODEL), row),
            _const_spec((1, D_MODEL)),
            pl.BlockSpec((D_MODEL, tf), lambda b, t, j: (0, j)),
            pl.BlockSpec((D_MODEL, tf), lambda b, t, j: (0, nj + j)),
            pl.BlockSpec((CONV_W, tf), lambda b, t, j: (0, j)),
            pl.BlockSpec((CONV_W, tf), lambda b, t, j: (0, nj + j)),
            pl.BlockSpec((1, tf), lambda b, t, j: (0, j)),
            pl.BlockSpec((1, tf), lambda b, t, j: (0, nj + j)),
            pl.BlockSpec((tf, D_MODEL), lambda b, t, j: (j, 0)),
            pl.BlockSpec((1, SUBLANES, tf), lambda b, t, j: (b, 0, j)),
            pl.BlockSpec((1, SUBLANES, tf), lambda b, t, j: (b, 0, j)),
        ],
        out_specs=[
            pl.BlockSpec((tm, D_MODEL), row),
            pl.BlockSpec((1, SUBLANES, D_FF), lambda b, t, j: (b, 0, 0)),
            pl.BlockSpec((1, SUBLANES, D_FF), lambda b, t, j: (b, 0, 0)),
        ],
        out_shape=[
            jax.ShapeDtypeStruct((n, D_MODEL), F32),
            jax.ShapeDtypeStruct((nb, SUBLANES, D_FF), F32),
            jax.ShapeDtypeStruct((nb, SUBLANES, D_FF), F32),
        ],
        scratch_shapes=[
            pltpu.VMEM((tm, D_MODEL), BF16),
            pltpu.VMEM((tm + SUBLANES, tf), F32),
            pltpu.VMEM((tm + SUBLANES, tf), F32),
        ],
        compiler_params=pltpu.CompilerParams(
            dimension_semantics=("arbitrary", "arbitrary", "arbitrary"), vmem_limit_bytes=VMEM_LIMIT),
        name="ffn_seq",
    )(x1, g, wup_bf, wup_bf, conv_w, conv_w, conv_b, conv_b, wdown_bf, bufv, bufg)


def _ffn_step_kernel(x_ref, g_ref, wv_ref, wg_ref, cwv_ref, cwg_ref, cbv_ref, cbg_ref, wd_ref,
                     buf_ref, y_ref, nbuf_ref, h_s):
    j = pl.program_id(0)

    @pl.when(j == 0)
    def _():
        x = x_ref[...]
        h_s[...] = _rms(x, g_ref[...]).astype(BF16)
        y_ref[...] = x

    h = h_s[...]
    upv = _mm(h, wv_ref[...])
    upg = _mm(h, wg_ref[...])
    val = cbv_ref[...] + cwv_ref[0:1, :] * buf_ref[:, 0, :] + cwv_ref[1:2, :] * buf_ref[:, 2, :] + cwv_ref[2:3, :] * upv
    gate = cbg_ref[...] + cwg_ref[0:1, :] * buf_ref[:, 1, :] + cwg_ref[1:2, :] * buf_ref[:, 3, :] + cwg_ref[2:3, :] * upg
    act = (jax.nn.silu(gate) * val).astype(BF16)
    y_ref[...] += _mm(act, wd_ref[...])
    nbuf_ref[:, 0, :] = buf_ref[:, 2, :]
    nbuf_ref[:, 1, :] = buf_ref[:, 3, :]
    nbuf_ref[:, 2, :] = upv
    nbuf_ref[:, 3, :] = upg


def _ffn_step(x1, g, wup_bf, conv_w, conv_b, wdown_bf, buf4):
    nseq = x1.shape[0]
    tf = FF_TILE
    nj = D_FF // tf
    return pl.pallas_call(
        _ffn_step_kernel,
        grid=(nj,),
        in_specs=[
            _const_spec((nseq, D_MODEL)),
            _const_spec((1, D_MODEL)),
            pl.BlockSpec((D_MODEL, tf), lambda j: (0, j)),
            pl.BlockSpec((D_MODEL, tf), lambda j: (0, nj + j)),
            pl.BlockSpec((CONV_W, tf), lambda j: (0, j)),
            pl.BlockSpec((CONV_W, tf), lambda j: (0, nj + j)),
            pl.BlockSpec((1, tf), lambda j: (0, j)),
            pl.BlockSpec((1, tf), lambda j: (0, nj + j)),
            pl.BlockSpec((tf, D_MODEL), lambda j: (j, 0)),
            pl.BlockSpec((nseq, 2 * (CONV_W - 1), tf), lambda j: (0, 0, j)),
        ],
        out_specs=[
            pl.BlockSpec((nseq, D_MODEL), lambda j: (0, 0)),
            pl.BlockSpec((nseq, 2 * (CONV_W - 1), tf), lambda j: (0, 0, j)),
        ],
        out_shape=[
            jax.ShapeDtypeStruct((nseq, D_MODEL), F32),
            jax.ShapeDtypeStruct((nseq, 2 * (CONV_W - 1), D_FF), F32),
        ],
        scratch_shapes=[pltpu.VMEM((nseq, D_MODEL), BF16)],
        compiler_params=pltpu.CompilerParams(
            dimension_semantics=("arbitrary",), vmem_limit_bytes=VMEM_LIMIT),
        name="ffn_step",
    )(x1, g, wup_bf, wup_bf, conv_w, conv_w, conv_b, conv_b, wdown_bf, buf4)


def _rope_tables(pos):
    half = HEAD_DIM // 2
    inv = ROPE_THETA ** (-jnp.arange(half, dtype=F32) * 2.0 / HEAD_DIM)
    ang = pos.astype(F32)[:, None] * inv[None, :]
    c = jnp.cos(ang)
    s = jnp.sin(ang)
    return jnp.concatenate([c, c, c, c], axis=1), jnp.concatenate([-s, s, -s, s], axis=1)


def _block_diag(w):
    nd, ng, r, c = w.shape
    eye = jnp.eye(ng, dtype=w.dtype)
    return jnp.einsum('dgrc,gh->dgrhc', w, eye).reshape(nd, ng * r, ng * c)


def kernel(x_prompt, x_sample, cache_k, cache_v, state_ssm_re, state_ssm_im, state_conv, meta_tokens,
           norm_mix_g, w_in, q_norm_g, k_norm_g, attn_sinks, lam_re, lam_im, log_dt, ssm_b_re, ssm_b_im,
           ssm_c_re, ssm_c_im, ssm_d, w_glu, b_glu, ssm_out_g, attn_out_g, w_out, norm_ffn_g, w_up,
           conv_w, conv_b, w_down):
    depth = w_in.shape[0]
    assert depth == 1
    nb, seq, _ = x_prompt.shape
    nseq = x_sample.shape[0]
    n_meta = meta_tokens.shape[0]
    assert x_sample.shape[1] == 1 and n_meta == 16 and nb == SUBLANES
    assert seq % 1024 == 0 and nseq % SUBLANES == 0
    l = 0
    gpd = SSM_GROUPS_PER_DOT
    n_dots = N_SSM_GROUPS // gpd

    w_in_bf = w_in[l].astype(BF16)
    w_glu_bf = w_glu[l].astype(BF16)
    w_out_bf = w_out[l].astype(BF16)
    w_up_bf = w_up[l].astype(BF16)
    w_down_bf = w_down[l].astype(BF16)
    g_mix = norm_mix_g[l][None, :]
    g_ffn = norm_ffn_g[l][None, :]
    qg = jnp.tile(q_norm_g[l], 256 // HEAD_DIM)[None, :]
    kg = jnp.tile(k_norm_g[l], 256 // HEAD_DIM)[None, :]
    lane_head = jnp.arange(256) // HEAD_DIM
    ones_bd = (lane_head[:, None] == lane_head[None, :]).astype(BF16)
    sinks = attn_sinks[l]
    lamre = lam_re[l].reshape(1, D_STATE)
    lamim = lam_im[l].reshape(1, D_STATE)
    logdt = jnp.repeat(log_dt[l], SSM_STATE)[None, :]
    b_re_t = jnp.swapaxes(ssm_b_re[l], 1, 2).reshape(n_dots, gpd, SSM_GROUP, SSM_STATE)
    b_im_t = jnp.swapaxes(ssm_b_im[l], 1, 2).reshape(n_dots, gpd, SSM_GROUP, SSM_STATE)
    bblk = jnp.concatenate([_block_diag(b_re_t), _block_diag(b_im_t)], axis=2)
    cre = _block_diag(jnp.swapaxes(ssm_c_re[l], 1, 2).reshape(n_dots, gpd, SSM_STATE, SSM_GROUP)).astype(BF16)
    cim = _block_diag(jnp.swapaxes(ssm_c_im[l], 1, 2).reshape(n_dots, gpd, SSM_STATE, SSM_GROUP)).astype(BF16)
    dskip = ssm_d[l].reshape(1, D_SSM)
    bglu = b_glu[l][None, :]
    g_ssm = ssm_out_g[l][None, :]
    g_att = attn_out_g[l][None, :]
    cw = conv_w[l]
    cb = conv_b[l][None, :]

    ssm_w = (lamre, lamim, logdt, bblk, cre, cim, dskip, w_glu_bf, bglu, g_ssm)

    pad = BLOCK - n_meta
    xm = jnp.concatenate([jnp.zeros((pad, D_MODEL), F32), meta_tokens.astype(F32)], axis=0)[None]
    cos_m, sin_m = _rope_tables(jnp.maximum(jnp.arange(BLOCK) - pad, 0))
    u_m, q_m, k_m, v_m = _in_proj(xm, g_mix, w_in_bf, cos_m, sin_m, qg, kg, ones_bd, tq=BLOCK)
    u_m8 = jnp.broadcast_to(u_m, (BLOCK, SUBLANES, D_SSM)).reshape(BLOCK * SUBLANES, D_SSM)
    zero_state = jnp.zeros((SUBLANES, D_STATE), F32)
    ys_m8, hre_m, him_m = _ssm(u_m8, zero_state, zero_state, *ssm_w, nbatch=SUBLANES, tt=SCAN_STEPS)
    zero_kv = jnp.zeros((1, BLOCK, D_KV), F32)
    ya_m = _attn_seq(sinks, q_m, k_m, v_m, zero_kv, zero_kv, g_att, n_meta=n_meta, first_block=0)
    x1_m = _out_proj(xm, ys_m8[0:1], ya_m, w_out_bf, tm=BLOCK)
    zero_buf = jnp.zeros((1, SUBLANES, D_FF), F32)
    _, bufv_m, bufg_m = _ffn_seq(x1_m[0], g_ffn, w_up_bf, cw, cb, w_down_bf, zero_buf, zero_buf, nb=1, tm=BLOCK)

    cos_p, sin_p = _rope_tables(n_meta + jnp.arange(seq))
    u_p, q_p, k_p, v_p = _in_proj(x_prompt, g_mix, w_in_bf, cos_p, sin_p, qg, kg, ones_bd, tq=128)
    ys_p, hre_p, him_p = _ssm(u_p.reshape(seq * nb, D_SSM), hre_m, him_m, *ssm_w, nbatch=nb, tt=SCAN_STEPS)
    k0 = jnp.broadcast_to(k_m, (nb, BLOCK, D_KV))
    v0 = jnp.broadcast_to(v_m, (nb, BLOCK, D_KV))
    ya_p = _attn_seq(sinks, q_p, k_p, v_p, k0, v0, g_att, n_meta=n_meta, first_block=1)
    x1_p = _out_proj(x_prompt, ys_p, ya_p, w_out_bf, tm=512)
    bufv0 = jnp.broadcast_to(bufv_m, (nb, SUBLANES, D_FF))
    bufg0 = jnp.broadcast_to(bufg_m, (nb, SUBLANES, D_FF))
    y_p, bufv_p, bufg_p = _ffn_seq(x1_p.reshape(nb * seq, D_MODEL), g_ffn, w_up_bf, cw, cb, w_down_bf,
                                   bufv0, bufg0, nb=nb, tm=1024)

    y_prompt = y_p.reshape(nb, seq, D_MODEL)
    prompt_k = k_p[:, seq - WINDOW:].reshape(nb, WINDOW, N_KV_HEADS, HEAD_DIM)[None]
    prompt_v = v_p[:, seq - WINDOW:].reshape(nb, WINDOW, N_KV_HEADS, HEAD_DIM)[None]
    prompt_re = hre_p.reshape(nb, N_SSM_GROUPS, SSM_STATE)[None]
    prompt_im = him_p.reshape(nb, N_SSM_GROUPS, SSM_STATE)[None]
    prompt_conv = jnp.concatenate([bufv_p[:, SUBLANES - 2:], bufg_p[:, SUBLANES - 2:]], axis=-1)[None]

    xs = x_sample.reshape(1, nseq, D_MODEL)
    cos_s, sin_s = _rope_tables(jnp.full((nseq,), PAST_LEN, jnp.int32))
    u_s, q_s, k_s, v_s = _in_proj(xs, g_mix, w_in_bf, cos_s, sin_s, qg, kg, ones_bd, tq=nseq)
    ys_s, hre_s, him_s = _ssm(u_s.reshape(nseq, D_SSM), state_ssm_re[l].reshape(nseq, D_STATE),
                              state_ssm_im[l].reshape(nseq, D_STATE), *ssm_w, nbatch=nseq, tt=1)
    ya_s, nk_s, nv_s = _attn_step(
        q_s.reshape(nseq, N_HEADS, HEAD_DIM), k_s.reshape(nseq, N_KV_HEADS, HEAD_DIM),
        v_s.reshape(nseq, N_KV_HEADS, HEAD_DIM),
        cache_k[l].reshape(nseq, WINDOW * N_KV_HEADS, HEAD_DIM),
        cache_v[l].reshape(nseq, WINDOW * N_KV_HEADS, HEAD_DIM),
        sinks[:, None], attn_out_g[l].reshape(N_HEADS, HEAD_DIM), bb=8)
    x1_s = _out_proj(xs, ys_s, ya_s.reshape(1, nseq, D_ATTN), w_out_bf, tm=nseq)
    buf4 = state_conv[l].reshape(nseq, 2 * (CONV_W - 1), D_FF)
    y_s, nbuf4 = _ffn_step(x1_s[0], g_ffn, w_up_bf, cw, cb, w_down_bf, buf4)

    y_sample = y_s.reshape(nseq, 1, D_MODEL)
    sample_k = nk_s.reshape(nseq, WINDOW, N_KV_HEADS, HEAD_DIM)[None]
    sample_v = nv_s.reshape(nseq, WINDOW, N_KV_HEADS, HEAD_DIM)[None]
    sample_re = hre_s.reshape(nseq, N_SSM_GROUPS, SSM_STATE)[None]
    sample_im = him_s.reshape(nseq, N_SSM_GROUPS, SSM_STATE)[None]
    sample_conv = nbuf4.reshape(nseq, CONV_W - 1, 2 * D_FF)[None]

    return (y_prompt, y_sample, prompt_k, prompt_v, prompt_re, prompt_im, prompt_conv,
            sample_k, sample_v, sample_re, sample_im, sample_conv)
```

```python
import functools

import jax
import jax.numpy as jnp
from jax import lax
from jax.experimental import pallas as pl
from jax.experimental.pallas import tpu as pltpu

F32 = jnp.float32
BF16 = jnp.bfloat16

D_MODEL = 2048
D_SSM = 1024
SSM_GROUP = 16
N_SSM_GROUPS = 64
SSM_STATE = 64
D_STATE = N_SSM_GROUPS * SSM_STATE
HEAD_DIM = 64
N_HEADS = 16
N_KV_HEADS = 4
Q_PER_KV = 4
D_ATTN = 1024
D_KV = 256
D_IN = D_SSM + D_ATTN + 2 * D_KV
WINDOW = 128
BLOCK = 128
ROPE_THETA = 10000.0
D_FF = 5632
CONV_W = 3
EPS = 1e-6
NEG = -1e30
PAST_LEN = 16384

SUBLANES = 8
LANES = 128
VMEM_LIMIT = 60 * 1024 * 1024
FF_TILE = 512
FFN_SUB_ROWS = 256
IN_SUB_ROWS = 256
ATTN_Q_BLOCKS = 2
SCAN_LANES = 512
SCAN_STEPS = 32
SSM_GROUPS_PER_DOT = 16


def _const_spec(shape):
    nd = len(shape)
    return pl.BlockSpec(shape, lambda *_: (0,) * nd, pipeline_mode=pl.Buffered(1))


def _rms(x, g):
    return x * lax.rsqrt(jnp.mean(x * x, axis=-1, keepdims=True) + EPS) * g


def _mm(a, b):
    return jnp.dot(a, b, preferred_element_type=F32)


def _in_proj_kernel(x_ref, g_ref, w_ref, cos_ref, sin_ref, qg_ref, kg_ref, ones_ref,
                    u_ref, q_ref, k_ref, v_ref, *, nb, tq, n_split):
    rs = nb * tq // n_split
    cos1 = jnp.concatenate([cos_ref[...], cos_ref[...]], axis=1)
    sin1 = jnp.concatenate([sin_ref[...], sin_ref[...]], axis=1)
    ones = ones_ref[...]
    n_q = D_ATTN // 256
    for s in range(n_split):
        segs = []
        for b in range(nb):
            lo = max(s * rs, b * tq)
            hi = min((s + 1) * rs, (b + 1) * tq)
            if lo < hi:
                segs.append((b, lo - b * tq, hi - lo, lo - s * rs))
        x = jnp.concatenate([x_ref[b, t0:t0 + n, :] for b, t0, n, _ in segs], axis=0)
        cos = jnp.concatenate([cos1[t0:t0 + n, :] for _, t0, n, _ in segs], axis=0)
        sin = jnp.concatenate([sin1[t0:t0 + n, :] for _, t0, n, _ in segs], axis=0)
        lane = lax.broadcasted_iota(jnp.int32, cos.shape, 1)
        first_half = (lane % HEAD_DIM) < (HEAD_DIM // 2)
        h = _rms(x, g_ref[...]).astype(BF16)
        z = _mm(h, w_ref[...])
        for b, t0, n, r0 in segs:
            u_ref[t0:t0 + n, b, :] = z[r0:r0 + n, :D_SSM]
            v_ref[b, t0:t0 + n, :] = z[r0:r0 + n, D_SSM + D_ATTN + D_KV:]
        for c in range(n_q + D_KV // 256):
            zc = z[:, D_SSM + 256 * c: D_SSM + 256 * (c + 1)]
            sq = zc * zc
            hi = sq.astype(BF16)
            lo = (sq - hi.astype(F32)).astype(BF16)
            ssum = _mm(hi, ones) + _mm(lo, ones)
            gain = qg_ref[...] if c < n_q else kg_ref[...]
            nrm = zc * lax.rsqrt(ssum * (1.0 / HEAD_DIM) + EPS) * gain
            partner = jnp.where(first_half, pltpu.roll(nrm, 256 - HEAD_DIM // 2, 1),
                                pltpu.roll(nrm, HEAD_DIM // 2, 1))
            r = nrm * cos + partner * sin
            for b, t0, n, r0 in segs:
                if c < n_q:
                    q_ref[b, t0:t0 + n, 256 * c: 256 * (c + 1)] = (
                        r[r0:r0 + n, :] * (HEAD_DIM ** -0.5)).astype(BF16)
                else:
                    k_ref[b, t0:t0 + n, :] = r[r0:r0 + n, :]


def _in_proj(x3, g, w_bf, cos, sin, qg, kg, ones_bd, *, tq):
    nb, t_len, _ = x3.shape
    blk = lambda t: (0, t, 0)
    kern = functools.partial(_in_proj_kernel, nb=nb, tq=tq, n_split=max(1, nb * tq // IN_SUB_ROWS))
    return pl.pallas_call(
        kern,
        grid=(t_len // tq,),
        in_specs=[
            pl.BlockSpec((nb, tq, D_MODEL), blk),
            _const_spec((1, D_MODEL)),
            _const_spec((D_MODEL, D_IN)),
            pl.BlockSpec((tq, LANES), lambda t: (t, 0)),
            pl.BlockSpec((tq, LANES), lambda t: (t, 0)),
            _const_spec((1, 256)),
            _const_spec((1, 256)),
            _const_spec((256, 256)),
        ],
        out_specs=[
            pl.BlockSpec((tq, nb, D_SSM), lambda t: (t, 0, 0)),
            pl.BlockSpec((nb, tq, D_ATTN), blk),
            pl.BlockSpec((nb, tq, D_KV), blk),
            pl.BlockSpec((nb, tq, D_KV), blk),
        ],
        out_shape=[
            jax.ShapeDtypeStruct((t_len, nb, D_SSM), F32),
            jax.ShapeDtypeStruct((nb, t_len, D_ATTN), BF16),
            jax.ShapeDtypeStruct((nb, t_len, D_KV), F32),
            jax.ShapeDtypeStruct((nb, t_len, D_KV), F32),
        ],
        compiler_params=pltpu.CompilerParams(
            dimension_semantics=("arbitrary",), vmem_limit_bytes=VMEM_LIMIT),
        name="in_proj",
    )(x3, g, w_bf, cos, sin, qg, kg, ones_bd)


def _ssm_kernel(u_ref, h0re_ref, h0im_ref, lamre_ref, lamim_ref, logdt_ref, bblk_ref,
                cre_ref, cim_ref, dskip_ref, wglu_ref, bglu_ref, g_ref, perm_ref,
                y_ref, sre_ref, sim_ref,
                are_s, aim_s, bbar_s, xre_s, xim_s, stre_s, stim_s, *, tt, nbg):
    step = pl.program_id(0)
    lanes_per_dot = SSM_GROUPS_PER_DOT * SSM_STATE
    chans_per_dot = SSM_GROUPS_PER_DOT * SSM_GROUP
    n_dots = N_SSM_GROUPS // SSM_GROUPS_PER_DOT
    state_lanes = [slice(lanes_per_dot * c, lanes_per_dot * (c + 1)) for c in range(n_dots)]
    chans = [slice(chans_per_dot * c, chans_per_dot * (c + 1)) for c in range(n_dots)]

    @pl.when(step == 0)
    def _():
        lr = lamre_ref[...]
        li = lamim_ref[...]
        dt = jnp.exp(logdt_ref[...])
        mag = jnp.exp(lr * dt)
        a_re = mag * jnp.cos(li * dt)
        a_im = mag * jnp.sin(li * dt)
        den = lr * lr + li * li
        am1 = a_re - 1.0
        cf_re = (am1 * lr + a_im * li) / den
        cf_im = (a_im * lr - am1 * li) / den
        shape = (SUBLANES, D_STATE)
        are_s[...] = jnp.broadcast_to(a_re, shape)
        aim_s[...] = jnp.broadcast_to(a_im, shape)
        for c in range(n_dots):
            b_re = bblk_ref[c, :, :lanes_per_dot]
            b_im = bblk_ref[c, :, lanes_per_dot:]
            c_re = cf_re[:, state_lanes[c]]
            c_im = cf_im[:, state_lanes[c]]
            bbar_s[c, :, :lanes_per_dot] = (c_re * b_re - c_im * b_im).astype(BF16)
            bbar_s[c, :, lanes_per_dot:] = (c_re * b_im + c_im * b_re).astype(BF16)
        stre_s[...] = h0re_ref[...]
        stim_s[...] = h0im_ref[...]

    u = u_ref[...]
    ub = u.astype(BF16)

    def project(c):
        bu = _mm(ub[:, chans[c]], bbar_s[c])
        xre_s[:, state_lanes[c]] = bu[:, :lanes_per_dot]
        xim_s[:, state_lanes[c]] = bu[:, lanes_per_dot:]

    def scan(c):
        blocks = [slice(lanes_per_dot * c + SCAN_LANES * i, lanes_per_dot * c + SCAN_LANES * (i + 1))
                  for i in range(lanes_per_dot // SCAN_LANES)]
        for bg in range(nbg):
            rs = slice(SUBLANES * bg, SUBLANES * (bg + 1))
            a = [(are_s[:, ls], aim_s[:, ls]) for ls in blocks]
            st = [(stre_s[rs, ls], stim_s[rs, ls]) for ls in blocks]
            for t in range(tt):
                rt = slice((t * nbg + bg) * SUBLANES, (t * nbg + bg + 1) * SUBLANES)
                for i, ls in enumerate(blocks):
                    (a_re, a_im), (s_re, s_im) = a[i], st[i]
                    n_re = a_re * s_re - a_im * s_im + xre_s[rt, ls]
                    n_im = a_re * s_im + a_im * s_re + xim_s[rt, ls]
                    xre_s[rt, ls] = n_re
                    xim_s[rt, ls] = n_im
                    st[i] = (n_re, n_im)
            for i, ls in enumerate(blocks):
                stre_s[rs, ls] = st[i][0]
                stim_s[rs, ls] = st[i][1]

    def readout(c):
        yc = (_mm(xre_s[:, state_lanes[c]].astype(BF16), cre_ref[c])
              - _mm(xim_s[:, state_lanes[c]].astype(BF16), cim_ref[c]))
        return jax.nn.gelu(yc + dskip_ref[:, chans[c]] * u[:, chans[c]])

    ys = [None] * n_dots
    project(0)
    for c in range(n_dots):
        if c + 1 < n_dots:
            project(c + 1)
        scan(c)
        if c > 0:
            ys[c - 1] = readout(c - 1)
    ys[n_dots - 1] = readout(n_dots - 1)
    y = jnp.concatenate(ys, axis=1)
    y = y * jax.nn.sigmoid(_mm(y.astype(BF16), wglu_ref[...]) + bglu_ref[...])
    yn = _rms(y, g_ref[...]).astype(BF16)
    if tt == 1:
        y_ref[0] = yn
    else:
        y_ref[...] = _mm(perm_ref[...], yn).astype(BF16).reshape(y_ref.shape)

    @pl.when(step == pl.num_programs(0) - 1)
    def _():
        sre_ref[...] = stre_s[...]
        sim_ref[...] = stim_s[...]


def _ssm(u_tm, h0re, h0im, lamre, lamim, logdt, bblk, cre, cim, dskip, wglu, bglu, g, *, nbatch, tt):
    n = u_tm.shape[0]
    t_len = n // nbatch
    nbg = nbatch // SUBLANES
    rows = tt * nbatch
    n_grp = nbatch if tt > 1 else 1
    r = jnp.arange(rows)
    src = (r % tt) * nbatch + r // tt
    perm = (src[:, None] == r[None, :]).astype(BF16)
    kern = functools.partial(_ssm_kernel, tt=tt, nbg=nbg)
    return pl.pallas_call(
        kern,
        grid=(t_len // tt,),
        in_specs=[
            pl.BlockSpec((rows, D_SSM), lambda s: (s, 0)),
            _const_spec((nbatch, D_STATE)),
            _const_spec((nbatch, D_STATE)),
            _const_spec((1, D_STATE)),
            _const_spec((1, D_STATE)),
            _const_spec((1, D_STATE)),
            _const_spec(bblk.shape),
            _const_spec(cre.shape),
            _const_spec(cim.shape),
            _const_spec((1, D_SSM)),
            _const_spec((D_SSM, D_SSM)),
            _const_spec((1, D_SSM)),
            _const_spec((1, D_SSM)),
            _const_spec((rows, rows)),
        ],
        out_specs=[
            pl.BlockSpec((n_grp, rows // n_grp, D_SSM), lambda s: (0, s, 0)),
            pl.BlockSpec((nbatch, D_STATE), lambda s: (0, 0)),
            pl.BlockSpec((nbatch, D_STATE), lambda s: (0, 0)),
        ],
        out_shape=[
            jax.ShapeDtypeStruct((n_grp, n // n_grp, D_SSM), BF16),
            jax.ShapeDtypeStruct((nbatch, D_STATE), F32),
            jax.ShapeDtypeStruct((nbatch, D_STATE), F32),
        ],
        scratch_shapes=[pltpu.VMEM((SUBLANES, D_STATE), F32)] * 2
        + [pltpu.VMEM(bblk.shape, BF16)]
        + [pltpu.VMEM((rows, D_STATE), F32)] * 2
        + [pltpu.VMEM((nbatch, D_STATE), F32)] * 2,
        compiler_params=pltpu.CompilerParams(
            dimension_semantics=("arbitrary",), vmem_limit_bytes=VMEM_LIMIT),
        name="ssm",
    )(u_tm, h0re, h0im, lamre, lamim, logdt, bblk, cre, cim, dskip, wglu, bglu, g, perm)


def _attn_seq_kernel(sink_ref, q_ref, kc_ref, kp_ref, k0_ref, vc_ref, vp_ref, v0_ref, g_ref, o_ref,
                     *, first_block, n_meta, qb):
    i = pl.program_id(1)
    is_first = i == 0
    r = lax.broadcasted_iota(jnp.int32, (BLOCK, 4 * BLOCK), 0)
    c = lax.broadcasted_iota(jnp.int32, (BLOCK, 4 * BLOCK), 1) % (2 * BLOCK)
    in_window = (c > r) & (c <= r + WINDOW)
    lo_half = lax.broadcasted_iota(jnp.int32, (2 * BLOCK, LANES), 1) < HEAD_DIM

    def pair_blocks(x2, kv):
        xc = x2[:, LANES * (kv // 2): LANES * (kv // 2 + 1)]
        xr = pltpu.roll(xc, HEAD_DIM, 1)
        zero = jnp.zeros_like(xc)
        if kv % 2 == 0:
            low, high = jnp.where(lo_half, xc, zero), jnp.where(lo_half, zero, xr)
        else:
            low, high = jnp.where(lo_half, xr, zero), jnp.where(lo_half, zero, xc)
        return jnp.concatenate([low, high], axis=0).astype(BF16)

    n_pairs = N_HEADS // 2
    pairs_per_kv = Q_PER_KV // 2
    row_hi = lax.broadcasted_iota(jnp.int32, (4 * BLOCK, LANES), 0) >= 2 * BLOCK
    lane_hi = lax.broadcasted_iota(jnp.int32, (4 * BLOCK, LANES), 1) >= HEAD_DIM
    ones_cols = (row_hi == lane_hi).astype(BF16)
    kbd, vbd, mask, rows = [], [], [], []
    for sb in range(qb):
        rows.append(slice(BLOCK * sb, BLOCK * (sb + 1)))
        if sb == 0:
            k_prev = jnp.where(is_first, k0_ref[0], kp_ref[0])
            v_prev = jnp.where(is_first, v0_ref[0], vp_ref[0])
        else:
            k_prev, v_prev = kc_ref[0, rows[sb - 1], :], vc_ref[0, rows[sb - 1], :]
        k2 = jnp.concatenate([k_prev, kc_ref[0, rows[sb], :]], axis=0)
        v2 = jnp.concatenate([v_prev, vc_ref[0, rows[sb], :]], axis=0)
        gb = i * qb + sb + first_block
        col_min = jnp.where(gb == 0, 2 * BLOCK - n_meta, jnp.where(gb == 1, BLOCK - n_meta, 0))
        mask.append(in_window & (c >= col_min))
        kbd.append([pair_blocks(k2, kv) for kv in range(N_KV_HEADS)])
        vbd.append([jnp.concatenate([pair_blocks(v2, kv), ones_cols], axis=1) for kv in range(N_KV_HEADS)])
    halves = (slice(0, 2 * BLOCK), slice(2 * BLOCK, 4 * BLOCK))
    probs = [(sb, pr) for sb in range(qb) for pr in range(n_pairs)]
    s = {(sb, pr): jnp.where(mask[sb], lax.dot_general(
        q_ref[0, rows[sb], LANES * pr: LANES * (pr + 1)], kbd[sb][pr // pairs_per_kv],
        (((1,), (1,)), ((), ())), preferred_element_type=F32), NEG) for sb, pr in probs}
    m = {(sb, pr): [jnp.maximum(jnp.max(s[sb, pr][:, halves[hh]], axis=-1, keepdims=True), sink_ref[2 * pr + hh])
                    for hh in range(2)] for sb, pr in probs}
    p = {(sb, pr): jnp.concatenate([jnp.exp(s[sb, pr][:, halves[hh]] - m[sb, pr][hh]) for hh in range(2)],
                                   axis=1).astype(BF16) for sb, pr in probs}
    out_hi = lax.broadcasted_iota(jnp.int32, (BLOCK, LANES), 1) >= HEAD_DIM
    outs = {}
    for sb, pr in probs:
        ov = _mm(p[sb, pr], vbd[sb][pr // pairs_per_kv])
        sink_w = jnp.where(out_hi, jnp.exp(sink_ref[2 * pr + 1] - m[sb, pr][1]),
                           jnp.exp(sink_ref[2 * pr] - m[sb, pr][0]))
        outs[sb, pr] = ov[:, :LANES] / (ov[:, LANES:] + sink_w)
    for sb in range(qb):
        o = jnp.concatenate([outs[sb, pr] for pr in range(n_pairs)], axis=1)
        o_ref[0, rows[sb], :] = _rms(o, g_ref[...]).astype(BF16)


def _attn_seq(sinks, q, k, v, k0, v0, g, *, n_meta, first_block):
    nb, t_len, _ = q.shape
    nblk = t_len // BLOCK
    qb = ATTN_Q_BLOCKS if nblk % ATTN_Q_BLOCKS == 0 else 1
    cur = lambda b, i: (b, i, 0)
    prev = lambda b, i: (b, jnp.maximum(i * qb - 1, 0), 0)
    init = lambda b, i: (b, 0, 0)
    kern = functools.partial(_attn_seq_kernel, first_block=first_block, n_meta=n_meta, qb=qb)
    return pl.pallas_call(
        kern,
        grid=(nb, nblk // qb),
        in_specs=[
            pl.BlockSpec(memory_space=pltpu.SMEM),
            pl.BlockSpec((1, qb * BLOCK, D_ATTN), cur),
            pl.BlockSpec((1, qb * BLOCK, D_KV), cur),
            pl.BlockSpec((1, BLOCK, D_KV), prev),
            pl.BlockSpec((1, BLOCK, D_KV), init),
            pl.BlockSpec((1, qb * BLOCK, D_KV), cur),
            pl.BlockSpec((1, BLOCK, D_KV), prev),
            pl.BlockSpec((1, BLOCK, D_KV), init),
            _const_spec((1, D_ATTN)),
        ],
        out_specs=pl.BlockSpec((1, qb * BLOCK, D_ATTN), cur),
        out_shape=jax.ShapeDtypeStruct((nb, t_len, D_ATTN), BF16),
        compiler_params=pltpu.CompilerParams(
            dimension_semantics=("arbitrary", "arbitrary"), vmem_limit_bytes=VMEM_LIMIT),
        name="attn_seq",
    )(sinks, q, k, k, k0, v, v, v0, g)


def _attn_step_kernel(q_ref, kn_ref, vn_ref, ck_ref, cv_ref, sink_ref, g_ref, o_ref, ok_ref, ov_ref):
    n_rows = WINDOW * N_KV_HEADS
    k_new = jnp.concatenate([ck_ref[:, N_KV_HEADS:n_rows, :], kn_ref[...]], axis=1)
    v_new = jnp.concatenate([cv_ref[:, N_KV_HEADS:n_rows, :], vn_ref[...]], axis=1)
    ok_ref[...] = k_new.reshape(ok_ref.shape)
    ov_ref[...] = v_new.reshape(ov_ref.shape)
    q = q_ref[...]
    s = jnp.einsum('bhd,bcd->bhc', q, k_new.astype(BF16), preferred_element_type=F32)
    hrow = lax.broadcasted_iota(jnp.int32, (N_HEADS, n_rows), 0)
    hcol = lax.broadcasted_iota(jnp.int32, (N_HEADS, n_rows), 1)
    own = (hcol % N_KV_HEADS) == (hrow // Q_PER_KV)
    s = jnp.where(own[None], s, NEG)
    sink = sink_ref[...][None]
    m = jnp.maximum(jnp.max(s, axis=-1, keepdims=True), sink)
    p = jnp.exp(s - m)
    den = jnp.sum(p, axis=-1, keepdims=True) + jnp.exp(sink - m)
    o = jnp.einsum('bhc,bcd->bhd', (p / den).astype(BF16), v_new.astype(BF16), preferred_element_type=F32)
    ms = jnp.sum(jnp.sum(o * o, axis=2, keepdims=True), axis=1, keepdims=True) * (1.0 / D_ATTN)
    o_ref[...] = (o * lax.rsqrt(ms + EPS) * g_ref[...][None]).astype(BF16)


def _attn_step(q3, kn, vn, ck, cv, sinks, g, *, bb):
    nseq = q3.shape[0]
    blk3 = lambda i: (i, 0, 0)
    blk4 = lambda i: (i, 0, 0, 0)
    cache_blk = (bb, WINDOW, N_KV_HEADS, HEAD_DIM)
    return pl.pallas_call(
        _attn_step_kernel,
        grid=(nseq // bb,),
        in_specs=[
            pl.BlockSpec((bb, N_HEADS, HEAD_DIM), blk3),
            pl.BlockSpec((bb, N_KV_HEADS, HEAD_DIM), blk3),
            pl.BlockSpec((bb, N_KV_HEADS, HEAD_DIM), blk3),
            pl.BlockSpec((bb, WINDOW * N_KV_HEADS, HEAD_DIM), blk3),
            pl.BlockSpec((bb, WINDOW * N_KV_HEADS, HEAD_DIM), blk3),
            _const_spec((N_HEADS, 1)),
            _const_spec((N_HEADS, HEAD_DIM)),
        ],
        out_specs=[
            pl.BlockSpec((bb, N_HEADS, HEAD_DIM), blk3),
            pl.BlockSpec(cache_blk, blk4),
            pl.BlockSpec(cache_blk, blk4),
        ],
        out_shape=[
            jax.ShapeDtypeStruct((nseq, N_HEADS, HEAD_DIM), BF16),
            jax.ShapeDtypeStruct((nseq,) + cache_blk[1:], F32),
            jax.ShapeDtypeStruct((nseq,) + cache_blk[1:], F32),
        ],
        compiler_params=pltpu.CompilerParams(
            dimension_semantics=("arbitrary",), vmem_limit_bytes=VMEM_LIMIT),
        name="attn_step",
    )(q3, kn, vn, ck, cv, sinks, g)


def _out_proj_kernel(x_ref, ys_ref, ya_ref, w_ref, o_ref):
    o_ref[0] = (x_ref[0] + _mm(ys_ref[0], w_ref[0:D_SSM, :])
                + _mm(ya_ref[0], w_ref[D_SSM:D_SSM + D_ATTN, :]))


def _out_proj(x3, ys, ya, w_bf, *, tm):
    nb, t_len, _ = x3.shape
    blk = lambda b, t: (b, t, 0)
    return pl.pallas_call(
        _out_proj_kernel,
        grid=(nb, t_len // tm),
        in_specs=[
            pl.BlockSpec((1, tm, D_MODEL), blk),
            pl.BlockSpec((1, tm, D_SSM), blk),
            pl.BlockSpec((1, tm, D_ATTN), blk),
            _const_spec((D_SSM + D_ATTN, D_MODEL)),
        ],
        out_specs=pl.BlockSpec((1, tm, D_MODEL), blk),
        out_shape=jax.ShapeDtypeStruct((nb, t_len, D_MODEL), F32),
        compiler_params=pltpu.CompilerParams(
            dimension_semantics=("arbitrary", "arbitrary"), vmem_limit_bytes=VMEM_LIMIT),
        name="out_proj",
    )(x3, ys, ya, w_bf)


def _ffn_seq_kernel(x_ref, g_ref, wv_ref, wg_ref, cwv_ref, cwg_ref, cbv_ref, cbg_ref, wd_ref,
                    bufv_ref, bufg_ref, y_ref, nbv_ref, nbg_ref,
                    h_s, upv_s, upg_s, *, tm, tf):
    t = pl.program_id(1)
    j = pl.program_id(2)
    chan = pl.ds(pl.multiple_of(j * tf, tf), tf)

    @pl.when(j == 0)
    def _():
        x = x_ref[...]
        h_s[...] = _rms(x, g_ref[...]).astype(BF16)
        y_ref[...] = x

    @pl.when(t == 0)
    def _():
        upv_s[0:SUBLANES, :] = bufv_ref[0]
        upg_s[0:SUBLANES, :] = bufg_ref[0]

    @pl.when(t > 0)
    def _():
        upv_s[0:SUBLANES, :] = nbv_ref[0, :, chan]
        upg_s[0:SUBLANES, :] = nbg_ref[0, :, chan]

    rm = min(tm, FFN_SUB_ROWS)
    n_sub = tm // rm

    def up_proj(up_s, w_ref, r0):
        up_s[SUBLANES + r0:SUBLANES + r0 + rm, :] = _mm(h_s[r0:r0 + rm, :], w_ref[...])

    def conv(up_s, cw_ref, cb_ref, r0):
        return (cb_ref[...] + cw_ref[0:1, :] * up_s[SUBLANES - 2 + r0:SUBLANES - 2 + r0 + rm, :]
                + cw_ref[1:2, :] * up_s[SUBLANES - 1 + r0:SUBLANES - 1 + r0 + rm, :]
                + cw_ref[2:3, :] * up_s[SUBLANES + r0:SUBLANES + r0 + rm, :])

    def gate_down(r0):
        val = conv(upv_s, cwv_ref, cbv_ref, r0)
        gate = conv(upg_s, cwg_ref, cbg_ref, r0)
        act = (jax.nn.silu(gate) * val).astype(BF16)
        y_ref[r0:r0 + rm, :] += _mm(act, wd_ref[...])

    for r in range(n_sub):
        up_proj(upv_s, wv_ref, r * rm)
        up_proj(upg_s, wg_ref, r * rm)
        if r > 0:
            gate_down((r - 1) * rm)
    gate_down((n_sub - 1) * rm)
    nbv_ref[0, :, chan] = upv_s[tm:tm + SUBLANES, :]
    nbg_ref[0, :, chan] = upg_s[tm:tm + SUBLANES, :]


def _ffn_seq(x1, g, wup_bf, conv_w, conv_b, wdown_bf, bufv, bufg, *, nb, tm):
    n = x1.shape[0]
    nt = n // nb // tm
    tf = FF_TILE
    nj = D_FF // tf
    row = lambda b, t, j: (b * nt + t, 0)
    kern = functools.partial(_ffn_seq_kernel, tm=tm, tf=tf)
    return pl.pallas_call(
        kern,
        grid=(nb, nt, nj),
        in_specs=[
            pl.BlockSpec((tm, D_MODEL), row),
            _const_spec((1, D_MODEL)),
            pl.BlockSpec((D_MODEL, tf), lambda b, t, j: (0, j)),
            pl.BlockSpec((D_MODEL, tf), lambda b, t, j: (0, nj + j)),
            pl.BlockSpec((CONV_W, tf), lambda b, t, j: (0, j)),
            pl.BlockSpec((CONV_W, tf), lambda b, t, j: (0, nj + j)),
            pl.BlockSpec((1, tf), lambda b, t, j: (0, j)),
            pl.BlockSpec((1, tf), lambda b, t, j: (0, nj + j)),
            pl.BlockSpec((tf, D_MODEL), lambda b, t, j: (j, 0)),
            pl.BlockSpec((1, SUBLANES, tf), lambda b, t, j: (b, 0, j)),
            pl.BlockSpec((1, SUBLANES, tf), lambda b, t, j: (b, 0, j)),
        ],
        out_specs=[
            pl.BlockSpec((tm, D_MODEL), row),
            pl.BlockSpec((1, SUBLANES, D_FF), lambda b, t, j: (b, 0, 0)),
            pl.BlockSpec((1, SUBLANES, D_FF), lambda b, t, j: (b, 0, 0)),
        ],
        out_shape=[
            jax.ShapeDtypeStruct((n, D_MODEL), F32),
            jax.ShapeDtypeStruct((nb, SUBLANES, D_FF), F32),
            jax.ShapeDtypeStruct((nb, SUBLANES, D_FF), F32),
        ],
        scratch_shapes=[
            pltpu.VMEM((tm, D_MODEL), BF16),
            pltpu.VMEM((tm + SUBLANES, tf), F32),
            pltpu.VMEM((tm + SUBLANES, tf), F32),
        ],
        compiler_params=pltpu.CompilerParams(
            dimension_semantics=("arbitrary", "arbitrary", "arbitrary"), vmem_limit_bytes=VMEM_LIMIT),
        name="ffn_seq",
    )(x1, g, wup_bf, wup_bf, conv_w, conv_w, conv_b, conv_b, wdown_bf, bufv, bufg)


def _ffn_step_kernel(x_ref, g_ref, wv_ref, wg_ref, cwv_ref, cwg_ref, cbv_ref, cbg_ref, wd_ref,
                     bufv_ref, bufg_ref, y_ref, nbv_ref, nbg_ref, h_s):
    j = pl.program_id(0)

    @pl.when(j == 0)
    def _():
        x = x_ref[...]
        h_s[...] = _rms(x, g_ref[...]).astype(BF16)
        y_ref[...] = x

    h = h_s[...]
    upv = _mm(h, wv_ref[...])
    upg = _mm(h, wg_ref[...])
    val = cbv_ref[...] + cwv_ref[0:1, :] * bufv_ref[:, 0, :] + cwv_ref[1:2, :] * bufv_ref[:, 1, :] + cwv_ref[2:3, :] * upv
    gate = cbg_ref[...] + cwg_ref[0:1, :] * bufg_ref[:, 0, :] + cwg_ref[1:2, :] * bufg_ref[:, 1, :] + cwg_ref[2:3, :] * upg
    act = (jax.nn.silu(gate) * val).astype(BF16)
    y_ref[...] += _mm(act, wd_ref[...])
    nbv_ref[:, 0, :] = bufv_ref[:, 1, :]
    nbg_ref[:, 0, :] = bufg_ref[:, 1, :]
    nbv_ref[:, 1, :] = upv
    nbg_ref[:, 1, :] = upg


def _ffn_step(x1, g, wup_bf, conv_w, conv_b, wdown_bf, buf):
    nseq = x1.shape[0]
    tf = FF_TILE
    nj = D_FF // tf
    return pl.pallas_call(
        _ffn_step_kernel,
        grid=(nj,),
        in_specs=[
            _const_spec((nseq, D_MODEL)),
            _const_spec((1, D_MODEL)),
            pl.BlockSpec((D_MODEL, tf), lambda j: (0, j)),
            pl.BlockSpec((D_MODEL, tf), lambda j: (0, nj + j)),
            pl.BlockSpec((CONV_W, tf), lambda j: (0, j)),
            pl.BlockSpec((CONV_W, tf), lambda j: (0, nj + j)),
            pl.BlockSpec((1, tf), lambda j: (0, j)),
            pl.BlockSpec((1, tf), lambda j: (0, nj + j)),
            pl.BlockSpec((tf, D_MODEL), lambda j: (j, 0)),
            pl.BlockSpec((nseq, CONV_W - 1, tf), lambda j: (0, 0, j)),
            pl.BlockSpec((nseq, CONV_W - 1, tf), lambda j: (0, 0, nj + j)),
        ],
        out_specs=[
            pl.BlockSpec((nseq, D_MODEL), lambda j: (0, 0)),
            pl.BlockSpec((nseq, CONV_W - 1, tf), lambda j: (0, 0, j)),
            pl.BlockSpec((nseq, CONV_W - 1, tf), lambda j: (0, 0, j)),
        ],
        out_shape=[
            jax.ShapeDtypeStruct((nseq, D_MODEL), F32),
            jax.ShapeDtypeStruct((nseq, CONV_W - 1, D_FF), F32),
            jax.ShapeDtypeStruct((nseq, CONV_W - 1, D_FF), F32),
        ],
        scratch_shapes=[pltpu.VMEM((nseq, D_MODEL), BF16)],
        compiler_params=pltpu.CompilerParams(
            dimension_semantics=("arbitrary",), vmem_limit_bytes=VMEM_LIMIT),
        name="ffn_step",
    )(x1, g, wup_bf, wup_bf, conv_w, conv_w, conv_b, conv_b, wdown_bf, buf, buf)


def _rope_tables(pos):
    half = HEAD_DIM // 2
    inv = ROPE_THETA ** (-jnp.arange(half, dtype=F32) * 2.0 / HEAD_DIM)
    ang = pos.astype(F32)[:, None] * inv[None, :]
    c = jnp.cos(ang)
    s = jnp.sin(ang)
    return jnp.concatenate([c, c, c, c], axis=1), jnp.concatenate([-s, s, -s, s], axis=1)


def _block_diag(w):
    nd, ng, r, c = w.shape
    tiled = jnp.tile(w.reshape(nd, ng * r, c), (1, 1, ng))
    on_diag = (jnp.arange(ng * r)[:, None] // r) == (jnp.arange(ng * c)[None, :] // c)
    return jnp.where(on_diag[None], tiled, jnp.zeros_like(tiled))


def kernel(x_prompt, x_sample, cache_k, cache_v, state_ssm_re, state_ssm_im, state_conv, meta_tokens,
           norm_mix_g, w_in, q_norm_g, k_norm_g, attn_sinks, lam_re, lam_im, log_dt, ssm_b_re, ssm_b_im,
           ssm_c_re, ssm_c_im, ssm_d, w_glu, b_glu, ssm_out_g, attn_out_g, w_out, norm_ffn_g, w_up,
           conv_w, conv_b, w_down):
    depth = w_in.shape[0]
    assert depth == 1
    nb, seq, _ = x_prompt.shape
    nseq = x_sample.shape[0]
    n_meta = meta_tokens.shape[0]
    assert x_sample.shape[1] == 1 and n_meta == 16 and nb == SUBLANES
    assert seq % 1024 == 0 and nseq % SUBLANES == 0
    l = 0
    gpd = SSM_GROUPS_PER_DOT
    n_dots = N_SSM_GROUPS // gpd

    w_in_bf = w_in[l].astype(BF16)
    w_glu_bf = w_glu[l].astype(BF16)
    w_out_bf = w_out[l].astype(BF16)
    w_up_bf = w_up[l].astype(BF16)
    w_down_bf = w_down[l].astype(BF16)
    g_mix = norm_mix_g[l][None, :]
    g_ffn = norm_ffn_g[l][None, :]
    qg = jnp.tile(q_norm_g[l], 256 // HEAD_DIM)[None, :]
    kg = jnp.tile(k_norm_g[l], 256 // HEAD_DIM)[None, :]
    lane_head = jnp.arange(256) // HEAD_DIM
    ones_bd = (lane_head[:, None] == lane_head[None, :]).astype(BF16)
    sinks = attn_sinks[l]
    lamre = lam_re[l].reshape(1, D_STATE)
    lamim = lam_im[l].reshape(1, D_STATE)
    logdt = jnp.repeat(log_dt[l], SSM_STATE)[None, :]
    b_re_t = jnp.swapaxes(ssm_b_re[l], 1, 2).reshape(n_dots, gpd, SSM_GROUP, SSM_STATE)
    b_im_t = jnp.swapaxes(ssm_b_im[l], 1, 2).reshape(n_dots, gpd, SSM_GROUP, SSM_STATE)
    bblk = jnp.concatenate([_block_diag(b_re_t), _block_diag(b_im_t)], axis=2)
    cre = _block_diag(jnp.swapaxes(ssm_c_re[l], 1, 2).reshape(n_dots, gpd, SSM_STATE, SSM_GROUP)).astype(BF16)
    cim = _block_diag(jnp.swapaxes(ssm_c_im[l], 1, 2).reshape(n_dots, gpd, SSM_STATE, SSM_GROUP)).astype(BF16)
    dskip = ssm_d[l].reshape(1, D_SSM)
    bglu = b_glu[l][None, :]
    g_ssm = ssm_out_g[l][None, :]
    g_att = attn_out_g[l][None, :]
    cw = conv_w[l]
    cb = conv_b[l][None, :]

    ssm_w = (lamre, lamim, logdt, bblk, cre, cim, dskip, w_glu_bf, bglu, g_ssm)

    pad = BLOCK - n_meta
    xm = jnp.concatenate([jnp.zeros((pad, D_MODEL), F32), meta_tokens.astype(F32)], axis=0)[None]
    cos_m, sin_m = _rope_tables(jnp.maximum(jnp.arange(BLOCK) - pad, 0))
    u_m, q_m, k_m, v_m = _in_proj(xm, g_mix, w_in_bf, cos_m, sin_m, qg, kg, ones_bd, tq=BLOCK)
    u_m8 = jnp.broadcast_to(u_m, (BLOCK, SUBLANES, D_SSM)).reshape(BLOCK * SUBLANES, D_SSM)
    zero_state = jnp.zeros((SUBLANES, D_STATE), F32)
    ys_m8, hre_m, him_m = _ssm(u_m8, zero_state, zero_state, *ssm_w, nbatch=SUBLANES, tt=SCAN_STEPS)
    zero_kv = jnp.zeros((1, BLOCK, D_KV), F32)
    ya_m = _attn_seq(sinks, q_m, k_m, v_m, zero_kv, zero_kv, g_att, n_meta=n_meta, first_block=0)
    x1_m = _out_proj(xm, ys_m8[0:1], ya_m, w_out_bf, tm=BLOCK)
    zero_buf = jnp.zeros((1, SUBLANES, D_FF), F32)
    _, bufv_m, bufg_m = _ffn_seq(x1_m[0], g_ffn, w_up_bf, cw, cb, w_down_bf, zero_buf, zero_buf, nb=1, tm=BLOCK)

    cos_p, sin_p = _rope_tables(n_meta + jnp.arange(seq))
    u_p, q_p, k_p, v_p = _in_proj(x_prompt, g_mix, w_in_bf, cos_p, sin_p, qg, kg, ones_bd, tq=128)
    ys_p, hre_p, him_p = _ssm(u_p.reshape(seq * nb, D_SSM), hre_m, him_m, *ssm_w, nbatch=nb, tt=SCAN_STEPS)
    k0 = jnp.broadcast_to(k_m, (nb, BLOCK, D_KV))
    v0 = jnp.broadcast_to(v_m, (nb, BLOCK, D_KV))
    ya_p = _attn_seq(sinks, q_p, k_p, v_p, k0, v0, g_att, n_meta=n_meta, first_block=1)
    x1_p = _out_proj(x_prompt, ys_p, ya_p, w_out_bf, tm=512)
    bufv0 = jnp.broadcast_to(bufv_m, (nb, SUBLANES, D_FF))
    bufg0 = jnp.broadcast_to(bufg_m, (nb, SUBLANES, D_FF))
    y_p, bufv_p, bufg_p = _ffn_seq(x1_p.reshape(nb * seq, D_MODEL), g_ffn, w_up_bf, cw, cb, w_down_bf,
                                   bufv0, bufg0, nb=nb, tm=1024)

    y_prompt = y_p.reshape(nb, seq, D_MODEL)
    prompt_k = k_p[:, seq - WINDOW:].reshape(nb, WINDOW, N_KV_HEADS, HEAD_DIM)[None]
    prompt_v = v_p[:, seq - WINDOW:].reshape(nb, WINDOW, N_KV_HEADS, HEAD_DIM)[None]
    prompt_re = hre_p.reshape(nb, N_SSM_GROUPS, SSM_STATE)[None]
    prompt_im = him_p.reshape(nb, N_SSM_GROUPS, SSM_STATE)[None]
    prompt_conv = jnp.concatenate([bufv_p[:, SUBLANES - 2:], bufg_p[:, SUBLANES - 2:]], axis=-1)[None]

    xs = x_sample.reshape(1, nseq, D_MODEL)
    cos_s, sin_s = _rope_tables(jnp.full((nseq,), PAST_LEN, jnp.int32))
    u_s, q_s, k_s, v_s = _in_proj(xs, g_mix, w_in_bf, cos_s, sin_s, qg, kg, ones_bd, tq=nseq)
    ys_s, hre_s, him_s = _ssm(u_s.reshape(nseq, D_SSM), state_ssm_re[l].reshape(nseq, D_STATE),
                              state_ssm_im[l].reshape(nseq, D_STATE), *ssm_w, nbatch=nseq, tt=1)
    ya_s, nk_s, nv_s = _attn_step(
        q_s.reshape(nseq, N_HEADS, HEAD_DIM), k_s.reshape(nseq, N_KV_HEADS, HEAD_DIM),
        v_s.reshape(nseq, N_KV_HEADS, HEAD_DIM),
        cache_k[l].reshape(nseq, WINDOW * N_KV_HEADS, HEAD_DIM),
        cache_v[l].reshape(nseq, WINDOW * N_KV_HEADS, HEAD_DIM),
        sinks[:, None], attn_out_g[l].reshape(N_HEADS, HEAD_DIM), bb=8)
    x1_s = _out_proj(xs, ys_s, ya_s.reshape(1, nseq, D_ATTN), w_out_bf, tm=nseq)
    y_s, nbv_s, nbg_s = _ffn_step(x1_s[0], g_ffn, w_up_bf, cw, cb, w_down_bf, state_conv[l])

    y_sample = y_s.reshape(nseq, 1, D_MODEL)
    sample_k = nk_s.reshape(nseq, WINDOW, N_KV_HEADS, HEAD_DIM)[None]
    sample_v = nv_s.reshape(nseq, WINDOW, N_KV_HEADS, HEAD_DIM)[None]
    sample_re = hre_s.reshape(nseq, N_SSM_GROUPS, SSM_STATE)[None]
    sample_im = him_s.reshape(nseq, N_SSM_GROUPS, SSM_STATE)[None]
    sample_conv = jnp.concatenate([nbv_s, nbg_s], axis=-1)[None]

    return (y_prompt, y_sample, prompt_k, prompt_v, prompt_re, prompt_im, prompt_conv,
            sample_k, sample_v, sample_re, sample_im, sample_conv)
```

```python
import functools

import jax
import jax.numpy as jnp
from jax import lax
from jax.experimental import pallas as pl
from jax.experimental.pallas import tpu as pltpu

F32 = jnp.float32
BF16 = jnp.bfloat16

D_MODEL = 2048
D_SSM = 1024
SSM_GROUP = 16
N_SSM_GROUPS = 64
SSM_STATE = 64
D_STATE = N_SSM_GROUPS * SSM_STATE
HEAD_DIM = 64
N_HEADS = 16
N_KV_HEADS = 4
Q_PER_KV = 4
D_ATTN = 1024
D_KV = 256
D_IN = D_SSM + D_ATTN + 2 * D_KV
WINDOW = 128
BLOCK = 128
ROPE_THETA = 10000.0
D_FF = 5632
CONV_W = 3
EPS = 1e-6
NEG = -1e30
PAST_LEN = 16384

SUBLANES = 8
LANES = 128
VMEM_LIMIT = 60 * 1024 * 1024
FF_TILE = 512
FFN_SUB_ROWS = 256
IN_SUB_ROWS = 256
ATTN_Q_BLOCKS = 2
SCAN_LANES = 512
SCAN_STEPS = 32
SSM_GROUPS_PER_DOT = 16


def _const_spec(shape):
    nd = len(shape)
    return pl.BlockSpec(shape, lambda *_: (0,) * nd, pipeline_mode=pl.Buffered(1))


def _rms(x, g):
    return x * lax.rsqrt(jnp.mean(x * x, axis=-1, keepdims=True) + EPS) * g


def _mm(a, b):
    return jnp.dot(a, b, preferred_element_type=F32)


def _in_proj_kernel(x_ref, g_ref, w_ref, cos_ref, sin_ref, qg_ref, kg_ref, ones_ref,
                    u_ref, q_ref, k_ref, v_ref, *, nb, tq, n_split):
    rs = nb * tq // n_split
    cos1 = jnp.concatenate([cos_ref[...], cos_ref[...]], axis=1)
    sin1 = jnp.concatenate([sin_ref[...], sin_ref[...]], axis=1)
    ones = ones_ref[...]
    n_q = D_ATTN // 256
    for s in range(n_split):
        segs = []
        for b in range(nb):
            lo = max(s * rs, b * tq)
            hi = min((s + 1) * rs, (b + 1) * tq)
            if lo < hi:
                segs.append((b, lo - b * tq, hi - lo, lo - s * rs))
        x = jnp.concatenate([x_ref[b, t0:t0 + n, :] for b, t0, n, _ in segs], axis=0)
        cos = jnp.concatenate([cos1[t0:t0 + n, :] for _, t0, n, _ in segs], axis=0)
        sin = jnp.concatenate([sin1[t0:t0 + n, :] for _, t0, n, _ in segs], axis=0)
        lane = lax.broadcasted_iota(jnp.int32, cos.shape, 1)
        first_half = (lane % HEAD_DIM) < (HEAD_DIM // 2)
        h = _rms(x, g_ref[...]).astype(BF16)
        z = _mm(h, w_ref[...])
        for b, t0, n, r0 in segs:
            u_ref[t0:t0 + n, b, :] = z[r0:r0 + n, :D_SSM]
            v_ref[b, t0:t0 + n, :] = z[r0:r0 + n, D_SSM + D_ATTN + D_KV:]
        for c in range(n_q + D_KV // 256):
            zc = z[:, D_SSM + 256 * c: D_SSM + 256 * (c + 1)]
            sq = zc * zc
            hi = sq.astype(BF16)
            lo = (sq - hi.astype(F32)).astype(BF16)
            ssum = _mm(hi, ones) + _mm(lo, ones)
            gain = qg_ref[...] if c < n_q else kg_ref[...]
            nrm = zc * lax.rsqrt(ssum * (1.0 / HEAD_DIM) + EPS) * gain
            partner = jnp.where(first_half, pltpu.roll(nrm, 256 - HEAD_DIM // 2, 1),
                                pltpu.roll(nrm, HEAD_DIM // 2, 1))
            r = nrm * cos + partner * sin
            for b, t0, n, r0 in segs:
                if c < n_q:
                    q_ref[b, t0:t0 + n, 256 * c: 256 * (c + 1)] = (
                        r[r0:r0 + n, :] * (HEAD_DIM ** -0.5)).astype(BF16)
                else:
                    k_ref[b, t0:t0 + n, :] = r[r0:r0 + n, :]


def _in_proj(x3, g, w_bf, cos, sin, qg, kg, ones_bd, *, tq):
    nb, t_len, _ = x3.shape
    blk = lambda t: (0, t, 0)
    kern = functools.partial(_in_proj_kernel, nb=nb, tq=tq, n_split=max(1, nb * tq // IN_SUB_ROWS))
    return pl.pallas_call(
        kern,
        grid=(t_len // tq,),
        in_specs=[
            pl.BlockSpec((nb, tq, D_MODEL), blk),
            _const_spec((1, D_MODEL)),
            _const_spec((D_MODEL, D_IN)),
            pl.BlockSpec((tq, LANES), lambda t: (t, 0)),
            pl.BlockSpec((tq, LANES), lambda t: (t, 0)),
            _const_spec((1, 256)),
            _const_spec((1, 256)),
            _const_spec((256, 256)),
        ],
        out_specs=[
            pl.BlockSpec((tq, nb, D_SSM), lambda t: (t, 0, 0)),
            pl.BlockSpec((nb, tq, D_ATTN), blk),
            pl.BlockSpec((nb, tq, D_KV), blk),
            pl.BlockSpec((nb, tq, D_KV), blk),
        ],
        out_shape=[
            jax.ShapeDtypeStruct((t_len, nb, D_SSM), F32),
            jax.ShapeDtypeStruct((nb, t_len, D_ATTN), BF16),
            jax.ShapeDtypeStruct((nb, t_len, D_KV), F32),
            jax.ShapeDtypeStruct((nb, t_len, D_KV), F32),
        ],
        compiler_params=pltpu.CompilerParams(
            dimension_semantics=("arbitrary",), vmem_limit_bytes=VMEM_LIMIT),
        name="in_proj",
    )(x3, g, w_bf, cos, sin, qg, kg, ones_bd)


def _ssm_kernel(u_ref, h0re_ref, h0im_ref, lamre_ref, lamim_ref, logdt_ref, bblk_ref,
                cre_ref, cim_ref, dskip_ref, wglu_ref, bglu_ref, g_ref, perm_ref,
                y_ref, sre_ref, sim_ref,
                are_s, aim_s, bbar_s, xre_s, xim_s, stre_s, stim_s, *, tt, nbg):
    step = pl.program_id(0)
    lanes_per_dot = SSM_GROUPS_PER_DOT * SSM_STATE
    chans_per_dot = SSM_GROUPS_PER_DOT * SSM_GROUP
    n_dots = N_SSM_GROUPS // SSM_GROUPS_PER_DOT
    state_lanes = [slice(lanes_per_dot * c, lanes_per_dot * (c + 1)) for c in range(n_dots)]
    chans = [slice(chans_per_dot * c, chans_per_dot * (c + 1)) for c in range(n_dots)]

    @pl.when(step == 0)
    def _():
        lr = lamre_ref[...]
        li = lamim_ref[...]
        dt = jnp.exp(logdt_ref[...])
        mag = jnp.exp(lr * dt)
        a_re = mag * jnp.cos(li * dt)
        a_im = mag * jnp.sin(li * dt)
        den = lr * lr + li * li
        am1 = a_re - 1.0
        cf_re = (am1 * lr + a_im * li) / den
        cf_im = (a_im * lr - am1 * li) / den
        shape = (SUBLANES, D_STATE)
        are_s[...] = jnp.broadcast_to(a_re, shape)
        aim_s[...] = jnp.broadcast_to(a_im, shape)
        for c in range(n_dots):
            b_re = bblk_ref[c, :, :lanes_per_dot]
            b_im = bblk_ref[c, :, lanes_per_dot:]
            c_re = cf_re[:, state_lanes[c]]
            c_im = cf_im[:, state_lanes[c]]
            bbar_s[c, :, :lanes_per_dot] = (c_re * b_re - c_im * b_im).astype(BF16)
            bbar_s[c, :, lanes_per_dot:] = (c_re * b_im + c_im * b_re).astype(BF16)
        stre_s[...] = h0re_ref[...]
        stim_s[...] = h0im_ref[...]

    u = u_ref[...]
    ub = u.astype(BF16)

    def project(c):
        bu = _mm(ub[:, chans[c]], bbar_s[c])
        xre_s[:, state_lanes[c]] = bu[:, :lanes_per_dot]
        xim_s[:, state_lanes[c]] = bu[:, lanes_per_dot:]

    def scan(c):
        blocks = [slice(lanes_per_dot * c + SCAN_LANES * i, lanes_per_dot * c + SCAN_LANES * (i + 1))
                  for i in range(lanes_per_dot // SCAN_LANES)]
        for bg in range(nbg):
            rs = slice(SUBLANES * bg, SUBLANES * (bg + 1))
            a = [(are_s[:, ls], aim_s[:, ls]) for ls in blocks]
            st = [(stre_s[rs, ls], stim_s[rs, ls]) for ls in blocks]
            for t in range(tt):
                rt = slice((t * nbg + bg) * SUBLANES, (t * nbg + bg + 1) * SUBLANES)
                for i, ls in enumerate(blocks):
                    (a_re, a_im), (s_re, s_im) = a[i], st[i]
                    n_re = a_re * s_re - a_im * s_im + xre_s[rt, ls]
                    n_im = a_re * s_im + a_im * s_re + xim_s[rt, ls]
                    xre_s[rt, ls] = n_re
                    xim_s[rt, ls] = n_im
                    st[i] = (n_re, n_im)
            for i, ls in enumerate(blocks):
                stre_s[rs, ls] = st[i][0]
                stim_s[rs, ls] = st[i][1]

    def readout(c):
        yc = (_mm(xre_s[:, state_lanes[c]].astype(BF16), cre_ref[c])
              - _mm(xim_s[:, state_lanes[c]].astype(BF16), cim_ref[c]))
        return jax.nn.gelu(yc + dskip_ref[:, chans[c]] * u[:, chans[c]])

    ys = [None] * n_dots
    project(0)
    for c in range(n_dots):
        if c + 1 < n_dots:
            project(c + 1)
        scan(c)
        if c > 0:
            ys[c - 1] = readout(c - 1)
    ys[n_dots - 1] = readout(n_dots - 1)
    y = jnp.concatenate(ys, axis=1)
    y = y * jax.nn.sigmoid(_mm(y.astype(BF16), wglu_ref[...]) + bglu_ref[...])
    yn = _rms(y, g_ref[...]).astype(BF16)
    if tt == 1:
        y_ref[0] = yn
    else:
        y_ref[...] = _mm(perm_ref[...], yn).astype(BF16).reshape(y_ref.shape)

    @pl.when(step == pl.num_programs(0) - 1)
    def _():
        sre_ref[...] = stre_s[...]
        sim_ref[...] = stim_s[...]


def _ssm(u_tm, h0re, h0im, lamre, lamim, logdt, bblk, cre, cim, dskip, wglu, bglu, g, *, nbatch, tt):
    n = u_tm.shape[0]
    t_len = n // nbatch
    nbg = nbatch // SUBLANES
    rows = tt * nbatch
    n_grp = nbatch if tt > 1 else 1
    r = jnp.arange(rows)
    src = (r % tt) * nbatch + r // tt
    perm = (src[:, None] == r[None, :]).astype(BF16)
    kern = functools.partial(_ssm_kernel, tt=tt, nbg=nbg)
    return pl.pallas_call(
        kern,
        grid=(t_len // tt,),
        in_specs=[
            pl.BlockSpec((rows, D_SSM), lambda s: (s, 0)),
            _const_spec((nbatch, D_STATE)),
            _const_spec((nbatch, D_STATE)),
            _const_spec((1, D_STATE)),
            _const_spec((1, D_STATE)),
            _const_spec((1, D_STATE)),
            _const_spec(bblk.shape),
            _const_spec(cre.shape),
            _const_spec(cim.shape),
            _const_spec((1, D_SSM)),
            _const_spec((D_SSM, D_SSM)),
            _const_spec((1, D_SSM)),
            _const_spec((1, D_SSM)),
            _const_spec((rows, rows)),
        ],
        out_specs=[
            pl.BlockSpec((n_grp, rows // n_grp, D_SSM), lambda s: (0, s, 0)),
            pl.BlockSpec((nbatch, D_STATE), lambda s: (0, 0)),
            pl.BlockSpec((nbatch, D_STATE), lambda s: (0, 0)),
        ],
        out_shape=[
            jax.ShapeDtypeStruct((n_grp, n // n_grp, D_SSM), BF16),
            jax.ShapeDtypeStruct((nbatch, D_STATE), F32),
            jax.ShapeDtypeStruct((nbatch, D_STATE), F32),
        ],
        scratch_shapes=[pltpu.VMEM((SUBLANES, D_STATE), F32)] * 2
        + [pltpu.VMEM(bblk.shape, BF16)]
        + [pltpu.VMEM((rows, D_STATE), F32)] * 2
        + [pltpu.VMEM((nbatch, D_STATE), F32)] * 2,
        compiler_params=pltpu.CompilerParams(
            dimension_semantics=("arbitrary",), vmem_limit_bytes=VMEM_LIMIT),
        name="ssm",
    )(u_tm, h0re, h0im, lamre, lamim, logdt, bblk, cre, cim, dskip, wglu, bglu, g, perm)


def _attn_seq_kernel(sink_ref, q_ref, kc_ref, kp_ref, k0_ref, vc_ref, vp_ref, v0_ref, g_ref, o_ref,
                     *, first_block, n_meta, qb):
    i = pl.program_id(1)
    is_first = i == 0
    r = lax.broadcasted_iota(jnp.int32, (BLOCK, 4 * BLOCK), 0)
    c = lax.broadcasted_iota(jnp.int32, (BLOCK, 4 * BLOCK), 1) % (2 * BLOCK)
    in_window = (c > r) & (c <= r + WINDOW)
    lo_half = lax.broadcasted_iota(jnp.int32, (2 * BLOCK, LANES), 1) < HEAD_DIM

    def pair_blocks(x2, kv):
        xc = x2[:, LANES * (kv // 2): LANES * (kv // 2 + 1)]
        xr = pltpu.roll(xc, HEAD_DIM, 1)
        zero = jnp.zeros_like(xc)
        if kv % 2 == 0:
            low, high = jnp.where(lo_half, xc, zero), jnp.where(lo_half, zero, xr)
        else:
            low, high = jnp.where(lo_half, xr, zero), jnp.where(lo_half, zero, xc)
        return jnp.concatenate([low, high], axis=0).astype(BF16)

    n_pairs = N_HEADS // 2
    pairs_per_kv = Q_PER_KV // 2
    row_hi = lax.broadcasted_iota(jnp.int32, (4 * BLOCK, LANES), 0) >= 2 * BLOCK
    lane_hi = lax.broadcasted_iota(jnp.int32, (4 * BLOCK, LANES), 1) >= HEAD_DIM
    ones_cols = (row_hi == lane_hi).astype(BF16)
    kbd, vbd, mask, rows = [], [], [], []
    for sb in range(qb):
        rows.append(slice(BLOCK * sb, BLOCK * (sb + 1)))
        if sb == 0:
            k_prev = jnp.where(is_first, k0_ref[0], kp_ref[0])
            v_prev = jnp.where(is_first, v0_ref[0], vp_ref[0])
        else:
            k_prev, v_prev = kc_ref[0, rows[sb - 1], :], vc_ref[0, rows[sb - 1], :]
        k2 = jnp.concatenate([k_prev, kc_ref[0, rows[sb], :]], axis=0)
        v2 = jnp.concatenate([v_prev, vc_ref[0, rows[sb], :]], axis=0)
        gb = i * qb + sb + first_block
        col_min = jnp.where(gb == 0, 2 * BLOCK - n_meta, jnp.where(gb == 1, BLOCK - n_meta, 0))
        mask.append(in_window & (c >= col_min))
        kbd.append([pair_blocks(k2, kv) for kv in range(N_KV_HEADS)])
        vbd.append([jnp.concatenate([pair_blocks(v2, kv), ones_cols], axis=1) for kv in range(N_KV_HEADS)])
    halves = (slice(0, 2 * BLOCK), slice(2 * BLOCK, 4 * BLOCK))
    probs = [(sb, pr) for sb in range(qb) for pr in range(n_pairs)]
    s = {(sb, pr): jnp.where(mask[sb], lax.dot_general(
        q_ref[0, rows[sb], LANES * pr: LANES * (pr + 1)], kbd[sb][pr // pairs_per_kv],
        (((1,), (1,)), ((), ())), preferred_element_type=F32), NEG) for sb, pr in probs}
    m = {(sb, pr): [jnp.maximum(jnp.max(s[sb, pr][:, halves[hh]], axis=-1, keepdims=True), sink_ref[2 * pr + hh])
                    for hh in range(2)] for sb, pr in probs}
    p = {(sb, pr): jnp.concatenate([jnp.exp(s[sb, pr][:, halves[hh]] - m[sb, pr][hh]) for hh in range(2)],
                                   axis=1).astype(BF16) for sb, pr in probs}
    out_hi = lax.broadcasted_iota(jnp.int32, (BLOCK, LANES), 1) >= HEAD_DIM
    outs = {}
    for sb, pr in probs:
        ov = _mm(p[sb, pr], vbd[sb][pr // pairs_per_kv])
        sink_w = jnp.where(out_hi, jnp.exp(sink_ref[2 * pr + 1] - m[sb, pr][1]),
                           jnp.exp(sink_ref[2 * pr] - m[sb, pr][0]))
        outs[sb, pr] = ov[:, :LANES] / (ov[:, LANES:] + sink_w)
    for sb in range(qb):
        o = jnp.concatenate([outs[sb, pr] for pr in range(n_pairs)], axis=1)
        o_ref[0, rows[sb], :] = _rms(o, g_ref[...]).astype(BF16)


def _attn_seq(sinks, q, k, v, k0, v0, g, *, n_meta, first_block):
    nb, t_len, _ = q.shape
    nblk = t_len // BLOCK
    qb = ATTN_Q_BLOCKS if nblk % ATTN_Q_BLOCKS == 0 else 1
    cur = lambda b, i: (b, i, 0)
    prev = lambda b, i: (b, jnp.maximum(i * qb - 1, 0), 0)
    init = lambda b, i: (b, 0, 0)
    kern = functools.partial(_attn_seq_kernel, first_block=first_block, n_meta=n_meta, qb=qb)
    return pl.pallas_call(
        kern,
        grid=(nb, nblk // qb),
        in_specs=[
            pl.BlockSpec(memory_space=pltpu.SMEM),
            pl.BlockSpec((1, qb * BLOCK, D_ATTN), cur),
            pl.BlockSpec((1, qb * BLOCK, D_KV), cur),
            pl.BlockSpec((1, BLOCK, D_KV), prev),
            pl.BlockSpec((1, BLOCK, D_KV), init),
            pl.BlockSpec((1, qb * BLOCK, D_KV), cur),
            pl.BlockSpec((1, BLOCK, D_KV), prev),
            pl.BlockSpec((1, BLOCK, D_KV), init),
            _const_spec((1, D_ATTN)),
        ],
        out_specs=pl.BlockSpec((1, qb * BLOCK, D_ATTN), cur),
        out_shape=jax.ShapeDtypeStruct((nb, t_len, D_ATTN), BF16),
        compiler_params=pltpu.CompilerParams(
            dimension_semantics=("arbitrary", "arbitrary"), vmem_limit_bytes=VMEM_LIMIT),
        name="attn_seq",
    )(sinks, q, k, k, k0, v, v, v0, g)


def _attn_step_kernel(q_ref, kn_ref, vn_ref, ck_ref, cv_ref, sink_ref, g_ref, o_ref, ok_ref, ov_ref):
    is_last = lax.broadcasted_iota(jnp.int32, (1, 1, 1, WINDOW), 3) == WINDOW - 1

    def shift_in(cache_ref, new_ref):
        return jnp.where(is_last, new_ref[...], pltpu.roll(cache_ref[...], WINDOW - 1, 3))

    k_t = shift_in(ck_ref, kn_ref)
    v_t = shift_in(cv_ref, vn_ref)
    ok_ref[...] = k_t
    ov_ref[...] = v_t
    q = q_ref[...]
    kv_of_head = lax.broadcasted_iota(jnp.int32, (1, N_HEADS, 1), 1) // Q_PER_KV
    s = None
    for kv in range(N_KV_HEADS):
        s_kv = jnp.einsum('bhd,bdw->bhw', q, k_t[:, kv].astype(BF16), preferred_element_type=F32)
        s = s_kv if s is None else jnp.where(kv_of_head == kv, s_kv, s)
    sink = sink_ref[...][None]
    m = jnp.maximum(jnp.max(s, axis=-1, keepdims=True), sink)
    p = jnp.exp(s - m)
    den = jnp.sum(p, axis=-1, keepdims=True) + jnp.exp(sink - m)
    pb = (p / den).astype(BF16)
    o = None
    for kv in range(N_KV_HEADS):
        o_kv = jnp.einsum('bhw,bdw->bhd', pb, v_t[:, kv].astype(BF16), preferred_element_type=F32)
        o = o_kv if o is None else jnp.where(kv_of_head == kv, o_kv, o)
    ms = jnp.sum(jnp.sum(o * o, axis=2, keepdims=True), axis=1, keepdims=True) * (1.0 / D_ATTN)
    o_ref[...] = (o * lax.rsqrt(ms + EPS) * g_ref[...][None]).astype(BF16)


def _attn_step(q3, kn, vn, ck, cv, sinks, g, *, bb):
    nseq = q3.shape[0]
    blk3 = lambda i: (i, 0, 0)
    blk4 = lambda i: (i, 0, 0, 0)
    cache_blk = (bb, N_KV_HEADS, HEAD_DIM, WINDOW)
    new_blk = (bb, N_KV_HEADS, HEAD_DIM, 1)
    return pl.pallas_call(
        _attn_step_kernel,
        grid=(nseq // bb,),
        in_specs=[
            pl.BlockSpec((bb, N_HEADS, HEAD_DIM), blk3),
            pl.BlockSpec(new_blk, blk4),
            pl.BlockSpec(new_blk, blk4),
            pl.BlockSpec(cache_blk, blk4),
            pl.BlockSpec(cache_blk, blk4),
            _const_spec((N_HEADS, 1)),
            _const_spec((N_HEADS, HEAD_DIM)),
        ],
        out_specs=[
            pl.BlockSpec((bb, N_HEADS, HEAD_DIM), blk3),
            pl.BlockSpec(cache_blk, blk4),
            pl.BlockSpec(cache_blk, blk4),
        ],
        out_shape=[
            jax.ShapeDtypeStruct((nseq, N_HEADS, HEAD_DIM), BF16),
            jax.ShapeDtypeStruct((nseq,) + cache_blk[1:], F32),
            jax.ShapeDtypeStruct((nseq,) + cache_blk[1:], F32),
        ],
        compiler_params=pltpu.CompilerParams(
            dimension_semantics=("arbitrary",), vmem_limit_bytes=VMEM_LIMIT),
        name="attn_step",
    )(q3, kn, vn, ck, cv, sinks, g)


def _out_proj_kernel(x_ref, ys_ref, ya_ref, w_ref, o_ref):
    o_ref[0] = (x_ref[0] + _mm(ys_ref[0], w_ref[0:D_SSM, :])
                + _mm(ya_ref[0], w_ref[D_SSM:D_SSM + D_ATTN, :]))


def _out_proj(x3, ys, ya, w_bf, *, tm):
    nb, t_len, _ = x3.shape
    blk = lambda b, t: (b, t, 0)
    return pl.pallas_call(
        _out_proj_kernel,
        grid=(nb, t_len // tm),
        in_specs=[
            pl.BlockSpec((1, tm, D_MODEL), blk),
            pl.BlockSpec((1, tm, D_SSM), blk),
            pl.BlockSpec((1, tm, D_ATTN), blk),
            _const_spec((D_SSM + D_ATTN, D_MODEL)),
        ],
        out_specs=pl.BlockSpec((1, tm, D_MODEL), blk),
        out_shape=jax.ShapeDtypeStruct((nb, t_len, D_MODEL), F32),
        compiler_params=pltpu.CompilerParams(
            dimension_semantics=("arbitrary", "arbitrary"), vmem_limit_bytes=VMEM_LIMIT),
        name="out_proj",
    )(x3, ys, ya, w_bf)


def _ffn_seq_kernel(x_ref, g_ref, wv_ref, wg_ref, cwv_ref, cwg_ref, cbv_ref, cbg_ref, wd_ref,
                    bufv_ref, bufg_ref, y_ref, nbv_ref, nbg_ref,
                    h_s, upv_s, upg_s, *, tm, tf):
    t = pl.program_id(1)
    j = pl.program_id(2)
    chan = pl.ds(pl.multiple_of(j * tf, tf), tf)

    @pl.when(j == 0)
    def _():
        x = x_ref[...]
        h_s[...] = _rms(x, g_ref[...]).astype(BF16)
        y_ref[...] = x

    @pl.when(t == 0)
    def _():
        upv_s[0:SUBLANES, :] = bufv_ref[0]
        upg_s[0:SUBLANES, :] = bufg_ref[0]

    @pl.when(t > 0)
    def _():
        upv_s[0:SUBLANES, :] = nbv_ref[0, :, chan]
        upg_s[0:SUBLANES, :] = nbg_ref[0, :, chan]

    rm = min(tm, FFN_SUB_ROWS)
    n_sub = tm // rm

    def up_proj(up_s, w_ref, r0):
        up_s[SUBLANES + r0:SUBLANES + r0 + rm, :] = _mm(h_s[r0:r0 + rm, :], w_ref[...])

    def conv(up_s, cw_ref, cb_ref, r0):
        return (cb_ref[...] + cw_ref[0:1, :] * up_s[SUBLANES - 2 + r0:SUBLANES - 2 + r0 + rm, :]
                + cw_ref[1:2, :] * up_s[SUBLANES - 1 + r0:SUBLANES - 1 + r0 + rm, :]
                + cw_ref[2:3, :] * up_s[SUBLANES + r0:SUBLANES + r0 + rm, :])

    def gate_down(r0):
        val = conv(upv_s, cwv_ref, cbv_ref, r0)
        gate = conv(upg_s, cwg_ref, cbg_ref, r0)
        act = (jax.nn.silu(gate) * val).astype(BF16)
        y_ref[r0:r0 + rm, :] += _mm(act, wd_ref[...])

    for r in range(n_sub):
        up_proj(upv_s, wv_ref, r * rm)
        up_proj(upg_s, wg_ref, r * rm)
        if r > 0:
            gate_down((r - 1) * rm)
    gate_down((n_sub - 1) * rm)
    nbv_ref[0, :, chan] = upv_s[tm:tm + SUBLANES, :]
    nbg_ref[0, :, chan] = upg_s[tm:tm + SUBLANES, :]


def _ffn_seq(x1, g, wup_bf, conv_w, conv_b, wdown_bf, bufv, bufg, *, nb, tm):
    n = x1.shape[0]
    nt = n // nb // tm
    tf = FF_TILE
    nj = D_FF // tf
    row = lambda b, t, j: (b * nt + t, 0)
    kern = functools.partial(_ffn_seq_kernel, tm=tm, tf=tf)
    return pl.pallas_call(
        kern,
        grid=(nb, nt, nj),
        in_specs=[
            pl.BlockSpec((tm, D_MODEL), row),
            _const_spec((1, D_MODEL)),
            pl.BlockSpec((D_MODEL, tf), lambda b, t, j: (0, j)),
            pl.BlockSpec((D_MODEL, tf), lambda b, t, j: (0, nj + j)),
            pl.BlockSpec((CONV_W, tf), lambda b, t, j: (0, j)),
            pl.BlockSpec((CONV_W, tf), lambda b, t, j: (0, nj + j)),
            pl.BlockSpec((1, tf), lambda b, t, j: (0, j)),
            pl.BlockSpec((1, tf), lambda b, t, j: (0, nj + j)),
            pl.BlockSpec((tf, D_MODEL), lambda b, t, j: (j, 0)),
            pl.BlockSpec((1, SUBLANES, tf), lambda b, t, j: (b, 0, j)),
            pl.BlockSpec((1, SUBLANES, tf), lambda b, t, j: (b, 0, j)),
        ],
        out_specs=[
            pl.BlockSpec((tm, D_MODEL), row),
            pl.BlockSpec((1, SUBLANES, D_FF), lambda b, t, j: (b, 0, 0)),
            pl.BlockSpec((1, SUBLANES, D_FF), lambda b, t, j: (b, 0, 0)),
        ],
        out_shape=[
            jax.ShapeDtypeStruct((n, D_MODEL), F32),
            jax.ShapeDtypeStruct((nb, SUBLANES, D_FF), F32),
            jax.ShapeDtypeStruct((nb, SUBLANES, D_FF), F32),
        ],
        scratch_shapes=[
            pltpu.VMEM((tm, D_MODEL), BF16),
            pltpu.VMEM((tm + SUBLANES, tf), F32),
            pltpu.VMEM((tm + SUBLANES, tf), F32),
        ],
        compiler_params=pltpu.CompilerParams(
            dimension_semantics=("arbitrary", "arbitrary", "arbitrary"), vmem_limit_bytes=VMEM_LIMIT),
        name="ffn_seq",
    )(x1, g, wup_bf, wup_bf, conv_w, conv_w, conv_b, conv_b, wdown_bf, bufv, bufg)


def _ffn_step_kernel(x_ref, g_ref, wv_ref, wg_ref, cwv_ref, cwg_ref, cbv_ref, cbg_ref, wd_ref,
                     bufv_ref, bufg_ref, xt_ref, y_ref, nbv_ref, nbg_ref, tv_ref, tg_ref, h_s, ht_s):
    j = pl.program_id(0)

    @pl.when(j == 0)
    def _():
        x = x_ref[...]
        h_s[...] = _rms(x, g_ref[...]).astype(BF16)
        ht_s[...] = _rms(xt_ref[...], g_ref[...]).astype(BF16)
        y_ref[...] = x

    n_tail = ht_s.shape[0]
    tv_ref[...] = _mm(ht_s[...], wv_ref[...])[n_tail - SUBLANES:, :]
    tg_ref[...] = _mm(ht_s[...], wg_ref[...])[n_tail - SUBLANES:, :]
    h = h_s[...]
    upv = _mm(h, wv_ref[...])
    upg = _mm(h, wg_ref[...])
    val = cbv_ref[...] + cwv_ref[0:1, :] * bufv_ref[:, 0, :] + cwv_ref[1:2, :] * bufv_ref[:, 1, :] + cwv_ref[2:3, :] * upv
    gate = cbg_ref[...] + cwg_ref[0:1, :] * bufg_ref[:, 0, :] + cwg_ref[1:2, :] * bufg_ref[:, 1, :] + cwg_ref[2:3, :] * upg
    act = (jax.nn.silu(gate) * val).astype(BF16)
    y_ref[...] += _mm(act, wd_ref[...])
    nbv_ref[:, 0, :] = bufv_ref[:, 1, :]
    nbg_ref[:, 0, :] = bufg_ref[:, 1, :]
    nbv_ref[:, 1, :] = upv
    nbg_ref[:, 1, :] = upg


def _ffn_step(x1, g, wup_bf, conv_w, conv_b, wdown_bf, buf, x_tail):
    nseq = x1.shape[0]
    n_tail = x_tail.shape[0]
    tf = FF_TILE
    nj = D_FF // tf
    return pl.pallas_call(
        _ffn_step_kernel,
        grid=(nj,),
        in_specs=[
            _const_spec((nseq, D_MODEL)),
            _const_spec((1, D_MODEL)),
            pl.BlockSpec((D_MODEL, tf), lambda j: (0, j)),
            pl.BlockSpec((D_MODEL, tf), lambda j: (0, nj + j)),
            pl.BlockSpec((CONV_W, tf), lambda j: (0, j)),
            pl.BlockSpec((CONV_W, tf), lambda j: (0, nj + j)),
            pl.BlockSpec((1, tf), lambda j: (0, j)),
            pl.BlockSpec((1, tf), lambda j: (0, nj + j)),
            pl.BlockSpec((tf, D_MODEL), lambda j: (j, 0)),
            pl.BlockSpec((nseq, CONV_W - 1, tf), lambda j: (0, 0, j)),
            pl.BlockSpec((nseq, CONV_W - 1, tf), lambda j: (0, 0, nj + j)),
            _const_spec((n_tail, D_MODEL)),
        ],
        out_specs=[
            pl.BlockSpec((nseq, D_MODEL), lambda j: (0, 0)),
            pl.BlockSpec((nseq, CONV_W - 1, tf), lambda j: (0, 0, j)),
            pl.BlockSpec((nseq, CONV_W - 1, tf), lambda j: (0, 0, j)),
            pl.BlockSpec((SUBLANES, tf), lambda j: (0, j)),
            pl.BlockSpec((SUBLANES, tf), lambda j: (0, j)),
        ],
        out_shape=[
            jax.ShapeDtypeStruct((nseq, D_MODEL), F32),
            jax.ShapeDtypeStruct((nseq, CONV_W - 1, D_FF), F32),
            jax.ShapeDtypeStruct((nseq, CONV_W - 1, D_FF), F32),
            jax.ShapeDtypeStruct((SUBLANES, D_FF), F32),
            jax.ShapeDtypeStruct((SUBLANES, D_FF), F32),
        ],
        scratch_shapes=[pltpu.VMEM((nseq, D_MODEL), BF16), pltpu.VMEM((n_tail, D_MODEL), BF16)],
        compiler_params=pltpu.CompilerParams(
            dimension_semantics=("arbitrary",), vmem_limit_bytes=VMEM_LIMIT),
        name="ffn_step",
    )(x1, g, wup_bf, wup_bf, conv_w, conv_w, conv_b, conv_b, wdown_bf, buf, buf, x_tail)


def _rope_tables(pos):
    half = HEAD_DIM // 2
    inv = ROPE_THETA ** (-jnp.arange(half, dtype=F32) * 2.0 / HEAD_DIM)
    ang = pos.astype(F32)[:, None] * inv[None, :]
    c = jnp.cos(ang)
    s = jnp.sin(ang)
    return jnp.concatenate([c, c, c, c], axis=1), jnp.concatenate([-s, s, -s, s], axis=1)


def _block_diag(w):
    nd, ng, r, c = w.shape
    tiled = jnp.tile(w.reshape(nd, ng * r, c), (1, 1, ng))
    on_diag = (jnp.arange(ng * r)[:, None] // r) == (jnp.arange(ng * c)[None, :] // c)
    return jnp.where(on_diag[None], tiled, jnp.zeros_like(tiled))


def kernel(x_prompt, x_sample, cache_k, cache_v, state_ssm_re, state_ssm_im, state_conv, meta_tokens,
           norm_mix_g, w_in, q_norm_g, k_norm_g, attn_sinks, lam_re, lam_im, log_dt, ssm_b_re, ssm_b_im,
           ssm_c_re, ssm_c_im, ssm_d, w_glu, b_glu, ssm_out_g, attn_out_g, w_out, norm_ffn_g, w_up,
           conv_w, conv_b, w_down):
    depth = w_in.shape[0]
    assert depth == 1
    nb, seq, _ = x_prompt.shape
    nseq = x_sample.shape[0]
    n_meta = meta_tokens.shape[0]
    assert x_sample.shape[1] == 1 and n_meta == 16 and nb == SUBLANES
    assert seq % 1024 == 0 and nseq % SUBLANES == 0
    l = 0
    gpd = SSM_GROUPS_PER_DOT
    n_dots = N_SSM_GROUPS // gpd

    w_in_bf = w_in[l].astype(BF16)
    w_glu_bf = w_glu[l].astype(BF16)
    w_out_bf = w_out[l].astype(BF16)
    w_up_bf = w_up[l].astype(BF16)
    w_down_bf = w_down[l].astype(BF16)
    g_mix = norm_mix_g[l][None, :]
    g_ffn = norm_ffn_g[l][None, :]
    qg = jnp.tile(q_norm_g[l], 256 // HEAD_DIM)[None, :]
    kg = jnp.tile(k_norm_g[l], 256 // HEAD_DIM)[None, :]
    lane_head = jnp.arange(256) // HEAD_DIM
    ones_bd = (lane_head[:, None] == lane_head[None, :]).astype(BF16)
    sinks = attn_sinks[l]
    lamre = lam_re[l].reshape(1, D_STATE)
    lamim = lam_im[l].reshape(1, D_STATE)
    logdt = jnp.repeat(log_dt[l], SSM_STATE)[None, :]
    b_re_t = jnp.swapaxes(ssm_b_re[l], 1, 2).reshape(n_dots, gpd, SSM_GROUP, SSM_STATE)
    b_im_t = jnp.swapaxes(ssm_b_im[l], 1, 2).reshape(n_dots, gpd, SSM_GROUP, SSM_STATE)
    bblk = jnp.concatenate([_block_diag(b_re_t), _block_diag(b_im_t)], axis=2)
    cre = _block_diag(jnp.swapaxes(ssm_c_re[l], 1, 2).reshape(n_dots, gpd, SSM_STATE, SSM_GROUP)).astype(BF16)
    cim = _block_diag(jnp.swapaxes(ssm_c_im[l], 1, 2).reshape(n_dots, gpd, SSM_STATE, SSM_GROUP)).astype(BF16)
    dskip = ssm_d[l].reshape(1, D_SSM)
    bglu = b_glu[l][None, :]
    g_ssm = ssm_out_g[l][None, :]
    g_att = attn_out_g[l][None, :]
    cw = conv_w[l]
    cb = conv_b[l][None, :]

    ssm_w = (lamre, lamim, logdt, bblk, cre, cim, dskip, w_glu_bf, bglu, g_ssm)

    pad = BLOCK - n_meta
    xm = jnp.concatenate([jnp.zeros((pad, D_MODEL), F32), meta_tokens.astype(F32)], axis=0)[None]
    cos_m, sin_m = _rope_tables(jnp.maximum(jnp.arange(BLOCK) - pad, 0))
    u_m, q_m, k_m, v_m = _in_proj(xm, g_mix, w_in_bf, cos_m, sin_m, qg, kg, ones_bd, tq=BLOCK)
    u_m8 = jnp.broadcast_to(u_m[pad:], (n_meta, SUBLANES, D_SSM)).reshape(n_meta * SUBLANES, D_SSM)
    zero_state = jnp.zeros((SUBLANES, D_STATE), F32)
    ys_m8, hre_m, him_m = _ssm(u_m8, zero_state, zero_state, *ssm_w, nbatch=SUBLANES, tt=n_meta)
    ys_m = jnp.concatenate([jnp.zeros((1, pad, D_SSM), BF16), ys_m8[0:1]], axis=1)
    zero_kv = jnp.zeros((1, BLOCK, D_KV), F32)
    ya_m = _attn_seq(sinks, q_m, k_m, v_m, zero_kv, zero_kv, g_att, n_meta=n_meta, first_block=0)
    x1_m = _out_proj(xm, ys_m, ya_m, w_out_bf, tm=BLOCK)

    xs = x_sample.reshape(1, nseq, D_MODEL)
    cos_s, sin_s = _rope_tables(jnp.full((nseq,), PAST_LEN, jnp.int32))
    u_s, q_s, k_s, v_s = _in_proj(xs, g_mix, w_in_bf, cos_s, sin_s, qg, kg, ones_bd, tq=nseq)
    ys_s, hre_s, him_s = _ssm(u_s.reshape(nseq, D_SSM), state_ssm_re[l].reshape(nseq, D_STATE),
                              state_ssm_im[l].reshape(nseq, D_STATE), *ssm_w, nbatch=nseq, tt=1)
    ya_s, nk_s, nv_s = _attn_step(
        q_s.reshape(nseq, N_HEADS, HEAD_DIM), k_s.reshape(nseq, N_KV_HEADS, HEAD_DIM, 1),
        v_s.reshape(nseq, N_KV_HEADS, HEAD_DIM, 1),
        jnp.transpose(cache_k[l], (0, 2, 3, 1)), jnp.transpose(cache_v[l], (0, 2, 3, 1)),
        sinks[:, None], attn_out_g[l].reshape(N_HEADS, HEAD_DIM), bb=8)
    x1_s = _out_proj(xs, ys_s, ya_s.reshape(1, nseq, D_ATTN), w_out_bf, tm=nseq)
    y_s, nbv_s, nbg_s, bufv_m, bufg_m = _ffn_step(x1_s[0], g_ffn, w_up_bf, cw, cb, w_down_bf, state_conv[l],
                                                  x1_m[0, BLOCK - 2 * SUBLANES:])

    y_sample = y_s.reshape(nseq, 1, D_MODEL)
    sample_k = jnp.transpose(nk_s, (0, 3, 1, 2))[None]
    sample_v = jnp.transpose(nv_s, (0, 3, 1, 2))[None]
    sample_re = hre_s.reshape(nseq, N_SSM_GROUPS, SSM_STATE)[None]
    sample_im = him_s.reshape(nseq, N_SSM_GROUPS, SSM_STATE)[None]
    sample_conv = jnp.concatenate([nbv_s, nbg_s], axis=-1)[None]

    cos_p, sin_p = _rope_tables(n_meta + jnp.arange(seq))
    u_p, q_p, k_p, v_p = _in_proj(x_prompt, g_mix, w_in_bf, cos_p, sin_p, qg, kg, ones_bd, tq=128)
    ys_p, hre_p, him_p = _ssm(u_p.reshape(seq * nb, D_SSM), hre_m, him_m, *ssm_w, nbatch=nb, tt=SCAN_STEPS)
    k0 = jnp.broadcast_to(k_m, (nb, BLOCK, D_KV))
    v0 = jnp.broadcast_to(v_m, (nb, BLOCK, D_KV))
    ya_p = _attn_seq(sinks, q_p, k_p, v_p, k0, v0, g_att, n_meta=n_meta, first_block=1)
    x1_p = _out_proj(x_prompt, ys_p, ya_p, w_out_bf, tm=512)
    bufv0 = jnp.broadcast_to(bufv_m, (nb, SUBLANES, D_FF))
    bufg0 = jnp.broadcast_to(bufg_m, (nb, SUBLANES, D_FF))
    y_p, bufv_p, bufg_p = _ffn_seq(x1_p.reshape(nb * seq, D_MODEL), g_ffn, w_up_bf, cw, cb, w_down_bf,
                                   bufv0, bufg0, nb=nb, tm=1024)

    y_prompt = y_p.reshape(nb, seq, D_MODEL)
    prompt_k = k_p[:, seq - WINDOW:].reshape(nb, WINDOW, N_KV_HEADS, HEAD_DIM)[None]
    prompt_v = v_p[:, seq - WINDOW:].reshape(nb, WINDOW, N_KV_HEADS, HEAD_DIM)[None]
    prompt_re = hre_p.reshape(nb, N_SSM_GROUPS, SSM_STATE)[None]
    prompt_im = him_p.reshape(nb, N_SSM_GROUPS, SSM_STATE)[None]
    prompt_conv = jnp.concatenate([bufv_p[:, SUBLANES - 2:], bufg_p[:, SUBLANES - 2:]], axis=-1)[None]

    return (y_prompt, y_sample, prompt_k, prompt_v, prompt_re, prompt_im, prompt_conv,
            sample_k, sample_v, sample_re, sample_im, sample_conv)
```

```python
import functools

import jax
import jax.numpy as jnp
from jax import lax
from jax.experimental import pallas as pl
from jax.experimental.pallas import tpu as pltpu

F32 = jnp.float32
BF16 = jnp.bfloat16

D_MODEL = 2048
D_SSM = 1024
SSM_GROUP = 16
N_SSM_GROUPS = 64
SSM_STATE = 64
D_STATE = N_SSM_GROUPS * SSM_STATE
HEAD_DIM = 64
N_HEADS = 16
N_KV_HEADS = 4
Q_PER_KV = 4
D_ATTN = 1024
D_KV = 256
D_IN = D_SSM + D_ATTN + 2 * D_KV
WINDOW = 128
BLOCK = 128
ROPE_THETA = 10000.0
D_FF = 5632
CONV_W = 3
EPS = 1e-6
NEG = -1e30
PAST_LEN = 16384

SUBLANES = 8
LANES = 128
VMEM_LIMIT = 60 * 1024 * 1024
FF_TILE = 512
FFN_SUB_ROWS = 512
IN_SUB_ROWS = 256
ATTN_Q_BLOCKS = 4
SCAN_LANES = 512
SCAN_STEPS = 32
SSM_GROUPS_PER_DOT = 16


def _const_spec(shape):
    nd = len(shape)
    return pl.BlockSpec(shape, lambda *_: (0,) * nd, pipeline_mode=pl.Buffered(1))


def _rms(x, g):
    return x * lax.rsqrt(jnp.mean(x * x, axis=-1, keepdims=True) + EPS) * g


def _mm(a, b):
    return jnp.dot(a, b, preferred_element_type=F32)


def _in_proj_kernel(x_ref, g_ref, w_ref, cos_ref, sin_ref, qg_ref, kg_ref, ones_ref,
                    u_ref, q_ref, k_ref, v_ref, *, nb, tq, n_split):
    rs = nb * tq // n_split
    cos1 = jnp.concatenate([cos_ref[...], cos_ref[...]], axis=1)
    sin1 = jnp.concatenate([sin_ref[...], sin_ref[...]], axis=1)
    ones = ones_ref[...]
    n_q = D_ATTN // 256
    for s in range(n_split):
        segs = []
        for b in range(nb):
            lo = max(s * rs, b * tq)
            hi = min((s + 1) * rs, (b + 1) * tq)
            if lo < hi:
                segs.append((b, lo - b * tq, hi - lo, lo - s * rs))
        x = jnp.concatenate([x_ref[b, t0:t0 + n, :] for b, t0, n, _ in segs], axis=0)
        cos = jnp.concatenate([cos1[t0:t0 + n, :] for _, t0, n, _ in segs], axis=0)
        sin = jnp.concatenate([sin1[t0:t0 + n, :] for _, t0, n, _ in segs], axis=0)
        lane = lax.broadcasted_iota(jnp.int32, cos.shape, 1)
        first_half = (lane % HEAD_DIM) < (HEAD_DIM // 2)
        h = _rms(x, g_ref[...]).astype(BF16)
        z = _mm(h, w_ref[...])
        for b, t0, n, r0 in segs:
            u_ref[t0:t0 + n, b, :] = z[r0:r0 + n, :D_SSM]
            v_ref[b, t0:t0 + n, :] = z[r0:r0 + n, D_SSM + D_ATTN + D_KV:]
        for c in range(n_q + D_KV // 256):
            zc = z[:, D_SSM + 256 * c: D_SSM + 256 * (c + 1)]
            sq = zc * zc
            hi = sq.astype(BF16)
            lo = (sq - hi.astype(F32)).astype(BF16)
            ssum = _mm(hi, ones) + _mm(lo, ones)
            gain = qg_ref[...] if c < n_q else kg_ref[...]
            nrm = zc * lax.rsqrt(ssum * (1.0 / HEAD_DIM) + EPS) * gain
            partner = jnp.where(first_half, pltpu.roll(nrm, 256 - HEAD_DIM // 2, 1),
                                pltpu.roll(nrm, HEAD_DIM // 2, 1))
            r = nrm * cos + partner * sin
            for b, t0, n, r0 in segs:
                if c < n_q:
                    q_ref[b, t0:t0 + n, 256 * c: 256 * (c + 1)] = (
                        r[r0:r0 + n, :] * (HEAD_DIM ** -0.5)).astype(BF16)
                else:
                    k_ref[b, t0:t0 + n, :] = r[r0:r0 + n, :]


def _in_proj(x3, g, w_bf, cos, sin, qg, kg, ones_bd, *, tq):
    nb, t_len, _ = x3.shape
    blk = lambda t: (0, t, 0)
    kern = functools.partial(_in_proj_kernel, nb=nb, tq=tq, n_split=max(1, nb * tq // IN_SUB_ROWS))
    return pl.pallas_call(
        kern,
        grid=(t_len // tq,),
        in_specs=[
            pl.BlockSpec((nb, tq, D_MODEL), blk),
            _const_spec((1, D_MODEL)),
            _const_spec((D_MODEL, D_IN)),
            pl.BlockSpec((tq, LANES), lambda t: (t, 0)),
            pl.BlockSpec((tq, LANES), lambda t: (t, 0)),
            _const_spec((1, 256)),
            _const_spec((1, 256)),
            _const_spec((256, 256)),
        ],
        out_specs=[
            pl.BlockSpec((tq, nb, D_SSM), lambda t: (t, 0, 0)),
            pl.BlockSpec((nb, tq, D_ATTN), blk),
            pl.BlockSpec((nb, tq, D_KV), blk),
            pl.BlockSpec((nb, tq, D_KV), blk),
        ],
        out_shape=[
            jax.ShapeDtypeStruct((t_len, nb, D_SSM), F32),
            jax.ShapeDtypeStruct((nb, t_len, D_ATTN), BF16),
            jax.ShapeDtypeStruct((nb, t_len, D_KV), F32),
            jax.ShapeDtypeStruct((nb, t_len, D_KV), F32),
        ],
        compiler_params=pltpu.CompilerParams(
            dimension_semantics=("arbitrary",), vmem_limit_bytes=VMEM_LIMIT),
        name="in_proj",
    )(x3, g, w_bf, cos, sin, qg, kg, ones_bd)


def _ssm_kernel(u_ref, h0re_ref, h0im_ref, lamre_ref, lamim_ref, logdt_ref, bblk_ref,
                cre_ref, cim_ref, dskip_ref, wglu_ref, bglu_ref, g_ref, perm_ref,
                y_ref, sre_ref, sim_ref,
                are_s, aim_s, bbar_s, xre_s, xim_s, stre_s, stim_s, *, tt, nbg):
    step = pl.program_id(0)
    lanes_per_dot = SSM_GROUPS_PER_DOT * SSM_STATE
    chans_per_dot = SSM_GROUPS_PER_DOT * SSM_GROUP
    n_dots = N_SSM_GROUPS // SSM_GROUPS_PER_DOT
    state_lanes = [slice(lanes_per_dot * c, lanes_per_dot * (c + 1)) for c in range(n_dots)]
    chans = [slice(chans_per_dot * c, chans_per_dot * (c + 1)) for c in range(n_dots)]

    @pl.when(step == 0)
    def _():
        lr = lamre_ref[...]
        li = lamim_ref[...]
        dt = jnp.exp(logdt_ref[...])
        mag = jnp.exp(lr * dt)
        a_re = mag * jnp.cos(li * dt)
        a_im = mag * jnp.sin(li * dt)
        den = lr * lr + li * li
        am1 = a_re - 1.0
        cf_re = (am1 * lr + a_im * li) / den
        cf_im = (a_im * lr - am1 * li) / den
        shape = (SUBLANES, D_STATE)
        are_s[...] = jnp.broadcast_to(a_re, shape)
        aim_s[...] = jnp.broadcast_to(a_im, shape)
        for c in range(n_dots):
            b_re = bblk_ref[c, :, :lanes_per_dot]
            b_im = bblk_ref[c, :, lanes_per_dot:]
            c_re = cf_re[:, state_lanes[c]]
            c_im = cf_im[:, state_lanes[c]]
            bbar_s[c, :, :lanes_per_dot] = (c_re * b_re - c_im * b_im).astype(BF16)
            bbar_s[c, :, lanes_per_dot:] = (c_re * b_im + c_im * b_re).astype(BF16)
        stre_s[...] = h0re_ref[...]
        stim_s[...] = h0im_ref[...]

    u = u_ref[...]
    ub = u.astype(BF16)

    def project(c):
        bu = _mm(ub[:, chans[c]], bbar_s[c])
        xre_s[:, state_lanes[c]] = bu[:, :lanes_per_dot]
        xim_s[:, state_lanes[c]] = bu[:, lanes_per_dot:]

    def scan(c):
        blocks = [slice(lanes_per_dot * c + SCAN_LANES * i, lanes_per_dot * c + SCAN_LANES * (i + 1))
                  for i in range(lanes_per_dot // SCAN_LANES)]
        for bg in range(nbg):
            rs = slice(SUBLANES * bg, SUBLANES * (bg + 1))
            a = [(are_s[:, ls], aim_s[:, ls]) for ls in blocks]
            st = [(stre_s[rs, ls], stim_s[rs, ls]) for ls in blocks]
            for t in range(tt):
                rt = slice((t * nbg + bg) * SUBLANES, (t * nbg + bg + 1) * SUBLANES)
                for i, ls in enumerate(blocks):
                    (a_re, a_im), (s_re, s_im) = a[i], st[i]
                    n_re = a_re * s_re - a_im * s_im + xre_s[rt, ls]
                    n_im = a_re * s_im + a_im * s_re + xim_s[rt, ls]
                    xre_s[rt, ls] = n_re
                    xim_s[rt, ls] = n_im
                    st[i] = (n_re, n_im)
            for i, ls in enumerate(blocks):
                stre_s[rs, ls] = st[i][0]
                stim_s[rs, ls] = st[i][1]

    def readout(c):
        yc = (_mm(xre_s[:, state_lanes[c]].astype(BF16), cre_ref[c])
              - _mm(xim_s[:, state_lanes[c]].astype(BF16), cim_ref[c]))
        return jax.nn.gelu(yc + dskip_ref[:, chans[c]] * u[:, chans[c]])

    ys = [None] * n_dots
    project(0)
    for c in range(n_dots):
        if c + 1 < n_dots:
            project(c + 1)
        scan(c)
        if c > 0:
            ys[c - 1] = readout(c - 1)
    ys[n_dots - 1] = readout(n_dots - 1)
    y = jnp.concatenate(ys, axis=1)
    y = y * jax.nn.sigmoid(_mm(y.astype(BF16), wglu_ref[...]) + bglu_ref[...])
    yn = _rms(y, g_ref[...]).astype(BF16)
    if tt == 1:
        y_ref[0] = yn
    else:
        y_ref[...] = _mm(perm_ref[...], yn).astype(BF16).reshape(y_ref.shape)

    @pl.when(step == pl.num_programs(0) - 1)
    def _():
        sre_ref[...] = stre_s[...]
        sim_ref[...] = stim_s[...]


def _ssm(u_tm, h0re, h0im, lamre, lamim, logdt, bblk, cre, cim, dskip, wglu, bglu, g, *, nbatch, tt):
    n = u_tm.shape[0]
    t_len = n // nbatch
    nbg = nbatch // SUBLANES
    rows = tt * nbatch
    n_grp = nbatch if tt > 1 else 1
    r = jnp.arange(rows)
    src = (r % tt) * nbatch + r // tt
    perm = (src[:, None] == r[None, :]).astype(BF16)
    kern = functools.partial(_ssm_kernel, tt=tt, nbg=nbg)
    return pl.pallas_call(
        kern,
        grid=(t_len // tt,),
        in_specs=[
            pl.BlockSpec((rows, D_SSM), lambda s: (s, 0)),
            _const_spec((nbatch, D_STATE)),
            _const_spec((nbatch, D_STATE)),
            _const_spec((1, D_STATE)),
            _const_spec((1, D_STATE)),
            _const_spec((1, D_STATE)),
            _const_spec(bblk.shape),
            _const_spec(cre.shape),
            _const_spec(cim.shape),
            _const_spec((1, D_SSM)),
            _const_spec((D_SSM, D_SSM)),
            _const_spec((1, D_SSM)),
            _const_spec((1, D_SSM)),
            _const_spec((rows, rows)),
        ],
        out_specs=[
            pl.BlockSpec((n_grp, rows // n_grp, D_SSM), lambda s: (0, s, 0)),
            pl.BlockSpec((nbatch, D_STATE), lambda s: (0, 0)),
            pl.BlockSpec((nbatch, D_STATE), lambda s: (0, 0)),
        ],
        out_shape=[
            jax.ShapeDtypeStruct((n_grp, n // n_grp, D_SSM), BF16),
            jax.ShapeDtypeStruct((nbatch, D_STATE), F32),
            jax.ShapeDtypeStruct((nbatch, D_STATE), F32),
        ],
        scratch_shapes=[pltpu.VMEM((SUBLANES, D_STATE), F32)] * 2
        + [pltpu.VMEM(bblk.shape, BF16)]
        + [pltpu.VMEM((rows, D_STATE), F32)] * 2
        + [pltpu.VMEM((nbatch, D_STATE), F32)] * 2,
        compiler_params=pltpu.CompilerParams(
            dimension_semantics=("arbitrary",), vmem_limit_bytes=VMEM_LIMIT),
        name="ssm",
    )(u_tm, h0re, h0im, lamre, lamim, logdt, bblk, cre, cim, dskip, wglu, bglu, g, perm)


def _attn_seq_kernel(sink_ref, q_ref, kc_ref, kp_ref, k0_ref, vc_ref, vp_ref, v0_ref, g_ref, o_ref,
                     *, first_block, n_meta, qb):
    i = pl.program_id(1)
    is_first = i == 0
    r = lax.broadcasted_iota(jnp.int32, (BLOCK, 4 * BLOCK), 0)
    c = lax.broadcasted_iota(jnp.int32, (BLOCK, 4 * BLOCK), 1) % (2 * BLOCK)
    in_window = (c > r) & (c <= r + WINDOW)
    lo_half = lax.broadcasted_iota(jnp.int32, (2 * BLOCK, LANES), 1) < HEAD_DIM

    def pair_blocks(x2, kv):
        xc = x2[:, LANES * (kv // 2): LANES * (kv // 2 + 1)]
        xr = pltpu.roll(xc, HEAD_DIM, 1)
        zero = jnp.zeros_like(xc)
        if kv % 2 == 0:
            low, high = jnp.where(lo_half, xc, zero), jnp.where(lo_half, zero, xr)
        else:
            low, high = jnp.where(lo_half, xr, zero), jnp.where(lo_half, zero, xc)
        return jnp.concatenate([low, high], axis=0).astype(BF16)

    n_pairs = N_HEADS // 2
    pairs_per_kv = Q_PER_KV // 2
    row_hi = lax.broadcasted_iota(jnp.int32, (4 * BLOCK, LANES), 0) >= 2 * BLOCK
    lane_hi = lax.broadcasted_iota(jnp.int32, (4 * BLOCK, LANES), 1) >= HEAD_DIM
    ones_cols = (row_hi == lane_hi).astype(BF16)
    kbd, vbd, mask, rows = [], [], [], []
    for sb in range(qb):
        rows.append(slice(BLOCK * sb, BLOCK * (sb + 1)))
        if sb == 0:
            k_prev = jnp.where(is_first, k0_ref[0], kp_ref[0])
            v_prev = jnp.where(is_first, v0_ref[0], vp_ref[0])
        else:
            k_prev, v_prev = kc_ref[0, rows[sb - 1], :], vc_ref[0, rows[sb - 1], :]
        k2 = jnp.concatenate([k_prev, kc_ref[0, rows[sb], :]], axis=0)
        v2 = jnp.concatenate([v_prev, vc_ref[0, rows[sb], :]], axis=0)
        gb = i * qb + sb + first_block
        col_min = jnp.where(gb == 0, 2 * BLOCK - n_meta, jnp.where(gb == 1, BLOCK - n_meta, 0))
        mask.append(in_window & (c >= col_min))
        kbd.append([pair_blocks(k2, kv) for kv in range(N_KV_HEADS)])
        vbd.append([jnp.concatenate([pair_blocks(v2, kv), ones_cols], axis=1) for kv in range(N_KV_HEADS)])
    halves = (slice(0, 2 * BLOCK), slice(2 * BLOCK, 4 * BLOCK))
    probs = [(sb, pr) for sb in range(qb) for pr in range(n_pairs)]
    s = {(sb, pr): jnp.where(mask[sb], lax.dot_general(
        q_ref[0, rows[sb], LANES * pr: LANES * (pr + 1)], kbd[sb][pr // pairs_per_kv],
        (((1,), (1,)), ((), ())), preferred_element_type=F32), NEG) for sb, pr in probs}
    m = {(sb, pr): [jnp.maximum(jnp.max(s[sb, pr][:, halves[hh]], axis=-1, keepdims=True), sink_ref[2 * pr + hh])
                    for hh in range(2)] for sb, pr in probs}
    p = {(sb, pr): jnp.concatenate([jnp.exp(s[sb, pr][:, halves[hh]] - m[sb, pr][hh]) for hh in range(2)],
                                   axis=1).astype(BF16) for sb, pr in probs}
    out_hi = lax.broadcasted_iota(jnp.int32, (BLOCK, LANES), 1) >= HEAD_DIM
    outs = {}
    for sb, pr in probs:
        ov = _mm(p[sb, pr], vbd[sb][pr // pairs_per_kv])
        sink_w = jnp.where(out_hi, jnp.exp(sink_ref[2 * pr + 1] - m[sb, pr][1]),
                           jnp.exp(sink_ref[2 * pr] - m[sb, pr][0]))
        outs[sb, pr] = ov[:, :LANES] / (ov[:, LANES:] + sink_w)
    for sb in range(qb):
        o = jnp.concatenate([outs[sb, pr] for pr in range(n_pairs)], axis=1)
        o_ref[0, rows[sb], :] = _rms(o, g_ref[...]).astype(BF16)


def _attn_seq(sinks, q, k, v, k0, v0, g, *, n_meta, first_block):
    nb, t_len, _ = q.shape
    nblk = t_len // BLOCK
    qb = ATTN_Q_BLOCKS if nblk % ATTN_Q_BLOCKS == 0 else 1
    cur = lambda b, i: (b, i, 0)
    prev = lambda b, i: (b, jnp.maximum(i * qb - 1, 0), 0)
    init = lambda b, i: (b, 0, 0)
    kern = functools.partial(_attn_seq_kernel, first_block=first_block, n_meta=n_meta, qb=qb)
    return pl.pallas_call(
        kern,
        grid=(nb, nblk // qb),
        in_specs=[
            pl.BlockSpec(memory_space=pltpu.SMEM),
            pl.BlockSpec((1, qb * BLOCK, D_ATTN), cur),
            pl.BlockSpec((1, qb * BLOCK, D_KV), cur),
            pl.BlockSpec((1, BLOCK, D_KV), prev),
            pl.BlockSpec((1, BLOCK, D_KV), init),
            pl.BlockSpec((1, qb * BLOCK, D_KV), cur),
            pl.BlockSpec((1, BLOCK, D_KV), prev),
            pl.BlockSpec((1, BLOCK, D_KV), init),
            _const_spec((1, D_ATTN)),
        ],
        out_specs=pl.BlockSpec((1, qb * BLOCK, D_ATTN), cur),
        out_shape=jax.ShapeDtypeStruct((nb, t_len, D_ATTN), BF16),
        compiler_params=pltpu.CompilerParams(
            dimension_semantics=("arbitrary", "arbitrary"), vmem_limit_bytes=VMEM_LIMIT),
        name="attn_seq",
    )(sinks, q, k, k, k0, v, v, v0, g)


def _attn_step_kernel(q_ref, kn_ref, vn_ref, ck_ref, cv_ref, sink_ref, g_ref, o_ref, ok_ref, ov_ref):
    is_last = lax.broadcasted_iota(jnp.int32, (1, WINDOW), 1) == WINDOW - 1
    bb = ck_ref.shape[0]

    def shift_in(cache_ref, new_ref, out_ref):
        new = new_ref[...]
        new = jnp.concatenate([new, jnp.zeros((new.shape[0], LANES - HEAD_DIM), F32)], axis=1)
        new = jnp.concatenate([new, jnp.zeros((LANES - new.shape[0], LANES), F32)], axis=0)
        cols = new.T
        for b in range(bb):
            for kv in range(N_KV_HEADS):
                j = b * N_KV_HEADS + kv
                out_ref[b, kv] = jnp.where(is_last, cols[0:HEAD_DIM, j:j + 1],
                                           pltpu.roll(cache_ref[b, kv], WINDOW - 1, 1))

    shift_in(ck_ref, kn_ref, ok_ref)
    shift_in(cv_ref, vn_ref, ov_ref)
    k_t = ok_ref[...]
    v_t = ov_ref[...]
    q = q_ref[...]
    kv_of_head = lax.broadcasted_iota(jnp.int32, (1, N_HEADS, 1), 1) // Q_PER_KV
    s = None
    for kv in range(N_KV_HEADS):
        s_kv = jnp.einsum('bhd,bdw->bhw', q, k_t[:, kv].astype(BF16), preferred_element_type=F32)
        s = s_kv if s is None else jnp.where(kv_of_head == kv, s_kv, s)
    sink = sink_ref[...][None]
    m = jnp.maximum(jnp.max(s, axis=-1, keepdims=True), sink)
    p = jnp.exp(s - m)
    den = jnp.sum(p, axis=-1, keepdims=True) + jnp.exp(sink - m)
    pb = (p / den).astype(BF16)
    o = None
    for kv in range(N_KV_HEADS):
        o_kv = jnp.einsum('bhw,bdw->bhd', pb, v_t[:, kv].astype(BF16), preferred_element_type=F32)
        o = o_kv if o is None else jnp.where(kv_of_head == kv, o_kv, o)
    ms = jnp.sum(jnp.sum(o * o, axis=2, keepdims=True), axis=1, keepdims=True) * (1.0 / D_ATTN)
    o_ref[...] = (o * lax.rsqrt(ms + EPS) * g_ref[...][None]).astype(BF16)


def _attn_step(q3, kn, vn, ck, cv, sinks, g, *, bb):
    nseq = q3.shape[0]
    blk3 = lambda i: (i, 0, 0)
    blk4 = lambda i: (i, 0, 0, 0)
    cache_blk = (bb, N_KV_HEADS, HEAD_DIM, WINDOW)
    new_spec = pl.BlockSpec((bb * N_KV_HEADS, HEAD_DIM), lambda i: (i, 0))
    return pl.pallas_call(
        _attn_step_kernel,
        grid=(nseq // bb,),
        in_specs=[
            pl.BlockSpec((bb, N_HEADS, HEAD_DIM), blk3),
            new_spec,
            new_spec,
            pl.BlockSpec(cache_blk, blk4),
            pl.BlockSpec(cache_blk, blk4),
            _const_spec((N_HEADS, 1)),
            _const_spec((N_HEADS, HEAD_DIM)),
        ],
        out_specs=[
            pl.BlockSpec((bb, N_HEADS, HEAD_DIM), blk3),
            pl.BlockSpec(cache_blk, blk4),
            pl.BlockSpec(cache_blk, blk4),
        ],
        out_shape=[
            jax.ShapeDtypeStruct((nseq, N_HEADS, HEAD_DIM), BF16),
            jax.ShapeDtypeStruct((nseq,) + cache_blk[1:], F32),
            jax.ShapeDtypeStruct((nseq,) + cache_blk[1:], F32),
        ],
        compiler_params=pltpu.CompilerParams(
            dimension_semantics=("arbitrary",), vmem_limit_bytes=VMEM_LIMIT),
        name="attn_step",
    )(q3, kn, vn, ck, cv, sinks, g)


def _out_proj_kernel(x_ref, ys_ref, ya_ref, w_ref, o_ref):
    o_ref[0] = (x_ref[0] + _mm(ys_ref[0], w_ref[0:D_SSM, :])
                + _mm(ya_ref[0], w_ref[D_SSM:D_SSM + D_ATTN, :]))


def _out_proj(x3, ys, ya, w_bf, *, tm):
    nb, t_len, _ = x3.shape
    blk = lambda b, t: (b, t, 0)
    return pl.pallas_call(
        _out_proj_kernel,
        grid=(nb, t_len // tm),
        in_specs=[
            pl.BlockSpec((1, tm, D_MODEL), blk),
            pl.BlockSpec((1, tm, D_SSM), blk),
            pl.BlockSpec((1, tm, D_ATTN), blk),
            _const_spec((D_SSM + D_ATTN, D_MODEL)),
        ],
        out_specs=pl.BlockSpec((1, tm, D_MODEL), blk),
        out_shape=jax.ShapeDtypeStruct((nb, t_len, D_MODEL), F32),
        compiler_params=pltpu.CompilerParams(
            dimension_semantics=("arbitrary", "arbitrary"), vmem_limit_bytes=VMEM_LIMIT),
        name="out_proj",
    )(x3, ys, ya, w_bf)


def _ffn_seq_kernel(x_ref, g_ref, wv_ref, wg_ref, cwv_ref, cwg_ref, cbv_ref, cbg_ref, wd_ref,
                    bufv_ref, bufg_ref, y_ref, nbv_ref, nbg_ref,
                    h_s, upv_s, upg_s, *, tm, tf):
    t = pl.program_id(1)
    j = pl.program_id(2)
    chan = pl.ds(pl.multiple_of(j * tf, tf), tf)

    @pl.when(j == 0)
    def _():
        x = x_ref[...]
        h_s[...] = _rms(x, g_ref[...]).astype(BF16)
        y_ref[...] = x

    @pl.when(t == 0)
    def _():
        upv_s[0:SUBLANES, :] = bufv_ref[0]
        upg_s[0:SUBLANES, :] = bufg_ref[0]

    @pl.when(t > 0)
    def _():
        upv_s[0:SUBLANES, :] = nbv_ref[0, :, chan]
        upg_s[0:SUBLANES, :] = nbg_ref[0, :, chan]

    rm = min(tm, FFN_SUB_ROWS)
    n_sub = tm // rm

    def up_proj(up_s, w_ref, r0):
        up_s[SUBLANES + r0:SUBLANES + r0 + rm, :] = _mm(h_s[r0:r0 + rm, :], w_ref[...])

    def conv(up_s, cw_ref, cb_ref, r0):
        return (cb_ref[...] + cw_ref[0:1, :] * up_s[SUBLANES - 2 + r0:SUBLANES - 2 + r0 + rm, :]
                + cw_ref[1:2, :] * up_s[SUBLANES - 1 + r0:SUBLANES - 1 + r0 + rm, :]
                + cw_ref[2:3, :] * up_s[SUBLANES + r0:SUBLANES + r0 + rm, :])

    def gate_down(r0):
        val = conv(upv_s, cwv_ref, cbv_ref, r0)
        gate = conv(upg_s, cwg_ref, cbg_ref, r0)
        act = (jax.nn.silu(gate) * val).astype(BF16)
        y_ref[r0:r0 + rm, :] += _mm(act, wd_ref[...])

    for r in range(n_sub):
        up_proj(upv_s, wv_ref, r * rm)
        up_proj(upg_s, wg_ref, r * rm)
        if r > 0:
            gate_down((r - 1) * rm)
    gate_down((n_sub - 1) * rm)
    nbv_ref[0, :, chan] = upv_s[tm:tm + SUBLANES, :]
    nbg_ref[0, :, chan] = upg_s[tm:tm + SUBLANES, :]


def _ffn_seq(x1, g, wup_bf, conv_w, conv_b, wdown_bf, bufv, bufg, *, nb, tm):
    n = x1.shape[0]
    nt = n // nb // tm
    tf = FF_TILE
    nj = D_FF // tf
    row = lambda b, t, j: (b * nt + t, 0)
    kern = functools.partial(_ffn_seq_kernel, tm=tm, tf=tf)
    return pl.pallas_call(
        kern,
        grid=(nb, nt, nj),
        in_specs=[
            pl.BlockSpec((tm, D_MODEL), row),
            _const_spec((1, D_MODEL)),
            pl.BlockSpec((D_MODEL, tf), lambda b, t, j: (0, j)),
            pl.BlockSpec((D_MODEL, tf), lambda b, t, j: (0, nj + j)),
            pl.BlockSpec((CONV_W, tf), lambda b, t, j: (0, j)),
            pl.BlockSpec((CONV_W, tf), lambda b, t, j: (0, nj + j)),
            pl.BlockSpec((1, tf), lambda b, t, j: (0, j)),
            pl.BlockSpec((1, tf), lambda b, t, j: (0, nj + j)),
            pl.BlockSpec((tf, D_MODEL), lambda b, t, j: (j, 0)),
            pl.BlockSpec((1, SUBLANES, tf), lambda b, t, j: (b, 0, j)),
            pl.BlockSpec((1, SUBLANES, tf), lambda b, t, j: (b, 0, j)),
        ],
        out_specs=[
            pl.BlockSpec((tm, D_MODEL), row),
            pl.BlockSpec((1, SUBLANES, D_FF), lambda b, t, j: (b, 0, 0)),
            pl.BlockSpec((1, SUBLANES, D_FF), lambda b, t, j: (b, 0, 0)),
        ],
        out_shape=[
            jax.ShapeDtypeStruct((n, D_MODEL), F32),
            jax.ShapeDtypeStruct((nb, SUBLANES, D_FF), F32),
            jax.ShapeDtypeStruct((nb, SUBLANES, D_FF), F32),
        ],
        scratch_shapes=[
            pltpu.VMEM((tm, D_MODEL), BF16),
            pltpu.VMEM((tm + SUBLANES, tf), F32),
            pltpu.VMEM((tm + SUBLANES, tf), F32),
        ],
        compiler_params=pltpu.CompilerParams(
            dimension_semantics=("arbitrary", "arbitrary", "arbitrary"), vmem_limit_bytes=VMEM_LIMIT),
        name="ffn_seq",
    )(x1, g, wup_bf, wup_bf, conv_w, conv_w, conv_b, conv_b, wdown_bf, bufv, bufg)


def _ffn_step_kernel(x_ref, g_ref, wv_ref, wg_ref, cwv_ref, cwg_ref, cbv_ref, cbg_ref, wd_ref,
                     bufv_ref, bufg_ref, xt_ref, y_ref, nbv_ref, nbg_ref, tv_ref, tg_ref, h_s, ht_s):
    j = pl.program_id(0)

    @pl.when(j == 0)
    def _():
        x = x_ref[...]
        h_s[...] = _rms(x, g_ref[...]).astype(BF16)
        ht_s[...] = _rms(xt_ref[...], g_ref[...]).astype(BF16)
        y_ref[...] = x

    n_tail = ht_s.shape[0]
    tv_ref[...] = _mm(ht_s[...], wv_ref[...])[n_tail - SUBLANES:, :]
    tg_ref[...] = _mm(ht_s[...], wg_ref[...])[n_tail - SUBLANES:, :]
    h = h_s[...]
    upv = _mm(h, wv_ref[...])
    upg = _mm(h, wg_ref[...])
    val = cbv_ref[...] + cwv_ref[0:1, :] * bufv_ref[:, 0, :] + cwv_ref[1:2, :] * bufv_ref[:, 1, :] + cwv_ref[2:3, :] * upv
    gate = cbg_ref[...] + cwg_ref[0:1, :] * bufg_ref[:, 0, :] + cwg_ref[1:2, :] * bufg_ref[:, 1, :] + cwg_ref[2:3, :] * upg
    act = (jax.nn.silu(gate) * val).astype(BF16)
    y_ref[...] += _mm(act, wd_ref[...])
    nbv_ref[:, 0, :] = bufv_ref[:, 1, :]
    nbg_ref[:, 0, :] = bufg_ref[:, 1, :]
    nbv_ref[:, 1, :] = upv
    nbg_ref[:, 1, :] = upg


def _ffn_step(x1, g, wup_bf, conv_w, conv_b, wdown_bf, buf, x_tail):
    nseq = x1.shape[0]
    n_tail = x_tail.shape[0]
    tf = FF_TILE
    nj = D_FF // tf
    return pl.pallas_call(
        _ffn_step_kernel,
        grid=(nj,),
        in_specs=[
            _const_spec((nseq, D_MODEL)),
            _const_spec((1, D_MODEL)),
            pl.BlockSpec((D_MODEL, tf), lambda j: (0, j)),
            pl.BlockSpec((D_MODEL, tf), lambda j: (0, nj + j)),
            pl.BlockSpec((CONV_W, tf), lambda j: (0, j)),
            pl.BlockSpec((CONV_W, tf), lambda j: (0, nj + j)),
            pl.BlockSpec((1, tf), lambda j: (0, j)),
            pl.BlockSpec((1, tf), lambda j: (0, nj + j)),
            pl.BlockSpec((tf, D_MODEL), lambda j: (j, 0)),
            pl.BlockSpec((nseq, CONV_W - 1, tf), lambda j: (0, 0, j)),
            pl.BlockSpec((nseq, CONV_W - 1, tf), lambda j: (0, 0, nj + j)),
            _const_spec((n_tail, D_MODEL)),
        ],
        out_specs=[
            pl.BlockSpec((nseq, D_MODEL), lambda j: (0, 0)),
            pl.BlockSpec((nseq, CONV_W - 1, tf), lambda j: (0, 0, j)),
            pl.BlockSpec((nseq, CONV_W - 1, tf), lambda j: (0, 0, j)),
            pl.BlockSpec((SUBLANES, tf), lambda j: (0, j)),
            pl.BlockSpec((SUBLANES, tf), lambda j: (0, j)),
        ],
        out_shape=[
            jax.ShapeDtypeStruct((nseq, D_MODEL), F32),
            jax.ShapeDtypeStruct((nseq, CONV_W - 1, D_FF), F32),
            jax.ShapeDtypeStruct((nseq, CONV_W - 1, D_FF), F32),
            jax.ShapeDtypeStruct((SUBLANES, D_FF), F32),
            jax.ShapeDtypeStruct((SUBLANES, D_FF), F32),
        ],
        scratch_shapes=[pltpu.VMEM((nseq, D_MODEL), BF16), pltpu.VMEM((n_tail, D_MODEL), BF16)],
        compiler_params=pltpu.CompilerParams(
            dimension_semantics=("arbitrary",), vmem_limit_bytes=VMEM_LIMIT),
        name="ffn_step",
    )(x1, g, wup_bf, wup_bf, conv_w, conv_w, conv_b, conv_b, wdown_bf, buf, buf, x_tail)


def _rope_tables(pos):
    half = HEAD_DIM // 2
    inv = ROPE_THETA ** (-jnp.arange(half, dtype=F32) * 2.0 / HEAD_DIM)
    ang = pos.astype(F32)[:, None] * inv[None, :]
    c = jnp.cos(ang)
    s = jnp.sin(ang)
    return jnp.concatenate([c, c, c, c], axis=1), jnp.concatenate([-s, s, -s, s], axis=1)


def _block_diag(w):
    nd, ng, r, c = w.shape
    tiled = jnp.tile(w.reshape(nd, ng * r, c), (1, 1, ng))
    on_diag = (jnp.arange(ng * r)[:, None] // r) == (jnp.arange(ng * c)[None, :] // c)
    return jnp.where(on_diag[None], tiled, jnp.zeros_like(tiled))


def kernel(x_prompt, x_sample, cache_k, cache_v, state_ssm_re, state_ssm_im, state_conv, meta_tokens,
           norm_mix_g, w_in, q_norm_g, k_norm_g, attn_sinks, lam_re, lam_im, log_dt, ssm_b_re, ssm_b_im,
           ssm_c_re, ssm_c_im, ssm_d, w_glu, b_glu, ssm_out_g, attn_out_g, w_out, norm_ffn_g, w_up,
           conv_w, conv_b, w_down):
    depth = w_in.shape[0]
    assert depth == 1
    nb, seq, _ = x_prompt.shape
    nseq = x_sample.shape[0]
    n_meta = meta_tokens.shape[0]
    assert x_sample.shape[1] == 1 and n_meta == 16 and nb == SUBLANES
    assert seq % 1024 == 0 and nseq % SUBLANES == 0
    l = 0
    gpd = SSM_GROUPS_PER_DOT
    n_dots = N_SSM_GROUPS // gpd

    w_in_bf = w_in[l].astype(BF16)
    w_glu_bf = w_glu[l].astype(BF16)
    w_out_bf = w_out[l].astype(BF16)
    w_up_bf = w_up[l].astype(BF16)
    w_down_bf = w_down[l].astype(BF16)
    g_mix = norm_mix_g[l][None, :]
    g_ffn = norm_ffn_g[l][None, :]
    qg = jnp.tile(q_norm_g[l], 256 // HEAD_DIM)[None, :]
    kg = jnp.tile(k_norm_g[l], 256 // HEAD_DIM)[None, :]
    lane_head = jnp.arange(256) // HEAD_DIM
    ones_bd = (lane_head[:, None] == lane_head[None, :]).astype(BF16)
    sinks = attn_sinks[l]
    lamre = lam_re[l].reshape(1, D_STATE)
    lamim = lam_im[l].reshape(1, D_STATE)
    logdt = jnp.repeat(log_dt[l], SSM_STATE)[None, :]
    b_re_t = jnp.swapaxes(ssm_b_re[l], 1, 2).reshape(n_dots, gpd, SSM_GROUP, SSM_STATE)
    b_im_t = jnp.swapaxes(ssm_b_im[l], 1, 2).reshape(n_dots, gpd, SSM_GROUP, SSM_STATE)
    bblk = jnp.concatenate([_block_diag(b_re_t), _block_diag(b_im_t)], axis=2)
    cre = _block_diag(jnp.swapaxes(ssm_c_re[l], 1, 2).reshape(n_dots, gpd, SSM_STATE, SSM_GROUP)).astype(BF16)
    cim = _block_diag(jnp.swapaxes(ssm_c_im[l], 1, 2).reshape(n_dots, gpd, SSM_STATE, SSM_GROUP)).astype(BF16)
    dskip = ssm_d[l].reshape(1, D_SSM)
    bglu = b_glu[l][None, :]
    g_ssm = ssm_out_g[l][None, :]
    g_att = attn_out_g[l][None, :]
    cw = conv_w[l]
    cb = conv_b[l][None, :]

    ssm_w = (lamre, lamim, logdt, bblk, cre, cim, dskip, w_glu_bf, bglu, g_ssm)

    pad = BLOCK - n_meta
    xm = jnp.concatenate([jnp.zeros((pad, D_MODEL), F32), meta_tokens.astype(F32)], axis=0)[None]
    cos_m, sin_m = _rope_tables(jnp.maximum(jnp.arange(BLOCK) - pad, 0))
    u_m, q_m, k_m, v_m = _in_proj(xm, g_mix, w_in_bf, cos_m, sin_m, qg, kg, ones_bd, tq=BLOCK)
    u_m8 = jnp.broadcast_to(u_m[pad:], (n_meta, SUBLANES, D_SSM)).reshape(n_meta * SUBLANES, D_SSM)
    zero_state = jnp.zeros((SUBLANES, D_STATE), F32)
    ys_m8, hre_m, him_m = _ssm(u_m8, zero_state, zero_state, *ssm_w, nbatch=SUBLANES, tt=n_meta)
    ys_m = jnp.concatenate([jnp.zeros((1, pad, D_SSM), BF16), ys_m8[0:1]], axis=1)
    zero_kv = jnp.zeros((1, BLOCK, D_KV), F32)
    ya_m = _attn_seq(sinks, q_m, k_m, v_m, zero_kv, zero_kv, g_att, n_meta=n_meta, first_block=0)
    x1_m = _out_proj(xm, ys_m, ya_m, w_out_bf, tm=BLOCK)

    xs = x_sample.reshape(1, nseq, D_MODEL)
    cos_s, sin_s = _rope_tables(jnp.full((nseq,), PAST_LEN, jnp.int32))
    u_s, q_s, k_s, v_s = _in_proj(xs, g_mix, w_in_bf, cos_s, sin_s, qg, kg, ones_bd, tq=nseq)
    ys_s, hre_s, him_s = _ssm(u_s.reshape(nseq, D_SSM), state_ssm_re[l].reshape(nseq, D_STATE),
                              state_ssm_im[l].reshape(nseq, D_STATE), *ssm_w, nbatch=nseq, tt=1)
    ya_s, nk_s, nv_s = _attn_step(
        q_s.reshape(nseq, N_HEADS, HEAD_DIM), k_s.reshape(nseq * N_KV_HEADS, HEAD_DIM),
        v_s.reshape(nseq * N_KV_HEADS, HEAD_DIM),
        jnp.transpose(cache_k[l], (0, 2, 3, 1)), jnp.transpose(cache_v[l], (0, 2, 3, 1)),
        sinks[:, None], attn_out_g[l].reshape(N_HEADS, HEAD_DIM), bb=8)
    x1_s = _out_proj(xs, ys_s, ya_s.reshape(1, nseq, D_ATTN), w_out_bf, tm=nseq)
    y_s, nbv_s, nbg_s, bufv_m, bufg_m = _ffn_step(x1_s[0], g_ffn, w_up_bf, cw, cb, w_down_bf, state_conv[l],
                                                  x1_m[0, BLOCK - 2 * SUBLANES:])

    y_sample = y_s.reshape(nseq, 1, D_MODEL)
    sample_k = jnp.transpose(nk_s, (0, 3, 1, 2))[None]
    sample_v = jnp.transpose(nv_s, (0, 3, 1, 2))[None]
    sample_re = hre_s.reshape(nseq, N_SSM_GROUPS, SSM_STATE)[None]
    sample_im = him_s.reshape(nseq, N_SSM_GROUPS, SSM_STATE)[None]
    sample_conv = jnp.concatenate([nbv_s, nbg_s], axis=-1)[None]

    cos_p, sin_p = _rope_tables(n_meta + jnp.arange(seq))
    u_p, q_p, k_p, v_p = _in_proj(x_prompt, g_mix, w_in_bf, cos_p, sin_p, qg, kg, ones_bd, tq=128)
    ys_p, hre_p, him_p = _ssm(u_p.reshape(seq * nb, D_SSM), hre_m, him_m, *ssm_w, nbatch=nb, tt=SCAN_STEPS)
    k0 = jnp.broadcast_to(k_m, (nb, BLOCK, D_KV))
    v0 = jnp.broadcast_to(v_m, (nb, BLOCK, D_KV))
    ya_p = _attn_seq(sinks, q_p, k_p, v_p, k0, v0, g_att, n_meta=n_meta, first_block=1)
    x1_p = _out_proj(x_prompt, ys_p, ya_p, w_out_bf, tm=512)
    bufv0 = jnp.broadcast_to(bufv_m, (nb, SUBLANES, D_FF))
    bufg0 = jnp.broadcast_to(bufg_m, (nb, SUBLANES, D_FF))
    y_p, bufv_p, bufg_p = _ffn_seq(x1_p.reshape(nb * seq, D_MODEL), g_ffn, w_up_bf, cw, cb, w_down_bf,
                                   bufv0, bufg0, nb=nb, tm=1024)

    y_prompt = y_p.reshape(nb, seq, D_MODEL)
    prompt_k = k_p[:, seq - WINDOW:].reshape(nb, WINDOW, N_KV_HEADS, HEAD_DIM)[None]
    prompt_v = v_p[:, seq - WINDOW:].reshape(nb, WINDOW, N_KV_HEADS, HEAD_DIM)[None]
    prompt_re = hre_p.reshape(nb, N_SSM_GROUPS, SSM_STATE)[None]
    prompt_im = him_p.reshape(nb, N_SSM_GROUPS, SSM_STATE)[None]
    prompt_conv = jnp.concatenate([bufv_p[:, SUBLANES - 2:], bufg_p[:, SUBLANES - 2:]], axis=-1)[None]

    return (y_prompt, y_sample, prompt_k, prompt_v, prompt_re, prompt_im, prompt_conv,
            sample_k, sample_v, sample_re, sample_im, sample_conv)
```

```python
import functools

import jax
import jax.numpy as jnp
from jax import lax
from jax.experimental import pallas as pl
from jax.experimental.pallas import tpu as pltpu

F32 = jnp.float32
BF16 = jnp.bfloat16

D_MODEL = 2048
D_SSM = 1024
SSM_GROUP = 16
N_SSM_GROUPS = 64
SSM_STATE = 64
D_STATE = N_SSM_GROUPS * SSM_STATE
HEAD_DIM = 64
N_HEADS = 16
N_KV_HEADS = 4
Q_PER_KV = 4
D_ATTN = 1024
D_KV = 256
D_IN = D_SSM + D_ATTN + 2 * D_KV
WINDOW = 128
BLOCK = 128
ROPE_THETA = 10000.0
D_FF = 5632
CONV_W = 3
EPS = 1e-6
NEG = -1e30
PAST_LEN = 16384

SUBLANES = 8
LANES = 128
VMEM_LIMIT = 62 * 1024 * 1024
FF_TILE = 512
FFN_SUB_ROWS = 512
IN_SUB_ROWS = 256
ATTN_Q_BLOCKS = 4
SCAN_LANES = 512
SCAN_STEPS = 32
SSM_GROUPS_PER_DOT = 16


def _const_spec(shape):
    nd = len(shape)
    return pl.BlockSpec(shape, lambda *_: (0,) * nd, pipeline_mode=pl.Buffered(1))


def _rms(x, g):
    return x * lax.rsqrt(jnp.mean(x * x, axis=-1, keepdims=True) + EPS) * g


def _mm(a, b):
    return jnp.dot(a, b, preferred_element_type=F32)


def _in_proj_kernel(x_ref, g_ref, w_ref, cos_ref, sin_ref, qg_ref, kg_ref, ones_ref,
                    u_ref, q_ref, k_ref, v_ref, *, nb, tq, n_split):
    rs = nb * tq // n_split
    cos1 = jnp.concatenate([cos_ref[...], cos_ref[...]], axis=1)
    sin1 = jnp.concatenate([sin_ref[...], sin_ref[...]], axis=1)
    ones = ones_ref[...]
    n_q = D_ATTN // 256
    for s in range(n_split):
        segs = []
        for b in range(nb):
            lo = max(s * rs, b * tq)
            hi = min((s + 1) * rs, (b + 1) * tq)
            if lo < hi:
                segs.append((b, lo - b * tq, hi - lo, lo - s * rs))
        x = jnp.concatenate([x_ref[b, t0:t0 + n, :] for b, t0, n, _ in segs], axis=0)
        cos = jnp.concatenate([cos1[t0:t0 + n, :] for _, t0, n, _ in segs], axis=0)
        sin = jnp.concatenate([sin1[t0:t0 + n, :] for _, t0, n, _ in segs], axis=0)
        lane = lax.broadcasted_iota(jnp.int32, cos.shape, 1)
        first_half = (lane % HEAD_DIM) < (HEAD_DIM // 2)
        h = _rms(x, g_ref[...]).astype(BF16)
        z = _mm(h, w_ref[...])
        for b, t0, n, r0 in segs:
            u_ref[t0:t0 + n, b, :] = z[r0:r0 + n, :D_SSM]
            v_ref[b, t0:t0 + n, :] = z[r0:r0 + n, D_SSM + D_ATTN + D_KV:]
        for c in range(n_q + D_KV // 256):
            zc = z[:, D_SSM + 256 * c: D_SSM + 256 * (c + 1)]
            sq = zc * zc
            hi = sq.astype(BF16)
            lo = (sq - hi.astype(F32)).astype(BF16)
            ssum = _mm(hi, ones) + _mm(lo, ones)
            gain = qg_ref[...] if c < n_q else kg_ref[...]
            nrm = zc * lax.rsqrt(ssum * (1.0 / HEAD_DIM) + EPS) * gain
            partner = jnp.where(first_half, pltpu.roll(nrm, 256 - HEAD_DIM // 2, 1),
                                pltpu.roll(nrm, HEAD_DIM // 2, 1))
            r = nrm * cos + partner * sin
            for b, t0, n, r0 in segs:
                if c < n_q:
                    q_ref[b, t0:t0 + n, 256 * c: 256 * (c + 1)] = (
                        r[r0:r0 + n, :] * (HEAD_DIM ** -0.5)).astype(BF16)
                else:
                    k_ref[b, t0:t0 + n, :] = r[r0:r0 + n, :]


def _in_proj(x3, g, w_bf, cos, sin, qg, kg, ones_bd, *, tq):
    nb, t_len, _ = x3.shape
    blk = lambda t: (0, t, 0)
    kern = functools.partial(_in_proj_kernel, nb=nb, tq=tq, n_split=max(1, nb * tq // IN_SUB_ROWS))
    return pl.pallas_call(
        kern,
        grid=(t_len // tq,),
        in_specs=[
            pl.BlockSpec((nb, tq, D_MODEL), blk),
            _const_spec((1, D_MODEL)),
            _const_spec((D_MODEL, D_IN)),
            pl.BlockSpec((tq, LANES), lambda t: (t, 0)),
            pl.BlockSpec((tq, LANES), lambda t: (t, 0)),
            _const_spec((1, 256)),
            _const_spec((1, 256)),
            _const_spec((256, 256)),
        ],
        out_specs=[
            pl.BlockSpec((tq, nb, D_SSM), lambda t: (t, 0, 0)),
            pl.BlockSpec((nb, tq, D_ATTN), blk),
            pl.BlockSpec((nb, tq, D_KV), blk),
            pl.BlockSpec((nb, tq, D_KV), blk),
        ],
        out_shape=[
            jax.ShapeDtypeStruct((t_len, nb, D_SSM), F32),
            jax.ShapeDtypeStruct((nb, t_len, D_ATTN), BF16),
            jax.ShapeDtypeStruct((nb, t_len, D_KV), F32),
            jax.ShapeDtypeStruct((nb, t_len, D_KV), F32),
        ],
        compiler_params=pltpu.CompilerParams(
            dimension_semantics=("arbitrary",), vmem_limit_bytes=VMEM_LIMIT),
        name="in_proj",
    )(x3, g, w_bf, cos, sin, qg, kg, ones_bd)


def _ssm_kernel(u_ref, h0re_ref, h0im_ref, lamre_ref, lamim_ref, logdt_ref, bblk_ref,
                cre_ref, cim_ref, dskip_ref, wglu_ref, bglu_ref, g_ref, perm_ref,
                y_ref, sre_ref, sim_ref,
                are_s, aim_s, bbar_s, xre_s, xim_s, stre_s, stim_s, *, tt, nbg):
    step = pl.program_id(0)
    lanes_per_dot = SSM_GROUPS_PER_DOT * SSM_STATE
    chans_per_dot = SSM_GROUPS_PER_DOT * SSM_GROUP
    n_dots = N_SSM_GROUPS // SSM_GROUPS_PER_DOT
    state_lanes = [slice(lanes_per_dot * c, lanes_per_dot * (c + 1)) for c in range(n_dots)]
    chans = [slice(chans_per_dot * c, chans_per_dot * (c + 1)) for c in range(n_dots)]

    @pl.when(step == 0)
    def _():
        lr = lamre_ref[...]
        li = lamim_ref[...]
        dt = jnp.exp(logdt_ref[...])
        mag = jnp.exp(lr * dt)
        a_re = mag * jnp.cos(li * dt)
        a_im = mag * jnp.sin(li * dt)
        den = lr * lr + li * li
        am1 = a_re - 1.0
        cf_re = (am1 * lr + a_im * li) / den
        cf_im = (a_im * lr - am1 * li) / den
        shape = (SUBLANES, D_STATE)
        are_s[...] = jnp.broadcast_to(a_re, shape)
        aim_s[...] = jnp.broadcast_to(a_im, shape)
        row_grp = lax.broadcasted_iota(jnp.int32, (chans_per_dot, lanes_per_dot), 0) // SSM_GROUP
        lane_grp = lax.broadcasted_iota(jnp.int32, (chans_per_dot, lanes_per_dot), 1) // SSM_STATE
        on_diag = row_grp == lane_grp

        def expand(b):
            return jnp.where(on_diag, jnp.concatenate([b] * SSM_GROUPS_PER_DOT, axis=1), 0.0)

        for c in range(n_dots):
            b_re = expand(bblk_ref[0, c])
            b_im = expand(bblk_ref[1, c])
            c_re = cf_re[:, state_lanes[c]]
            c_im = cf_im[:, state_lanes[c]]
            bbar_s[c, :, :lanes_per_dot] = (c_re * b_re - c_im * b_im).astype(BF16)
            bbar_s[c, :, lanes_per_dot:] = (c_re * b_im + c_im * b_re).astype(BF16)
        stre_s[...] = h0re_ref[...]
        stim_s[...] = h0im_ref[...]

    u = u_ref[...]
    ub = u.astype(BF16)

    def project(c):
        bu = _mm(ub[:, chans[c]], bbar_s[c])
        xre_s[:, state_lanes[c]] = bu[:, :lanes_per_dot]
        xim_s[:, state_lanes[c]] = bu[:, lanes_per_dot:]

    def scan(c):
        blocks = [slice(lanes_per_dot * c + SCAN_LANES * i, lanes_per_dot * c + SCAN_LANES * (i + 1))
                  for i in range(lanes_per_dot // SCAN_LANES)]
        for bg in range(nbg):
            rs = slice(SUBLANES * bg, SUBLANES * (bg + 1))
            a = [(are_s[:, ls], aim_s[:, ls]) for ls in blocks]
            st = [(stre_s[rs, ls], stim_s[rs, ls]) for ls in blocks]
            for t in range(tt):
                rt = slice((t * nbg + bg) * SUBLANES, (t * nbg + bg + 1) * SUBLANES)
                for i, ls in enumerate(blocks):
                    (a_re, a_im), (s_re, s_im) = a[i], st[i]
                    n_re = a_re * s_re - a_im * s_im + xre_s[rt, ls]
                    n_im = a_re * s_im + a_im * s_re + xim_s[rt, ls]
                    xre_s[rt, ls] = n_re
                    xim_s[rt, ls] = n_im
                    st[i] = (n_re, n_im)
            for i, ls in enumerate(blocks):
                stre_s[rs, ls] = st[i][0]
                stim_s[rs, ls] = st[i][1]

    def readout(c):
        yc = (_mm(xre_s[:, state_lanes[c]].astype(BF16), cre_ref[c])
              - _mm(xim_s[:, state_lanes[c]].astype(BF16), cim_ref[c]))
        return jax.nn.gelu(yc + dskip_ref[:, chans[c]] * u[:, chans[c]])

    ys = [None] * n_dots
    project(0)
    for c in range(n_dots):
        if c + 1 < n_dots:
            project(c + 1)
        scan(c)
        if c > 0:
            ys[c - 1] = readout(c - 1)
    ys[n_dots - 1] = readout(n_dots - 1)
    y = jnp.concatenate(ys, axis=1)
    y = y * jax.nn.sigmoid(_mm(y.astype(BF16), wglu_ref[...]) + bglu_ref[...])
    yn = _rms(y, g_ref[...]).astype(BF16)
    if tt == 1:
        y_ref[0] = yn
    else:
        y_ref[...] = _mm(perm_ref[...], yn).astype(BF16).reshape(y_ref.shape)

    @pl.when(step == pl.num_programs(0) - 1)
    def _():
        sre_ref[...] = stre_s[...]
        sim_ref[...] = stim_s[...]


def _ssm(u_tm, h0re, h0im, lamre, lamim, logdt, bblk, cre, cim, dskip, wglu, bglu, g, *, nbatch, tt):
    n = u_tm.shape[0]
    t_len = n // nbatch
    nbg = nbatch // SUBLANES
    rows = tt * nbatch
    n_grp = nbatch if tt > 1 else 1
    r = jnp.arange(rows)
    src = (r % tt) * nbatch + r // tt
    perm = (src[:, None] == r[None, :]).astype(BF16)
    kern = functools.partial(_ssm_kernel, tt=tt, nbg=nbg)
    return pl.pallas_call(
        kern,
        grid=(t_len // tt,),
        in_specs=[
            pl.BlockSpec((rows, D_SSM), lambda s: (s, 0)),
            _const_spec((nbatch, D_STATE)),
            _const_spec((nbatch, D_STATE)),
            _const_spec((1, D_STATE)),
            _const_spec((1, D_STATE)),
            _const_spec((1, D_STATE)),
            _const_spec(bblk.shape),
            _const_spec(cre.shape),
            _const_spec(cim.shape),
            _const_spec((1, D_SSM)),
            _const_spec((D_SSM, D_SSM)),
            _const_spec((1, D_SSM)),
            _const_spec((1, D_SSM)),
            _const_spec((rows, rows)),
        ],
        out_specs=[
            pl.BlockSpec((n_grp, rows // n_grp, D_SSM), lambda s: (0, s, 0)),
            pl.BlockSpec((nbatch, D_STATE), lambda s: (0, 0)),
            pl.BlockSpec((nbatch, D_STATE), lambda s: (0, 0)),
        ],
        out_shape=[
            jax.ShapeDtypeStruct((n_grp, n // n_grp, D_SSM), BF16),
            jax.ShapeDtypeStruct((nbatch, D_STATE), F32),
            jax.ShapeDtypeStruct((nbatch, D_STATE), F32),
        ],
        scratch_shapes=[pltpu.VMEM((SUBLANES, D_STATE), F32)] * 2
        + [pltpu.VMEM((bblk.shape[1], bblk.shape[2], 2 * SSM_GROUPS_PER_DOT * SSM_STATE), BF16)]
        + [pltpu.VMEM((rows, D_STATE), F32)] * 2
        + [pltpu.VMEM((nbatch, D_STATE), F32)] * 2,
        compiler_params=pltpu.CompilerParams(
            dimension_semantics=("arbitrary",), vmem_limit_bytes=VMEM_LIMIT),
        name="ssm",
    )(u_tm, h0re, h0im, lamre, lamim, logdt, bblk, cre, cim, dskip, wglu, bglu, g, perm)


def _attn_seq_kernel(sink_ref, q_ref, kc_ref, kp_ref, k0_ref, vc_ref, vp_ref, v0_ref, g_ref, o_ref,
                     *, first_block, n_meta, qb):
    i = pl.program_id(1)
    is_first = i == 0
    r = lax.broadcasted_iota(jnp.int32, (BLOCK, 4 * BLOCK), 0)
    c = lax.broadcasted_iota(jnp.int32, (BLOCK, 4 * BLOCK), 1) % (2 * BLOCK)
    in_window = (c > r) & (c <= r + WINDOW)
    lo_half = lax.broadcasted_iota(jnp.int32, (2 * BLOCK, LANES), 1) < HEAD_DIM

    def pair_blocks(x2, kv):
        xc = x2[:, LANES * (kv // 2): LANES * (kv // 2 + 1)]
        xr = pltpu.roll(xc, HEAD_DIM, 1)
        zero = jnp.zeros_like(xc)
        if kv % 2 == 0:
            low, high = jnp.where(lo_half, xc, zero), jnp.where(lo_half, zero, xr)
        else:
            low, high = jnp.where(lo_half, xr, zero), jnp.where(lo_half, zero, xc)
        return jnp.concatenate([low, high], axis=0).astype(BF16)

    n_pairs = N_HEADS // 2
    pairs_per_kv = Q_PER_KV // 2
    row_hi = lax.broadcasted_iota(jnp.int32, (4 * BLOCK, LANES), 0) >= 2 * BLOCK
    lane_hi = lax.broadcasted_iota(jnp.int32, (4 * BLOCK, LANES), 1) >= HEAD_DIM
    ones_cols = (row_hi == lane_hi).astype(BF16)
    kbd, vbd, mask, rows = [], [], [], []
    for sb in range(qb):
        rows.append(slice(BLOCK * sb, BLOCK * (sb + 1)))
        if sb == 0:
            k_prev = jnp.where(is_first, k0_ref[0], kp_ref[0])
            v_prev = jnp.where(is_first, v0_ref[0], vp_ref[0])
        else:
            k_prev, v_prev = kc_ref[0, rows[sb - 1], :], vc_ref[0, rows[sb - 1], :]
        k2 = jnp.concatenate([k_prev, kc_ref[0, rows[sb], :]], axis=0)
        v2 = jnp.concatenate([v_prev, vc_ref[0, rows[sb], :]], axis=0)
        gb = i * qb + sb + first_block
        col_min = jnp.where(gb == 0, 2 * BLOCK - n_meta, jnp.where(gb == 1, BLOCK - n_meta, 0))
        mask.append(in_window & (c >= col_min))
        kbd.append([pair_blocks(k2, kv) for kv in range(N_KV_HEADS)])
        vbd.append([jnp.concatenate([pair_blocks(v2, kv), ones_cols], axis=1) for kv in range(N_KV_HEADS)])
    halves = (slice(0, 2 * BLOCK), slice(2 * BLOCK, 4 * BLOCK))
    probs = [(sb, pr) for sb in range(qb) for pr in range(n_pairs)]
    s = {(sb, pr): jnp.where(mask[sb], lax.dot_general(
        q_ref[0, rows[sb], LANES * pr: LANES * (pr + 1)], kbd[sb][pr // pairs_per_kv],
        (((1,), (1,)), ((), ())), preferred_element_type=F32), NEG) for sb, pr in probs}
    m = {(sb, pr): [jnp.maximum(jnp.max(s[sb, pr][:, halves[hh]], axis=-1, keepdims=True), sink_ref[2 * pr + hh])
                    for hh in range(2)] for sb, pr in probs}
    p = {(sb, pr): jnp.concatenate([jnp.exp(s[sb, pr][:, halves[hh]] - m[sb, pr][hh]) for hh in range(2)],
                                   axis=1).astype(BF16) for sb, pr in probs}
    out_hi = lax.broadcasted_iota(jnp.int32, (BLOCK, LANES), 1) >= HEAD_DIM
    outs = {}
    for sb, pr in probs:
        ov = _mm(p[sb, pr], vbd[sb][pr // pairs_per_kv])
        sink_w = jnp.where(out_hi, jnp.exp(sink_ref[2 * pr + 1] - m[sb, pr][1]),
                           jnp.exp(sink_ref[2 * pr] - m[sb, pr][0]))
        outs[sb, pr] = ov[:, :LANES] / (ov[:, LANES:] + sink_w)
    for sb in range(qb):
        o = jnp.concatenate([outs[sb, pr] for pr in range(n_pairs)], axis=1)
        o_ref[0, rows[sb], :] = _rms(o, g_ref[...]).astype(BF16)


def _attn_seq(sinks, q, k, v, k0, v0, g, *, n_meta, first_block):
    nb, t_len, _ = q.shape
    nblk = t_len // BLOCK
    qb = ATTN_Q_BLOCKS if nblk % ATTN_Q_BLOCKS == 0 else 1
    cur = lambda b, i: (b, i, 0)
    prev = lambda b, i: (b, jnp.maximum(i * qb - 1, 0), 0)
    init = lambda b, i: (b, 0, 0)
    kern = functools.partial(_attn_seq_kernel, first_block=first_block, n_meta=n_meta, qb=qb)
    return pl.pallas_call(
        kern,
        grid=(nb, nblk // qb),
        in_specs=[
            pl.BlockSpec(memory_space=pltpu.SMEM),
            pl.BlockSpec((1, qb * BLOCK, D_ATTN), cur),
            pl.BlockSpec((1, qb * BLOCK, D_KV), cur),
            pl.BlockSpec((1, BLOCK, D_KV), prev),
            pl.BlockSpec((1, BLOCK, D_KV), init),
            pl.BlockSpec((1, qb * BLOCK, D_KV), cur),
            pl.BlockSpec((1, BLOCK, D_KV), prev),
            pl.BlockSpec((1, BLOCK, D_KV), init),
            _const_spec((1, D_ATTN)),
        ],
        out_specs=pl.BlockSpec((1, qb * BLOCK, D_ATTN), cur),
        out_shape=jax.ShapeDtypeStruct((nb, t_len, D_ATTN), BF16),
        compiler_params=pltpu.CompilerParams(
            dimension_semantics=("arbitrary", "arbitrary"), vmem_limit_bytes=VMEM_LIMIT),
        name="attn_seq",
    )(sinks, q, k, k, k0, v, v, v0, g)


def _attn_step_kernel(q_ref, kn_ref, vn_ref, ck_ref, cv_ref, sink_ref, g_ref, o_ref, ok_ref, ov_ref):
    is_last = lax.broadcasted_iota(jnp.int32, (1, WINDOW), 1) == WINDOW - 1
    bb = ck_ref.shape[0]

    def shift_in(cache_ref, new_ref, out_ref):
        new = new_ref[...]
        new = jnp.concatenate([new, jnp.zeros((new.shape[0], LANES - HEAD_DIM), F32)], axis=1)
        new = jnp.concatenate([new, jnp.zeros((LANES - new.shape[0], LANES), F32)], axis=0)
        cols = new.T
        for b in range(bb):
            for kv in range(N_KV_HEADS):
                j = b * N_KV_HEADS + kv
                out_ref[b, kv] = jnp.where(is_last, cols[0:HEAD_DIM, j:j + 1],
                                           pltpu.roll(cache_ref[b, kv], WINDOW - 1, 1))

    shift_in(ck_ref, kn_ref, ok_ref)
    shift_in(cv_ref, vn_ref, ov_ref)
    k_t = ok_ref[...]
    v_t = ov_ref[...]
    q = q_ref[...]
    kv_of_head = lax.broadcasted_iota(jnp.int32, (1, N_HEADS, 1), 1) // Q_PER_KV
    s = None
    for kv in range(N_KV_HEADS):
        s_kv = jnp.einsum('bhd,bdw->bhw', q, k_t[:, kv].astype(BF16), preferred_element_type=F32)
        s = s_kv if s is None else jnp.where(kv_of_head == kv, s_kv, s)
    sink = sink_ref[...][None]
    m = jnp.maximum(jnp.max(s, axis=-1, keepdims=True), sink)
    p = jnp.exp(s - m)
    den = jnp.sum(p, axis=-1, keepdims=True) + jnp.exp(sink - m)
    pb = (p / den).astype(BF16)
    o = None
    for kv in range(N_KV_HEADS):
        o_kv = jnp.einsum('bhw,bdw->bhd', pb, v_t[:, kv].astype(BF16), preferred_element_type=F32)
        o = o_kv if o is None else jnp.where(kv_of_head == kv, o_kv, o)
    ms = jnp.sum(jnp.sum(o * o, axis=2, keepdims=True), axis=1, keepdims=True) * (1.0 / D_ATTN)
    o_ref[...] = (o * lax.rsqrt(ms + EPS) * g_ref[...][None]).astype(BF16)


def _attn_step(q3, kn, vn, ck, cv, sinks, g, *, bb):
    nseq = q3.shape[0]
    blk3 = lambda i: (i, 0, 0)
    blk4 = lambda i: (i, 0, 0, 0)
    cache_blk = (bb, N_KV_HEADS, HEAD_DIM, WINDOW)
    new_spec = pl.BlockSpec((bb * N_KV_HEADS, HEAD_DIM), lambda i: (i, 0))
    return pl.pallas_call(
        _attn_step_kernel,
        grid=(nseq // bb,),
        in_specs=[
            pl.BlockSpec((bb, N_HEADS, HEAD_DIM), blk3),
            new_spec,
            new_spec,
            pl.BlockSpec(cache_blk, blk4),
            pl.BlockSpec(cache_blk, blk4),
            _const_spec((N_HEADS, 1)),
            _const_spec((N_HEADS, HEAD_DIM)),
        ],
        out_specs=[
            pl.BlockSpec((bb, N_HEADS, HEAD_DIM), blk3),
            pl.BlockSpec(cache_blk, blk4),
            pl.BlockSpec(cache_blk, blk4),
        ],
        out_shape=[
            jax.ShapeDtypeStruct((nseq, N_HEADS, HEAD_DIM), BF16),
            jax.ShapeDtypeStruct((nseq,) + cache_blk[1:], F32),
            jax.ShapeDtypeStruct((nseq,) + cache_blk[1:], F32),
        ],
        compiler_params=pltpu.CompilerParams(
            dimension_semantics=("arbitrary",), vmem_limit_bytes=VMEM_LIMIT),
        name="attn_step",
    )(q3, kn, vn, ck, cv, sinks, g)


def _out_proj_kernel(x_ref, ys_ref, ya_ref, w_ref, o_ref):
    o_ref[0] = (x_ref[0] + _mm(ys_ref[0], w_ref[0:D_SSM, :])
                + _mm(ya_ref[0], w_ref[D_SSM:D_SSM + D_ATTN, :]))


def _out_proj(x3, ys, ya, w_bf, *, tm):
    nb, t_len, _ = x3.shape
    blk = lambda b, t: (b, t, 0)
    return pl.pallas_call(
        _out_proj_kernel,
        grid=(nb, t_len // tm),
        in_specs=[
            pl.BlockSpec((1, tm, D_MODEL), blk),
            pl.BlockSpec((1, tm, D_SSM), blk),
            pl.BlockSpec((1, tm, D_ATTN), blk),
            _const_spec((D_SSM + D_ATTN, D_MODEL)),
        ],
        out_specs=pl.BlockSpec((1, tm, D_MODEL), blk),
        out_shape=jax.ShapeDtypeStruct((nb, t_len, D_MODEL), F32),
        compiler_params=pltpu.CompilerParams(
            dimension_semantics=("arbitrary", "arbitrary"), vmem_limit_bytes=VMEM_LIMIT),
        name="out_proj",
    )(x3, ys, ya, w_bf)


def _ffn_seq_kernel(x_ref, g_ref, wv_ref, wg_ref, cwv_ref, cwg_ref, cbv_ref, cbg_ref, wd_ref,
                    bufv_ref, bufg_ref, y_ref, nbv_ref, nbg_ref,
                    h_s, upv_s, upg_s, *, tm, tf):
    t = pl.program_id(1)
    j = pl.program_id(2)
    chan = pl.ds(pl.multiple_of(j * tf, tf), tf)

    @pl.when(j == 0)
    def _():
        x = x_ref[...]
        h_s[...] = _rms(x, g_ref[...]).astype(BF16)
        y_ref[...] = x

    @pl.when(t == 0)
    def _():
        upv_s[0:SUBLANES, :] = bufv_ref[0]
        upg_s[0:SUBLANES, :] = bufg_ref[0]

    @pl.when(t > 0)
    def _():
        upv_s[0:SUBLANES, :] = nbv_ref[0, :, chan]
        upg_s[0:SUBLANES, :] = nbg_ref[0, :, chan]

    rm = min(tm, FFN_SUB_ROWS)
    n_sub = tm // rm

    def up_proj(up_s, w_ref, r0):
        up_s[SUBLANES + r0:SUBLANES + r0 + rm, :] = _mm(h_s[r0:r0 + rm, :], w_ref[...])

    def conv(up_s, cw_ref, cb_ref, r0):
        return (cb_ref[...] + cw_ref[0:1, :] * up_s[SUBLANES - 2 + r0:SUBLANES - 2 + r0 + rm, :]
                + cw_ref[1:2, :] * up_s[SUBLANES - 1 + r0:SUBLANES - 1 + r0 + rm, :]
                + cw_ref[2:3, :] * up_s[SUBLANES + r0:SUBLANES + r0 + rm, :])

    def gate_down(r0):
        val = conv(upv_s, cwv_ref, cbv_ref, r0)
        gate = conv(upg_s, cwg_ref, cbg_ref, r0)
        act = (jax.nn.silu(gate) * val).astype(BF16)
        y_ref[r0:r0 + rm, :] += _mm(act, wd)

    wd = wd_ref[...].astype(BF16)
    for r in range(n_sub):
        up_proj(upv_s, wv_ref, r * rm)
        up_proj(upg_s, wg_ref, r * rm)
        if r > 0:
            gate_down((r - 1) * rm)
    gate_down((n_sub - 1) * rm)
    nbv_ref[0, :, chan] = upv_s[tm:tm + SUBLANES, :]
    nbg_ref[0, :, chan] = upg_s[tm:tm + SUBLANES, :]


def _ffn_seq(x1, g, wup_bf, conv_w, conv_b, wdown, bufv, bufg, *, nb, tm):
    n = x1.shape[0]
    nt = n // nb // tm
    tf = FF_TILE
    nj = D_FF // tf
    row = lambda b, t, j: (b * nt + t, 0)
    kern = functools.partial(_ffn_seq_kernel, tm=tm, tf=tf)
    return pl.pallas_call(
        kern,
        grid=(nb, nt, nj),
        in_specs=[
            pl.BlockSpec((tm, D_MODEL), row),
            _const_spec((1, D_MODEL)),
            pl.BlockSpec((D_MODEL, tf), lambda b, t, j: (0, j)),
            pl.BlockSpec((D_MODEL, tf), lambda b, t, j: (0, nj + j)),
            pl.BlockSpec((CONV_W, tf), lambda b, t, j: (0, j)),
            pl.BlockSpec((CONV_W, tf), lambda b, t, j: (0, nj + j)),
            pl.BlockSpec((1, tf), lambda b, t, j: (0, j)),
            pl.BlockSpec((1, tf), lambda b, t, j: (0, nj + j)),
            pl.BlockSpec((tf, D_MODEL), lambda b, t, j: (j, 0)),
            pl.BlockSpec((1, SUBLANES, tf), lambda b, t, j: (b, 0, j)),
            pl.BlockSpec((1, SUBLANES, tf), lambda b, t, j: (b, 0, j)),
        ],
        out_specs=[
            pl.BlockSpec((tm, D_MODEL), row),
            pl.BlockSpec((1, SUBLANES, D_FF), lambda b, t, j: (b, 0, 0)),
            pl.BlockSpec((1, SUBLANES, D_FF), lambda b, t, j: (b, 0, 0)),
        ],
        out_shape=[
            jax.ShapeDtypeStruct((n, D_MODEL), F32),
            jax.ShapeDtypeStruct((nb, SUBLANES, D_FF), F32),
            jax.ShapeDtypeStruct((nb, SUBLANES, D_FF), F32),
        ],
        scratch_shapes=[
            pltpu.VMEM((tm, D_MODEL), BF16),
            pltpu.VMEM((tm + SUBLANES, tf), F32),
            pltpu.VMEM((tm + SUBLANES, tf), F32),
        ],
        compiler_params=pltpu.CompilerParams(
            dimension_semantics=("arbitrary", "arbitrary", "arbitrary"), vmem_limit_bytes=VMEM_LIMIT),
        name="ffn_seq",
    )(x1, g, wup_bf, wup_bf, conv_w, conv_w, conv_b, conv_b, wdown, bufv, bufg)


def _ffn_step_kernel(x_ref, g_ref, wv_ref, wg_ref, cwv_ref, cwg_ref, cbv_ref, cbg_ref, wd_ref,
                     bufv_ref, bufg_ref, xt_ref, y_ref, nbuf_ref, tv_ref, tg_ref, h_s, ht_s, *, tf):
    j = pl.program_id(0)

    @pl.when(j == 0)
    def _():
        x = x_ref[...]
        h_s[...] = _rms(x, g_ref[...]).astype(BF16)
        ht_s[...] = _rms(xt_ref[...], g_ref[...]).astype(BF16)
        y_ref[...] = x

    n_tail = ht_s.shape[0]
    tv_ref[...] = _mm(ht_s[...], wv_ref[...])[n_tail - SUBLANES:, :]
    tg_ref[...] = _mm(ht_s[...], wg_ref[...])[n_tail - SUBLANES:, :]
    h = h_s[...]
    upv = _mm(h, wv_ref[...])
    upg = _mm(h, wg_ref[...])
    val = cbv_ref[...] + cwv_ref[0:1, :] * bufv_ref[:, 0, :] + cwv_ref[1:2, :] * bufv_ref[:, 1, :] + cwv_ref[2:3, :] * upv
    gate = cbg_ref[...] + cwg_ref[0:1, :] * bufg_ref[:, 0, :] + cwg_ref[1:2, :] * bufg_ref[:, 1, :] + cwg_ref[2:3, :] * upg
    act = (jax.nn.silu(gate) * val).astype(BF16)
    y_ref[...] += _mm(act, wd_ref[...].astype(BF16))
    val_cols = pl.ds(pl.multiple_of(j * tf, tf), tf)
    gate_cols = pl.ds(pl.multiple_of(D_FF + j * tf, tf), tf)
    nbuf_ref[:, 0, val_cols] = bufv_ref[:, 1, :]
    nbuf_ref[:, 0, gate_cols] = bufg_ref[:, 1, :]
    nbuf_ref[:, 1, val_cols] = upv
    nbuf_ref[:, 1, gate_cols] = upg


def _ffn_step(x1, g, wup_bf, conv_w, conv_b, wdown, buf, x_tail):
    nseq = x1.shape[0]
    n_tail = x_tail.shape[0]
    tf = FF_TILE
    nj = D_FF // tf
    return pl.pallas_call(
        functools.partial(_ffn_step_kernel, tf=tf),
        grid=(nj,),
        in_specs=[
            _const_spec((nseq, D_MODEL)),
            _const_spec((1, D_MODEL)),
            pl.BlockSpec((D_MODEL, tf), lambda j: (0, j)),
            pl.BlockSpec((D_MODEL, tf), lambda j: (0, nj + j)),
            pl.BlockSpec((CONV_W, tf), lambda j: (0, j)),
            pl.BlockSpec((CONV_W, tf), lambda j: (0, nj + j)),
            pl.BlockSpec((1, tf), lambda j: (0, j)),
            pl.BlockSpec((1, tf), lambda j: (0, nj + j)),
            pl.BlockSpec((tf, D_MODEL), lambda j: (j, 0)),
            pl.BlockSpec((nseq, CONV_W - 1, tf), lambda j: (0, 0, j)),
            pl.BlockSpec((nseq, CONV_W - 1, tf), lambda j: (0, 0, nj + j)),
            _const_spec((n_tail, D_MODEL)),
        ],
        out_specs=[
            pl.BlockSpec((nseq, D_MODEL), lambda j: (0, 0)),
            _const_spec((nseq, CONV_W - 1, 2 * D_FF)),
            pl.BlockSpec((SUBLANES, tf), lambda j: (0, j)),
            pl.BlockSpec((SUBLANES, tf), lambda j: (0, j)),
        ],
        out_shape=[
            jax.ShapeDtypeStruct((nseq, D_MODEL), F32),
            jax.ShapeDtypeStruct((nseq, CONV_W - 1, 2 * D_FF), F32),
            jax.ShapeDtypeStruct((SUBLANES, D_FF), F32),
            jax.ShapeDtypeStruct((SUBLANES, D_FF), F32),
        ],
        scratch_shapes=[pltpu.VMEM((nseq, D_MODEL), BF16), pltpu.VMEM((n_tail, D_MODEL), BF16)],
        compiler_params=pltpu.CompilerParams(
            dimension_semantics=("arbitrary",), vmem_limit_bytes=VMEM_LIMIT),
        name="ffn_step",
    )(x1, g, wup_bf, wup_bf, conv_w, conv_w, conv_b, conv_b, wdown, buf, buf, x_tail)


def _rope_tables(pos):
    half = HEAD_DIM // 2
    inv = ROPE_THETA ** (-jnp.arange(half, dtype=F32) * 2.0 / HEAD_DIM)
    ang = pos.astype(F32)[:, None] * inv[None, :]
    c = jnp.cos(ang)
    s = jnp.sin(ang)
    return jnp.concatenate([c, c, c, c], axis=1), jnp.concatenate([-s, s, -s, s], axis=1)


def _block_diag(w):
    nd, ng, r, c = w.shape
    tiled = jnp.tile(w.reshape(nd, ng * r, c), (1, 1, ng))
    on_diag = (jnp.arange(ng * r)[:, None] // r) == (jnp.arange(ng * c)[None, :] // c)
    return jnp.where(on_diag[None], tiled, jnp.zeros_like(tiled))


def kernel(x_prompt, x_sample, cache_k, cache_v, state_ssm_re, state_ssm_im, state_conv, meta_tokens,
           norm_mix_g, w_in, q_norm_g, k_norm_g, attn_sinks, lam_re, lam_im, log_dt, ssm_b_re, ssm_b_im,
           ssm_c_re, ssm_c_im, ssm_d, w_glu, b_glu, ssm_out_g, attn_out_g, w_out, norm_ffn_g, w_up,
           conv_w, conv_b, w_down):
    depth = w_in.shape[0]
    assert depth == 1
    nb, seq, _ = x_prompt.shape
    nseq = x_sample.shape[0]
    n_meta = meta_tokens.shape[0]
    assert x_sample.shape[1] == 1 and n_meta == 16 and nb == SUBLANES
    assert seq % 1024 == 0 and nseq % SUBLANES == 0
    l = 0
    gpd = SSM_GROUPS_PER_DOT
    n_dots = N_SSM_GROUPS // gpd

    w_in_bf = w_in[l].astype(BF16)
    w_glu_bf = w_glu[l].astype(BF16)
    w_out_bf = w_out[l].astype(BF16)
    w_up_bf = w_up[l].astype(BF16)
    w_down_f = w_down[l]
    g_mix = norm_mix_g[l][None, :]
    g_ffn = norm_ffn_g[l][None, :]
    qg = jnp.tile(q_norm_g[l], 256 // HEAD_DIM)[None, :]
    kg = jnp.tile(k_norm_g[l], 256 // HEAD_DIM)[None, :]
    lane_head = jnp.arange(256) // HEAD_DIM
    ones_bd = (lane_head[:, None] == lane_head[None, :]).astype(BF16)
    sinks = attn_sinks[l]
    lamre = lam_re[l].reshape(1, D_STATE)
    lamim = lam_im[l].reshape(1, D_STATE)
    logdt = jnp.repeat(log_dt[l], SSM_STATE)[None, :]
    b_re_t = jnp.swapaxes(ssm_b_re[l], 1, 2).reshape(n_dots, gpd, SSM_GROUP, SSM_STATE)
    b_im_t = jnp.swapaxes(ssm_b_im[l], 1, 2).reshape(n_dots, gpd, SSM_GROUP, SSM_STATE)
    bblk = jnp.stack([b_re_t, b_im_t]).reshape(2, n_dots, gpd * SSM_GROUP, SSM_STATE)
    cre = _block_diag(jnp.swapaxes(ssm_c_re[l], 1, 2).reshape(n_dots, gpd, SSM_STATE, SSM_GROUP)).astype(BF16)
    cim = _block_diag(jnp.swapaxes(ssm_c_im[l], 1, 2).reshape(n_dots, gpd, SSM_STATE, SSM_GROUP)).astype(BF16)
    dskip = ssm_d[l].reshape(1, D_SSM)
    bglu = b_glu[l][None, :]
    g_ssm = ssm_out_g[l][None, :]
    g_att = attn_out_g[l][None, :]
    cw = conv_w[l]
    cb = conv_b[l][None, :]

    ssm_w = (lamre, lamim, logdt, bblk, cre, cim, dskip, w_glu_bf, bglu, g_ssm)

    pad = BLOCK - n_meta
    xm = jnp.concatenate([jnp.zeros((pad, D_MODEL), F32), meta_tokens.astype(F32)], axis=0)[None]
    cos_m, sin_m = _rope_tables(jnp.maximum(jnp.arange(BLOCK) - pad, 0))
    u_m, q_m, k_m, v_m = _in_proj(xm, g_mix, w_in_bf, cos_m, sin_m, qg, kg, ones_bd, tq=BLOCK)
    u_m8 = jnp.broadcast_to(u_m[pad:], (n_meta, SUBLANES, D_SSM)).reshape(n_meta * SUBLANES, D_SSM)
    zero_state = jnp.zeros((SUBLANES, D_STATE), F32)
    ys_m8, hre_m, him_m = _ssm(u_m8, zero_state, zero_state, *ssm_w, nbatch=SUBLANES, tt=n_meta)
    ys_m = jnp.concatenate([jnp.zeros((1, pad, D_SSM), BF16), ys_m8[0:1]], axis=1)
    zero_kv = jnp.zeros((1, BLOCK, D_KV), F32)
    ya_m = _attn_seq(sinks, q_m, k_m, v_m, zero_kv, zero_kv, g_att, n_meta=n_meta, first_block=0)
    x1_m = _out_proj(xm, ys_m, ya_m, w_out_bf, tm=BLOCK)

    xs = x_sample.reshape(1, nseq, D_MODEL)
    cos_s, sin_s = _rope_tables(jnp.full((nseq,), PAST_LEN, jnp.int32))
    u_s, q_s, k_s, v_s = _in_proj(xs, g_mix, w_in_bf, cos_s, sin_s, qg, kg, ones_bd, tq=nseq)
    ys_s, hre_s, him_s = _ssm(u_s.reshape(nseq, D_SSM), state_ssm_re[l].reshape(nseq, D_STATE),
                              state_ssm_im[l].reshape(nseq, D_STATE), *ssm_w, nbatch=nseq, tt=1)
    ya_s, nk_s, nv_s = _attn_step(
        q_s.reshape(nseq, N_HEADS, HEAD_DIM), k_s.reshape(nseq * N_KV_HEADS, HEAD_DIM),
        v_s.reshape(nseq * N_KV_HEADS, HEAD_DIM),
        jnp.transpose(cache_k[l], (0, 2, 3, 1)), jnp.transpose(cache_v[l], (0, 2, 3, 1)),
        sinks[:, None], attn_out_g[l].reshape(N_HEADS, HEAD_DIM), bb=8)
    x1_s = _out_proj(xs, ys_s, ya_s.reshape(1, nseq, D_ATTN), w_out_bf, tm=nseq)
    y_s, nbuf_s, bufv_m, bufg_m = _ffn_step(x1_s[0], g_ffn, w_up_bf, cw, cb, w_down_f, state_conv[l],
                                                  x1_m[0, BLOCK - 2 * SUBLANES:])

    y_sample = y_s.reshape(nseq, 1, D_MODEL)
    sample_k = jnp.transpose(nk_s, (0, 3, 1, 2))[None]
    sample_v = jnp.transpose(nv_s, (0, 3, 1, 2))[None]
    sample_re = hre_s.reshape(nseq, N_SSM_GROUPS, SSM_STATE)[None]
    sample_im = him_s.reshape(nseq, N_SSM_GROUPS, SSM_STATE)[None]
    sample_conv = nbuf_s[None]

    cos_p, sin_p = _rope_tables(n_meta + jnp.arange(seq))
    u_p, q_p, k_p, v_p = _in_proj(x_prompt, g_mix, w_in_bf, cos_p, sin_p, qg, kg, ones_bd, tq=128)
    ys_p, hre_p, him_p = _ssm(u_p.reshape(seq * nb, D_SSM), hre_m, him_m, *ssm_w, nbatch=nb, tt=SCAN_STEPS)
    k0 = jnp.broadcast_to(k_m, (nb, BLOCK, D_KV))
    v0 = jnp.broadcast_to(v_m, (nb, BLOCK, D_KV))
    ya_p = _attn_seq(sinks, q_p, k_p, v_p, k0, v0, g_att, n_meta=n_meta, first_block=1)
    x1_p = _out_proj(x_prompt, ys_p, ya_p, w_out_bf, tm=512)
    bufv0 = jnp.broadcast_to(bufv_m, (nb, SUBLANES, D_FF))
    bufg0 = jnp.broadcast_to(bufg_m, (nb, SUBLANES, D_FF))
    y_p, bufv_p, bufg_p = _ffn_seq(x1_p.reshape(nb * seq, D_MODEL), g_ffn, w_up_bf, cw, cb, w_down_f,
                                   bufv0, bufg0, nb=nb, tm=1024)

    y_prompt = y_p.reshape(nb, seq, D_MODEL)
    prompt_k = k_p[:, seq - WINDOW:].reshape(nb, WINDOW, N_KV_HEADS, HEAD_DIM)[None]
    prompt_v = v_p[:, seq - WINDOW:].reshape(nb, WINDOW, N_KV_HEADS, HEAD_DIM)[None]
    prompt_re = hre_p.reshape(nb, N_SSM_GROUPS, SSM_STATE)[None]
    prompt_im = him_p.reshape(nb, N_SSM_GROUPS, SSM_STATE)[None]
    prompt_conv = jnp.concatenate([bufv_p[:, SUBLANES - 2:], bufg_p[:, SUBLANES - 2:]], axis=-1)[None]

    return (y_prompt, y_sample, prompt_k, prompt_v, prompt_re, prompt_im, prompt_conv,
            sample_k, sample_v, sample_re, sample_im, sample_conv)
```

```python
import functools

import jax
import jax.numpy as jnp
from jax import lax
from jax.experimental import pallas as pl
from jax.experimental.pallas import tpu as pltpu

F32 = jnp.float32
BF16 = jnp.bfloat16

D_MODEL = 2048
D_SSM = 1024
SSM_GROUP = 16
N_SSM_GROUPS = 64
SSM_STATE = 64
D_STATE = N_SSM_GROUPS * SSM_STATE
HEAD_DIM = 64
N_HEADS = 16
N_KV_HEADS = 4
Q_PER_KV = 4
D_ATTN = 1024
D_KV = 256
D_IN = D_SSM + D_ATTN + 2 * D_KV
WINDOW = 128
BLOCK = 128
ROPE_THETA = 10000.0
D_FF = 5632
CONV_W = 3
EPS = 1e-6
NEG = -1e30
PAST_LEN = 16384

SUBLANES = 8
LANES = 128
VMEM_LIMIT = 62 * 1024 * 1024
FF_TILE = 512
FFN_SUB_ROWS = 512
IN_SUB_ROWS = 256
ATTN_Q_BLOCKS = 4
SCAN_LANES = 512
SCAN_STEPS = 32
SSM_GROUPS_PER_DOT = 16


def _const_spec(shape):
    nd = len(shape)
    return pl.BlockSpec(shape, lambda *_: (0,) * nd, pipeline_mode=pl.Buffered(1))


def _rms(x, g):
    return x * lax.rsqrt(jnp.mean(x * x, axis=-1, keepdims=True) + EPS) * g


def _mm(a, b):
    return jnp.dot(a, b, preferred_element_type=F32)


def _in_proj_kernel(x_ref, g_ref, w_ref, cos_ref, sin_ref, qg_ref, kg_ref, ones_ref,
                    u_ref, q_ref, k_ref, v_ref, *, nb, tq, n_split):
    rs = nb * tq // n_split
    cos1 = jnp.concatenate([cos_ref[...], cos_ref[...]], axis=1)
    sin1 = jnp.concatenate([sin_ref[...], sin_ref[...]], axis=1)
    ones = ones_ref[...]
    n_q = D_ATTN // 256
    for s in range(n_split):
        segs = []
        for b in range(nb):
            lo = max(s * rs, b * tq)
            hi = min((s + 1) * rs, (b + 1) * tq)
            if lo < hi:
                segs.append((b, lo - b * tq, hi - lo, lo - s * rs))
        x = jnp.concatenate([x_ref[b, t0:t0 + n, :] for b, t0, n, _ in segs], axis=0)
        cos = jnp.concatenate([cos1[t0:t0 + n, :] for _, t0, n, _ in segs], axis=0)
        sin = jnp.concatenate([sin1[t0:t0 + n, :] for _, t0, n, _ in segs], axis=0)
        lane = lax.broadcasted_iota(jnp.int32, cos.shape, 1)
        first_half = (lane % HEAD_DIM) < (HEAD_DIM // 2)
        h = _rms(x, g_ref[...]).astype(BF16)
        z = _mm(h, w_ref[...])
        for b, t0, n, r0 in segs:
            u_ref[t0:t0 + n, b, :] = z[r0:r0 + n, :D_SSM]
            v_ref[b, t0:t0 + n, :] = z[r0:r0 + n, D_SSM + D_ATTN + D_KV:]
        for c in range(n_q + D_KV // 256):
            zc = z[:, D_SSM + 256 * c: D_SSM + 256 * (c + 1)]
            sq = zc * zc
            hi = sq.astype(BF16)
            lo = (sq - hi.astype(F32)).astype(BF16)
            ssum = _mm(hi, ones) + _mm(lo, ones)
            gain = qg_ref[...] if c < n_q else kg_ref[...]
            nrm = zc * lax.rsqrt(ssum * (1.0 / HEAD_DIM) + EPS) * gain
            partner = jnp.where(first_half, pltpu.roll(nrm, 256 - HEAD_DIM // 2, 1),
                                pltpu.roll(nrm, HEAD_DIM // 2, 1))
            r = nrm * cos + partner * sin
            for b, t0, n, r0 in segs:
                if c < n_q:
                    q_ref[b, t0:t0 + n, 256 * c: 256 * (c + 1)] = (
                        r[r0:r0 + n, :] * (HEAD_DIM ** -0.5)).astype(BF16)
                else:
                    k_ref[b, t0:t0 + n, :] = r[r0:r0 + n, :]


def _in_proj(x3, g, w_bf, cos, sin, qg, kg, ones_bd, *, tq, tab0):
    nb, t_len, _ = x3.shape
    blk = lambda t: (0, t, 0)
    kern = functools.partial(_in_proj_kernel, nb=nb, tq=tq, n_split=max(1, nb * tq // IN_SUB_ROWS))
    return pl.pallas_call(
        kern,
        grid=(t_len // tq,),
        in_specs=[
            pl.BlockSpec((nb, tq, D_MODEL), blk),
            _const_spec((1, D_MODEL)),
            _const_spec((D_MODEL, D_IN)),
            pl.BlockSpec((tq, LANES), lambda t: (tab0 + t, 0)),
            pl.BlockSpec((tq, LANES), lambda t: (tab0 + t, 0)),
            _const_spec((1, 256)),
            _const_spec((1, 256)),
            _const_spec((256, 256)),
        ],
        out_specs=[
            pl.BlockSpec((tq, nb, D_SSM), lambda t: (t, 0, 0)),
            pl.BlockSpec((nb, tq, D_ATTN), blk),
            pl.BlockSpec((nb, tq, D_KV), blk),
            pl.BlockSpec((nb, tq, D_KV), blk),
        ],
        out_shape=[
            jax.ShapeDtypeStruct((t_len, nb, D_SSM), F32),
            jax.ShapeDtypeStruct((nb, t_len, D_ATTN), BF16),
            jax.ShapeDtypeStruct((nb, t_len, D_KV), F32),
            jax.ShapeDtypeStruct((nb, t_len, D_KV), F32),
        ],
        compiler_params=pltpu.CompilerParams(
            dimension_semantics=("arbitrary",), vmem_limit_bytes=VMEM_LIMIT),
        name="in_proj",
    )(x3, g, w_bf, cos, sin, qg, kg, ones_bd)


def _ssm_kernel(u_ref, h0re_ref, h0im_ref, lamre_ref, lamim_ref, logdt_ref, bblk_ref,
                cre_ref, cim_ref, dskip_ref, wglu_ref, bglu_ref, g_ref, perm_ref,
                y_ref, sre_ref, sim_ref,
                are_s, aim_s, bbar_s, xre_s, xim_s, stre_s, stim_s, *, tt, nbg):
    step = pl.program_id(0)
    lanes_per_dot = SSM_GROUPS_PER_DOT * SSM_STATE
    chans_per_dot = SSM_GROUPS_PER_DOT * SSM_GROUP
    n_dots = N_SSM_GROUPS // SSM_GROUPS_PER_DOT
    state_lanes = [slice(lanes_per_dot * c, lanes_per_dot * (c + 1)) for c in range(n_dots)]
    chans = [slice(chans_per_dot * c, chans_per_dot * (c + 1)) for c in range(n_dots)]

    @pl.when(step == 0)
    def _():
        lr = lamre_ref[...]
        li = lamim_ref[...]
        dt = jnp.exp(logdt_ref[...])
        mag = jnp.exp(lr * dt)
        a_re = mag * jnp.cos(li * dt)
        a_im = mag * jnp.sin(li * dt)
        den = lr * lr + li * li
        am1 = a_re - 1.0
        cf_re = (am1 * lr + a_im * li) / den
        cf_im = (a_im * lr - am1 * li) / den
        shape = (SUBLANES, D_STATE)
        are_s[...] = jnp.broadcast_to(a_re, shape)
        aim_s[...] = jnp.broadcast_to(a_im, shape)
        row_grp = lax.broadcasted_iota(jnp.int32, (chans_per_dot, lanes_per_dot), 0) // SSM_GROUP
        lane_grp = lax.broadcasted_iota(jnp.int32, (chans_per_dot, lanes_per_dot), 1) // SSM_STATE
        on_diag = row_grp == lane_grp

        def expand(b):
            return jnp.where(on_diag, jnp.concatenate([b] * SSM_GROUPS_PER_DOT, axis=1), 0.0)

        for c in range(n_dots):
            b_re = expand(bblk_ref[0, c])
            b_im = expand(bblk_ref[1, c])
            c_re = cf_re[:, state_lanes[c]]
            c_im = cf_im[:, state_lanes[c]]
            bbar_s[c, :, :lanes_per_dot] = (c_re * b_re - c_im * b_im).astype(BF16)
            bbar_s[c, :, lanes_per_dot:] = (c_re * b_im + c_im * b_re).astype(BF16)
        stre_s[...] = h0re_ref[...]
        stim_s[...] = h0im_ref[...]

    u = u_ref[...]
    ub = u.astype(BF16)

    def project(c):
        bu = _mm(ub[:, chans[c]], bbar_s[c])
        xre_s[:, state_lanes[c]] = bu[:, :lanes_per_dot]
        xim_s[:, state_lanes[c]] = bu[:, lanes_per_dot:]

    def scan(c):
        blocks = [slice(lanes_per_dot * c + SCAN_LANES * i, lanes_per_dot * c + SCAN_LANES * (i + 1))
                  for i in range(lanes_per_dot // SCAN_LANES)]
        for bg in range(nbg):
            rs = slice(SUBLANES * bg, SUBLANES * (bg + 1))
            a = [(are_s[:, ls], aim_s[:, ls]) for ls in blocks]
            st = [(stre_s[rs, ls], stim_s[rs, ls]) for ls in blocks]
            for t in range(tt):
                rt = slice((t * nbg + bg) * SUBLANES, (t * nbg + bg + 1) * SUBLANES)
                for i, ls in enumerate(blocks):
                    (a_re, a_im), (s_re, s_im) = a[i], st[i]
                    n_re = a_re * s_re - a_im * s_im + xre_s[rt, ls]
                    n_im = a_re * s_im + a_im * s_re + xim_s[rt, ls]
                    xre_s[rt, ls] = n_re
                    xim_s[rt, ls] = n_im
                    st[i] = (n_re, n_im)
            for i, ls in enumerate(blocks):
                stre_s[rs, ls] = st[i][0]
                stim_s[rs, ls] = st[i][1]

    def readout(c):
        yc = (_mm(xre_s[:, state_lanes[c]].astype(BF16), cre_ref[c])
              - _mm(xim_s[:, state_lanes[c]].astype(BF16), cim_ref[c]))
        return jax.nn.gelu(yc + dskip_ref[:, chans[c]] * u[:, chans[c]])

    ys = [None] * n_dots
    project(0)
    for c in range(n_dots):
        if c + 1 < n_dots:
            project(c + 1)
        scan(c)
        if c > 0:
            ys[c - 1] = readout(c - 1)
    ys[n_dots - 1] = readout(n_dots - 1)
    y = jnp.concatenate(ys, axis=1)
    y = y * jax.nn.sigmoid(_mm(y.astype(BF16), wglu_ref[...]) + bglu_ref[...])
    yn = _rms(y, g_ref[...]).astype(BF16)
    if tt == 1:
        y_ref[0] = yn
    else:
        y_ref[...] = _mm(perm_ref[...], yn).astype(BF16).reshape(y_ref.shape)

    @pl.when(step == pl.num_programs(0) - 1)
    def _():
        sre_ref[...] = stre_s[...]
        sim_ref[...] = stim_s[...]


def _ssm(u_tm, h0re, h0im, lamre, lamim, logdt, bblk, cre, cim, dskip, wglu, bglu, g, *, nbatch, tt):
    n = u_tm.shape[0]
    t_len = n // nbatch
    nbg = nbatch // SUBLANES
    rows = tt * nbatch
    n_grp = nbatch if tt > 1 else 1
    r = jnp.arange(rows)
    src = (r % tt) * nbatch + r // tt
    perm = (src[:, None] == r[None, :]).astype(BF16)
    kern = functools.partial(_ssm_kernel, tt=tt, nbg=nbg)
    return pl.pallas_call(
        kern,
        grid=(t_len // tt,),
        in_specs=[
            pl.BlockSpec((rows, D_SSM), lambda s: (s, 0)),
            _const_spec((nbatch, D_STATE)),
            _const_spec((nbatch, D_STATE)),
            _const_spec((1, D_STATE)),
            _const_spec((1, D_STATE)),
            _const_spec((1, D_STATE)),
            _const_spec(bblk.shape),
            _const_spec(cre.shape),
            _const_spec(cim.shape),
            _const_spec((1, D_SSM)),
            _const_spec((D_SSM, D_SSM)),
            _const_spec((1, D_SSM)),
            _const_spec((1, D_SSM)),
            _const_spec((rows, rows)),
        ],
        out_specs=[
            pl.BlockSpec((n_grp, rows // n_grp, D_SSM), lambda s: (0, s, 0)),
            pl.BlockSpec((nbatch, D_STATE), lambda s: (0, 0)),
            pl.BlockSpec((nbatch, D_STATE), lambda s: (0, 0)),
        ],
        out_shape=[
            jax.ShapeDtypeStruct((n_grp, n // n_grp, D_SSM), BF16),
            jax.ShapeDtypeStruct((nbatch, D_STATE), F32),
            jax.ShapeDtypeStruct((nbatch, D_STATE), F32),
        ],
        scratch_shapes=[pltpu.VMEM((SUBLANES, D_STATE), F32)] * 2
        + [pltpu.VMEM((bblk.shape[1], bblk.shape[2], 2 * SSM_GROUPS_PER_DOT * SSM_STATE), BF16)]
        + [pltpu.VMEM((rows, D_STATE), F32)] * 2
        + [pltpu.VMEM((nbatch, D_STATE), F32)] * 2,
        compiler_params=pltpu.CompilerParams(
            dimension_semantics=("arbitrary",), vmem_limit_bytes=VMEM_LIMIT),
        name="ssm",
    )(u_tm, h0re, h0im, lamre, lamim, logdt, bblk, cre, cim, dskip, wglu, bglu, g, perm)


def _attn_seq_kernel(sink_ref, q_ref, kc_ref, kp_ref, k0_ref, vc_ref, vp_ref, v0_ref, g_ref, o_ref,
                     *, first_block, n_meta, qb):
    i = pl.program_id(1)
    is_first = i == 0
    r = lax.broadcasted_iota(jnp.int32, (BLOCK, 4 * BLOCK), 0)
    c = lax.broadcasted_iota(jnp.int32, (BLOCK, 4 * BLOCK), 1) % (2 * BLOCK)
    in_window = (c > r) & (c <= r + WINDOW)
    lo_half = lax.broadcasted_iota(jnp.int32, (2 * BLOCK, LANES), 1) < HEAD_DIM

    def pair_blocks(x2, kv):
        xc = x2[:, LANES * (kv // 2): LANES * (kv // 2 + 1)]
        xr = pltpu.roll(xc, HEAD_DIM, 1)
        zero = jnp.zeros_like(xc)
        if kv % 2 == 0:
            low, high = jnp.where(lo_half, xc, zero), jnp.where(lo_half, zero, xr)
        else:
            low, high = jnp.where(lo_half, xr, zero), jnp.where(lo_half, zero, xc)
        return jnp.concatenate([low, high], axis=0).astype(BF16)

    n_pairs = N_HEADS // 2
    pairs_per_kv = Q_PER_KV // 2
    row_hi = lax.broadcasted_iota(jnp.int32, (4 * BLOCK, LANES), 0) >= 2 * BLOCK
    lane_hi = lax.broadcasted_iota(jnp.int32, (4 * BLOCK, LANES), 1) >= HEAD_DIM
    ones_cols = (row_hi == lane_hi).astype(BF16)
    kbd, vbd, mask, rows = [], [], [], []
    for sb in range(qb):
        rows.append(slice(BLOCK * sb, BLOCK * (sb + 1)))
        if sb == 0:
            k_prev = jnp.where(is_first, k0_ref[0], kp_ref[0])
            v_prev = jnp.where(is_first, v0_ref[0], vp_ref[0])
        else:
            k_prev, v_prev = kc_ref[0, rows[sb - 1], :], vc_ref[0, rows[sb - 1], :]
        k2 = jnp.concatenate([k_prev, kc_ref[0, rows[sb], :]], axis=0)
        v2 = jnp.concatenate([v_prev, vc_ref[0, rows[sb], :]], axis=0)
        gb = i * qb + sb + first_block
        col_min = jnp.where(gb == 0, 2 * BLOCK - n_meta, jnp.where(gb == 1, BLOCK - n_meta, 0))
        mask.append(in_window & (c >= col_min))
        kbd.append([pair_blocks(k2, kv) for kv in range(N_KV_HEADS)])
        vbd.append([jnp.concatenate([pair_blocks(v2, kv), ones_cols], axis=1) for kv in range(N_KV_HEADS)])
    halves = (slice(0, 2 * BLOCK), slice(2 * BLOCK, 4 * BLOCK))
    probs = [(sb, pr) for sb in range(qb) for pr in range(n_pairs)]
    s = {(sb, pr): jnp.where(mask[sb], lax.dot_general(
        q_ref[0, rows[sb], LANES * pr: LANES * (pr + 1)], kbd[sb][pr // pairs_per_kv],
        (((1,), (1,)), ((), ())), preferred_element_type=F32), NEG) for sb, pr in probs}
    m = {(sb, pr): [jnp.maximum(jnp.max(s[sb, pr][:, halves[hh]], axis=-1, keepdims=True), sink_ref[2 * pr + hh])
                    for hh in range(2)] for sb, pr in probs}
    p = {(sb, pr): jnp.concatenate([jnp.exp(s[sb, pr][:, halves[hh]] - m[sb, pr][hh]) for hh in range(2)],
                                   axis=1).astype(BF16) for sb, pr in probs}
    out_hi = lax.broadcasted_iota(jnp.int32, (BLOCK, LANES), 1) >= HEAD_DIM
    outs = {}
    for sb, pr in probs:
        ov = _mm(p[sb, pr], vbd[sb][pr // pairs_per_kv])
        sink_w = jnp.where(out_hi, jnp.exp(sink_ref[2 * pr + 1] - m[sb, pr][1]),
                           jnp.exp(sink_ref[2 * pr] - m[sb, pr][0]))
        outs[sb, pr] = ov[:, :LANES] / (ov[:, LANES:] + sink_w)
    for sb in range(qb):
        o = jnp.concatenate([outs[sb, pr] for pr in range(n_pairs)], axis=1)
        o_ref[0, rows[sb], :] = _rms(o, g_ref[...]).astype(BF16)


def _attn_seq(sinks, q, k, v, k0, v0, g, *, n_meta, first_block):
    nb, t_len, _ = q.shape
    nblk = t_len // BLOCK
    qb = ATTN_Q_BLOCKS if nblk % ATTN_Q_BLOCKS == 0 else 1
    cur = lambda b, i: (b, i, 0)
    prev = lambda b, i: (b, jnp.maximum(i * qb - 1, 0), 0)
    init = lambda b, i: (0, 0, 0)
    kern = functools.partial(_attn_seq_kernel, first_block=first_block, n_meta=n_meta, qb=qb)
    return pl.pallas_call(
        kern,
        grid=(nb, nblk // qb),
        in_specs=[
            pl.BlockSpec(memory_space=pltpu.SMEM),
            pl.BlockSpec((1, qb * BLOCK, D_ATTN), cur),
            pl.BlockSpec((1, qb * BLOCK, D_KV), cur),
            pl.BlockSpec((1, BLOCK, D_KV), prev),
            pl.BlockSpec((1, BLOCK, D_KV), init),
            pl.BlockSpec((1, qb * BLOCK, D_KV), cur),
            pl.BlockSpec((1, BLOCK, D_KV), prev),
            pl.BlockSpec((1, BLOCK, D_KV), init),
            _const_spec((1, D_ATTN)),
        ],
        out_specs=pl.BlockSpec((1, qb * BLOCK, D_ATTN), cur),
        out_shape=jax.ShapeDtypeStruct((nb, t_len, D_ATTN), BF16),
        compiler_params=pltpu.CompilerParams(
            dimension_semantics=("arbitrary", "arbitrary"), vmem_limit_bytes=VMEM_LIMIT),
        name="attn_seq",
    )(sinks, q, k, k, k0, v, v, v0, g)


def _attn_step_kernel(q_ref, kn_ref, vn_ref, ck_ref, cv_ref, sink_ref, g_ref, o_ref, ok_ref, ov_ref):
    is_last = lax.broadcasted_iota(jnp.int32, (1, WINDOW), 1) == WINDOW - 1
    bb = ck_ref.shape[0]

    def shift_in(cache_ref, new_ref, out_ref):
        new = new_ref[...]
        new = jnp.concatenate([new, jnp.zeros((new.shape[0], LANES - HEAD_DIM), F32)], axis=1)
        new = jnp.concatenate([new, jnp.zeros((LANES - new.shape[0], LANES), F32)], axis=0)
        cols = new.T
        for b in range(bb):
            for kv in range(N_KV_HEADS):
                j = b * N_KV_HEADS + kv
                out_ref[b, kv] = jnp.where(is_last, cols[0:HEAD_DIM, j:j + 1],
                                           pltpu.roll(cache_ref[b, kv], WINDOW - 1, 1))

    shift_in(ck_ref, kn_ref, ok_ref)
    shift_in(cv_ref, vn_ref, ov_ref)
    k_t = ok_ref[...]
    v_t = ov_ref[...]
    q = q_ref[...]
    kv_of_head = lax.broadcasted_iota(jnp.int32, (1, N_HEADS, 1), 1) // Q_PER_KV
    s = None
    for kv in range(N_KV_HEADS):
        s_kv = jnp.einsum('bhd,bdw->bhw', q, k_t[:, kv].astype(BF16), preferred_element_type=F32)
        s = s_kv if s is None else jnp.where(kv_of_head == kv, s_kv, s)
    sink = sink_ref[...][None]
    m = jnp.maximum(jnp.max(s, axis=-1, keepdims=True), sink)
    p = jnp.exp(s - m)
    den = jnp.sum(p, axis=-1, keepdims=True) + jnp.exp(sink - m)
    pb = (p / den).astype(BF16)
    o = None
    for kv in range(N_KV_HEADS):
        o_kv = jnp.einsum('bhw,bdw->bhd', pb, v_t[:, kv].astype(BF16), preferred_element_type=F32)
        o = o_kv if o is None else jnp.where(kv_of_head == kv, o_kv, o)
    ms = jnp.sum(jnp.sum(o * o, axis=2, keepdims=True), axis=1, keepdims=True) * (1.0 / D_ATTN)
    o_ref[...] = (o * lax.rsqrt(ms + EPS) * g_ref[...][None]).astype(BF16)


def _attn_step(q3, kn, vn, ck, cv, sinks, g, *, bb):
    nseq = q3.shape[0]
    blk3 = lambda i: (i, 0, 0)
    blk4 = lambda i: (i, 0, 0, 0)
    cache_blk = (bb, N_KV_HEADS, HEAD_DIM, WINDOW)
    new_spec = pl.BlockSpec((bb * N_KV_HEADS, HEAD_DIM), lambda i: (i, 0))
    return pl.pallas_call(
        _attn_step_kernel,
        grid=(nseq // bb,),
        in_specs=[
            pl.BlockSpec((bb, N_HEADS, HEAD_DIM), blk3),
            new_spec,
            new_spec,
            pl.BlockSpec(cache_blk, blk4),
            pl.BlockSpec(cache_blk, blk4),
            _const_spec((N_HEADS, 1)),
            _const_spec((N_HEADS, HEAD_DIM)),
        ],
        out_specs=[
            pl.BlockSpec((bb, N_HEADS, HEAD_DIM), blk3),
            pl.BlockSpec(cache_blk, blk4),
            pl.BlockSpec(cache_blk, blk4),
        ],
        out_shape=[
            jax.ShapeDtypeStruct((nseq, N_HEADS, HEAD_DIM), BF16),
            jax.ShapeDtypeStruct((nseq,) + cache_blk[1:], F32),
            jax.ShapeDtypeStruct((nseq,) + cache_blk[1:], F32),
        ],
        compiler_params=pltpu.CompilerParams(
            dimension_semantics=("arbitrary",), vmem_limit_bytes=VMEM_LIMIT),
        name="attn_step",
    )(q3, kn, vn, ck, cv, sinks, g)


def _out_proj_kernel(x_ref, ys_ref, ya_ref, w_ref, o_ref):
    o_ref[0] = (x_ref[0] + _mm(ys_ref[0], w_ref[0:D_SSM, :])
                + _mm(ya_ref[0], w_ref[D_SSM:D_SSM + D_ATTN, :]))


def _out_proj(x3, ys, ya, w_bf, *, tm):
    nb, t_len, _ = x3.shape
    blk = lambda b, t: (b, t, 0)
    return pl.pallas_call(
        _out_proj_kernel,
        grid=(nb, t_len // tm),
        in_specs=[
            pl.BlockSpec((1, tm, D_MODEL), blk),
            pl.BlockSpec((1, tm, D_SSM), blk),
            pl.BlockSpec((1, tm, D_ATTN), blk),
            _const_spec((D_SSM + D_ATTN, D_MODEL)),
        ],
        out_specs=pl.BlockSpec((1, tm, D_MODEL), blk),
        out_shape=jax.ShapeDtypeStruct((nb, t_len, D_MODEL), F32),
        compiler_params=pltpu.CompilerParams(
            dimension_semantics=("arbitrary", "arbitrary"), vmem_limit_bytes=VMEM_LIMIT),
        name="out_proj",
    )(x3, ys, ya, w_bf)


def _ffn_seq_kernel(x_ref, g_ref, wv_ref, wg_ref, cwv_ref, cwg_ref, cbv_ref, cbg_ref, wd_ref,
                    bufv_ref, bufg_ref, y_ref, nbv_ref, nbg_ref,
                    h_s, upv_s, upg_s, *, tm, tf):
    t = pl.program_id(1)
    j = pl.program_id(2)
    chan = pl.ds(pl.multiple_of(j * tf, tf), tf)

    @pl.when(j == 0)
    def _():
        x = x_ref[...]
        h_s[...] = _rms(x, g_ref[...]).astype(BF16)
        y_ref[...] = x

    @pl.when(t == 0)
    def _():
        upv_s[0:SUBLANES, :] = bufv_ref[0]
        upg_s[0:SUBLANES, :] = bufg_ref[0]

    @pl.when(t > 0)
    def _():
        upv_s[0:SUBLANES, :] = nbv_ref[0, :, chan]
        upg_s[0:SUBLANES, :] = nbg_ref[0, :, chan]

    rm = min(tm, FFN_SUB_ROWS)
    n_sub = tm // rm

    def up_proj(up_s, w_ref, r0):
        up_s[SUBLANES + r0:SUBLANES + r0 + rm, :] = _mm(h_s[r0:r0 + rm, :], w_ref[...])

    def conv(up_s, cw_ref, cb_ref, r0):
        return (cb_ref[...] + cw_ref[0:1, :] * up_s[SUBLANES - 2 + r0:SUBLANES - 2 + r0 + rm, :]
                + cw_ref[1:2, :] * up_s[SUBLANES - 1 + r0:SUBLANES - 1 + r0 + rm, :]
                + cw_ref[2:3, :] * up_s[SUBLANES + r0:SUBLANES + r0 + rm, :])

    def gate_down(r0):
        val = conv(upv_s, cwv_ref, cbv_ref, r0)
        gate = conv(upg_s, cwg_ref, cbg_ref, r0)
        act = (jax.nn.silu(gate) * val).astype(BF16)
        y_ref[r0:r0 + rm, :] += _mm(act, wd)

    wd = wd_ref[...].astype(BF16)
    for r in range(n_sub):
        up_proj(upv_s, wv_ref, r * rm)
        up_proj(upg_s, wg_ref, r * rm)
        if r > 0:
            gate_down((r - 1) * rm)
    gate_down((n_sub - 1) * rm)
    nbv_ref[0, :, chan] = upv_s[tm:tm + SUBLANES, :]
    nbg_ref[0, :, chan] = upg_s[tm:tm + SUBLANES, :]


def _ffn_seq(x1, g, wup_bf, conv_w, conv_b, wdown, bufv, bufg, *, nb, tm):
    n = x1.shape[0]
    nt = n // nb // tm
    tf = FF_TILE
    nj = D_FF // tf
    row = lambda b, t, j: (b * nt + t, 0)
    kern = functools.partial(_ffn_seq_kernel, tm=tm, tf=tf)
    return pl.pallas_call(
        kern,
        grid=(nb, nt, nj),
        in_specs=[
            pl.BlockSpec((tm, D_MODEL), row),
            _const_spec((1, D_MODEL)),
            pl.BlockSpec((D_MODEL, tf), lambda b, t, j: (0, j)),
            pl.BlockSpec((D_MODEL, tf), lambda b, t, j: (0, nj + j)),
            pl.BlockSpec((CONV_W, tf), lambda b, t, j: (0, j)),
            pl.BlockSpec((CONV_W, tf), lambda b, t, j: (0, nj + j)),
            pl.BlockSpec((1, tf), lambda b, t, j: (0, j)),
            pl.BlockSpec((1, tf), lambda b, t, j: (0, nj + j)),
            pl.BlockSpec((tf, D_MODEL), lambda b, t, j: (j, 0)),
            pl.BlockSpec((1, SUBLANES, tf), lambda b, t, j: (0, 0, j)),
            pl.BlockSpec((1, SUBLANES, tf), lambda b, t, j: (0, 0, j)),
        ],
        out_specs=[
            pl.BlockSpec((tm, D_MODEL), row),
            pl.BlockSpec((1, SUBLANES, D_FF), lambda b, t, j: (b, 0, 0)),
            pl.BlockSpec((1, SUBLANES, D_FF), lambda b, t, j: (b, 0, 0)),
        ],
        out_shape=[
            jax.ShapeDtypeStruct((n, D_MODEL), F32),
            jax.ShapeDtypeStruct((nb, SUBLANES, D_FF), F32),
            jax.ShapeDtypeStruct((nb, SUBLANES, D_FF), F32),
        ],
        scratch_shapes=[
            pltpu.VMEM((tm, D_MODEL), BF16),
            pltpu.VMEM((tm + SUBLANES, tf), F32),
            pltpu.VMEM((tm + SUBLANES, tf), F32),
        ],
        compiler_params=pltpu.CompilerParams(
            dimension_semantics=("arbitrary", "arbitrary", "arbitrary"), vmem_limit_bytes=VMEM_LIMIT),
        name="ffn_seq",
    )(x1, g, wup_bf, wup_bf, conv_w, conv_w, conv_b, conv_b, wdown, bufv, bufg)


def _ffn_step_kernel(x_ref, g_ref, wv_ref, wg_ref, cwv_ref, cwg_ref, cbv_ref, cbg_ref, wd_ref,
                     bufv_ref, bufg_ref, xt_ref, y_ref, nbuf_ref, tv_ref, tg_ref, h_s, ht_s, *, tf):
    j = pl.program_id(0)

    @pl.when(j == 0)
    def _():
        x = x_ref[...]
        h_s[...] = _rms(x, g_ref[...]).astype(BF16)
        ht_s[...] = _rms(xt_ref[...], g_ref[...]).astype(BF16)
        y_ref[...] = x

    n_tail = ht_s.shape[0]
    tv_ref[...] = _mm(ht_s[...], wv_ref[...])[n_tail - SUBLANES:, :]
    tg_ref[...] = _mm(ht_s[...], wg_ref[...])[n_tail - SUBLANES:, :]
    h = h_s[...]
    upv = _mm(h, wv_ref[...])
    upg = _mm(h, wg_ref[...])
    val = cbv_ref[...] + cwv_ref[0:1, :] * bufv_ref[:, 0, :] + cwv_ref[1:2, :] * bufv_ref[:, 1, :] + cwv_ref[2:3, :] * upv
    gate = cbg_ref[...] + cwg_ref[0:1, :] * bufg_ref[:, 0, :] + cwg_ref[1:2, :] * bufg_ref[:, 1, :] + cwg_ref[2:3, :] * upg
    act = (jax.nn.silu(gate) * val).astype(BF16)
    y_ref[...] += _mm(act, wd_ref[...].astype(BF16))
    val_cols = pl.ds(pl.multiple_of(j * tf, tf), tf)
    gate_cols = pl.ds(pl.multiple_of(D_FF + j * tf, tf), tf)
    nbuf_ref[:, 0, val_cols] = bufv_ref[:, 1, :]
    nbuf_ref[:, 0, gate_cols] = bufg_ref[:, 1, :]
    nbuf_ref[:, 1, val_cols] = upv
    nbuf_ref[:, 1, gate_cols] = upg


def _ffn_step(x1, g, wup_bf, conv_w, conv_b, wdown, buf, x_tail):
    nseq = x1.shape[0]
    n_tail = x_tail.shape[0]
    tf = FF_TILE
    nj = D_FF // tf
    return pl.pallas_call(
        functools.partial(_ffn_step_kernel, tf=tf),
        grid=(nj,),
        in_specs=[
            _const_spec((nseq, D_MODEL)),
            _const_spec((1, D_MODEL)),
            pl.BlockSpec((D_MODEL, tf), lambda j: (0, j)),
            pl.BlockSpec((D_MODEL, tf), lambda j: (0, nj + j)),
            pl.BlockSpec((CONV_W, tf), lambda j: (0, j)),
            pl.BlockSpec((CONV_W, tf), lambda j: (0, nj + j)),
            pl.BlockSpec((1, tf), lambda j: (0, j)),
            pl.BlockSpec((1, tf), lambda j: (0, nj + j)),
            pl.BlockSpec((tf, D_MODEL), lambda j: (j, 0)),
            pl.BlockSpec((nseq, CONV_W - 1, tf), lambda j: (0, 0, j)),
            pl.BlockSpec((nseq, CONV_W - 1, tf), lambda j: (0, 0, nj + j)),
            _const_spec((n_tail, D_MODEL)),
        ],
        out_specs=[
            pl.BlockSpec((nseq, D_MODEL), lambda j: (0, 0)),
            _const_spec((nseq, CONV_W - 1, 2 * D_FF)),
            pl.BlockSpec((SUBLANES, tf), lambda j: (0, j)),
            pl.BlockSpec((SUBLANES, tf), lambda j: (0, j)),
        ],
        out_shape=[
            jax.ShapeDtypeStruct((nseq, D_MODEL), F32),
            jax.ShapeDtypeStruct((nseq, CONV_W - 1, 2 * D_FF), F32),
            jax.ShapeDtypeStruct((SUBLANES, D_FF), F32),
            jax.ShapeDtypeStruct((SUBLANES, D_FF), F32),
        ],
        scratch_shapes=[pltpu.VMEM((nseq, D_MODEL), BF16), pltpu.VMEM((n_tail, D_MODEL), BF16)],
        compiler_params=pltpu.CompilerParams(
            dimension_semantics=("arbitrary",), vmem_limit_bytes=VMEM_LIMIT),
        name="ffn_step",
    )(x1, g, wup_bf, wup_bf, conv_w, conv_w, conv_b, conv_b, wdown, buf, buf, x_tail)


def _rope_tables(pos):
    half = HEAD_DIM // 2
    inv = ROPE_THETA ** (-jnp.arange(half, dtype=F32) * 2.0 / HEAD_DIM)
    ang = pos.astype(F32)[:, None] * inv[None, :]
    c = jnp.cos(ang)
    s = jnp.sin(ang)
    return jnp.concatenate([c, c, c, c], axis=1), jnp.concatenate([-s, s, -s, s], axis=1)


def _block_diag(w):
    nd, ng, r, c = w.shape
    tiled = jnp.tile(w.reshape(nd, ng * r, c), (1, 1, ng))
    on_diag = (jnp.arange(ng * r)[:, None] // r) == (jnp.arange(ng * c)[None, :] // c)
    return jnp.where(on_diag[None], tiled, jnp.zeros_like(tiled))


def kernel(x_prompt, x_sample, cache_k, cache_v, state_ssm_re, state_ssm_im, state_conv, meta_tokens,
           norm_mix_g, w_in, q_norm_g, k_norm_g, attn_sinks, lam_re, lam_im, log_dt, ssm_b_re, ssm_b_im,
           ssm_c_re, ssm_c_im, ssm_d, w_glu, b_glu, ssm_out_g, attn_out_g, w_out, norm_ffn_g, w_up,
           conv_w, conv_b, w_down):
    depth = w_in.shape[0]
    assert depth == 1
    nb, seq, _ = x_prompt.shape
    nseq = x_sample.shape[0]
    n_meta = meta_tokens.shape[0]
    assert x_sample.shape[1] == 1 and n_meta == 16 and nb == SUBLANES
    assert seq % 1024 == 0 and nseq % SUBLANES == 0
    l = 0
    gpd = SSM_GROUPS_PER_DOT
    n_dots = N_SSM_GROUPS // gpd

    w_in_bf = w_in[l].astype(BF16)
    w_glu_bf = w_glu[l].astype(BF16)
    w_out_bf = w_out[l].astype(BF16)
    w_up_bf = w_up[l].astype(BF16)
    w_down_f = w_down[l]
    g_mix = norm_mix_g[l][None, :]
    g_ffn = norm_ffn_g[l][None, :]
    qg = jnp.tile(q_norm_g[l], 256 // HEAD_DIM)[None, :]
    kg = jnp.tile(k_norm_g[l], 256 // HEAD_DIM)[None, :]
    lane_head = jnp.arange(256) // HEAD_DIM
    ones_bd = (lane_head[:, None] == lane_head[None, :]).astype(BF16)
    sinks = attn_sinks[l]
    lamre = lam_re[l].reshape(1, D_STATE)
    lamim = lam_im[l].reshape(1, D_STATE)
    logdt = jnp.repeat(log_dt[l], SSM_STATE)[None, :]
    b_re_t = jnp.swapaxes(ssm_b_re[l], 1, 2).reshape(n_dots, gpd, SSM_GROUP, SSM_STATE)
    b_im_t = jnp.swapaxes(ssm_b_im[l], 1, 2).reshape(n_dots, gpd, SSM_GROUP, SSM_STATE)
    bblk = jnp.stack([b_re_t, b_im_t]).reshape(2, n_dots, gpd * SSM_GROUP, SSM_STATE)
    cre = _block_diag(jnp.swapaxes(ssm_c_re[l], 1, 2).reshape(n_dots, gpd, SSM_STATE, SSM_GROUP)).astype(BF16)
    cim = _block_diag(jnp.swapaxes(ssm_c_im[l], 1, 2).reshape(n_dots, gpd, SSM_STATE, SSM_GROUP)).astype(BF16)
    dskip = ssm_d[l].reshape(1, D_SSM)
    bglu = b_glu[l][None, :]
    g_ssm = ssm_out_g[l][None, :]
    g_att = attn_out_g[l][None, :]
    cw = conv_w[l]
    cb = conv_b[l][None, :]

    ssm_w = (lamre, lamim, logdt, bblk, cre, cim, dskip, w_glu_bf, bglu, g_ssm)

    pad = BLOCK - n_meta
    xm = jnp.concatenate([jnp.zeros((pad, D_MODEL), F32), meta_tokens.astype(F32)], axis=0)[None]
    cos_t, sin_t = _rope_tables(jnp.concatenate([
        jnp.maximum(jnp.arange(BLOCK) - pad, 0), n_meta + jnp.arange(seq), jnp.full((nseq,), PAST_LEN, jnp.int32)]))
    in_w = (g_mix, w_in_bf, cos_t, sin_t, qg, kg, ones_bd)
    u_m, q_m, k_m, v_m = _in_proj(xm, *in_w, tq=BLOCK, tab0=0)
    u_m8 = jnp.broadcast_to(u_m[pad:], (n_meta, SUBLANES, D_SSM)).reshape(n_meta * SUBLANES, D_SSM)
    zero_state = jnp.zeros((SUBLANES, D_STATE), F32)
    ys_m8, hre_m, him_m = _ssm(u_m8, zero_state, zero_state, *ssm_w, nbatch=SUBLANES, tt=n_meta)
    ys_m = jnp.concatenate([jnp.zeros((1, pad, D_SSM), BF16), ys_m8[0:1]], axis=1)
    zero_kv = jnp.zeros((1, BLOCK, D_KV), F32)
    ya_m = _attn_seq(sinks, q_m, k_m, v_m, zero_kv, zero_kv, g_att, n_meta=n_meta, first_block=0)
    x1_m = _out_proj(xm, ys_m, ya_m, w_out_bf, tm=BLOCK)

    xs = jnp.swapaxes(x_sample, 0, 1)
    u_s, q_s, k_s, v_s = _in_proj(xs, *in_w, tq=nseq, tab0=(BLOCK + seq) // nseq)
    ys_s, hre_s, him_s = _ssm(u_s.reshape(nseq, D_SSM), state_ssm_re[l].reshape(nseq, D_STATE),
                              state_ssm_im[l].reshape(nseq, D_STATE), *ssm_w, nbatch=nseq, tt=1)
    ya_s, nk_s, nv_s = _attn_step(
        q_s.reshape(nseq, N_HEADS, HEAD_DIM), k_s.reshape(nseq * N_KV_HEADS, HEAD_DIM),
        v_s.reshape(nseq * N_KV_HEADS, HEAD_DIM),
        jnp.transpose(cache_k[l], (0, 2, 3, 1)), jnp.transpose(cache_v[l], (0, 2, 3, 1)),
        sinks[:, None], attn_out_g[l].reshape(N_HEADS, HEAD_DIM), bb=8)
    x1_s = _out_proj(xs, ys_s, ya_s.reshape(1, nseq, D_ATTN), w_out_bf, tm=nseq)
    y_s, nbuf_s, bufv_m, bufg_m = _ffn_step(x1_s[0], g_ffn, w_up_bf, cw, cb, w_down_f, state_conv[l],
                                                  x1_m[0, BLOCK - 2 * SUBLANES:])

    y_sample = y_s.reshape(nseq, 1, D_MODEL)
    sample_k = jnp.transpose(nk_s, (0, 3, 1, 2))[None]
    sample_v = jnp.transpose(nv_s, (0, 3, 1, 2))[None]
    sample_re = hre_s.reshape(nseq, N_SSM_GROUPS, SSM_STATE)[None]
    sample_im = him_s.reshape(nseq, N_SSM_GROUPS, SSM_STATE)[None]
    sample_conv = nbuf_s[None]

    u_p, q_p, k_p, v_p = _in_proj(x_prompt, *in_w, tq=BLOCK, tab0=1)
    ys_p, hre_p, him_p = _ssm(u_p.reshape(seq * nb, D_SSM), hre_m, him_m, *ssm_w, nbatch=nb, tt=SCAN_STEPS)
    ya_p = _attn_seq(sinks, q_p, k_p, v_p, k_m, v_m, g_att, n_meta=n_meta, first_block=1)
    x1_p = _out_proj(x_prompt, ys_p, ya_p, w_out_bf, tm=512)
    y_p, bufv_p, bufg_p = _ffn_seq(x1_p.reshape(nb * seq, D_MODEL), g_ffn, w_up_bf, cw, cb, w_down_f,
                                   bufv_m[None], bufg_m[None], nb=nb, tm=1024)

    y_prompt = y_p.reshape(nb, seq, D_MODEL)
    prompt_k = k_p[:, seq - WINDOW:].reshape(nb, WINDOW, N_KV_HEADS, HEAD_DIM)[None]
    prompt_v = v_p[:, seq - WINDOW:].reshape(nb, WINDOW, N_KV_HEADS, HEAD_DIM)[None]
    prompt_re = hre_p.reshape(nb, N_SSM_GROUPS, SSM_STATE)[None]
    prompt_im = him_p.reshape(nb, N_SSM_GROUPS, SSM_STATE)[None]
    prompt_conv = jnp.concatenate([bufv_p[:, SUBLANES - 2:], bufg_p[:, SUBLANES - 2:]], axis=-1)[None]

    return (y_prompt, y_sample, prompt_k, prompt_v, prompt_re, prompt_im, prompt_conv,
            sample_k, sample_v, sample_re, sample_im, sample_conv)
```

```python
import functools

import jax
import jax.numpy as jnp
from jax import lax
from jax.experimental import pallas as pl
from jax.experimental.pallas import tpu as pltpu

F32 = jnp.float32
BF16 = jnp.bfloat16

D_MODEL = 2048
D_SSM = 1024
SSM_GROUP = 16
N_SSM_GROUPS = 64
SSM_STATE = 64
D_STATE = N_SSM_GROUPS * SSM_STATE
HEAD_DIM = 64
N_HEADS = 16
N_KV_HEADS = 4
Q_PER_KV = 4
D_ATTN = 1024
D_KV = 256
D_IN = D_SSM + D_ATTN + 2 * D_KV
WINDOW = 128
BLOCK = 128
ROPE_THETA = 10000.0
D_FF = 5632
CONV_W = 3
EPS = 1e-6
NEG = -1e30
PAST_LEN = 16384

SUBLANES = 8
LANES = 128
VMEM_LIMIT = 62 * 1024 * 1024
FF_TILE = 512
FFN_SUB_ROWS = 512
IN_SUB_ROWS = 256
ATTN_Q_BLOCKS = 4
SCAN_LANES = 512
SCAN_STEPS = 32
SSM_GROUPS_PER_DOT = 16


def _const_spec(shape):
    nd = len(shape)
    return pl.BlockSpec(shape, lambda *_: (0,) * nd, pipeline_mode=pl.Buffered(1))


def _rms(x, g):
    return x * lax.rsqrt(jnp.mean(x * x, axis=-1, keepdims=True) + EPS) * g


def _mm(a, b):
    return jnp.dot(a, b, preferred_element_type=F32)


def _in_proj_kernel(x_ref, g_ref, w_ref, cos_ref, sin_ref, qg_ref, kg_ref, ones_ref,
                    u_ref, q_ref, k_ref, v_ref, *, nb, tq, n_split):
    rs = nb * tq // n_split
    cos1 = jnp.concatenate([cos_ref[...], cos_ref[...]], axis=1)
    sin1 = jnp.concatenate([sin_ref[...], sin_ref[...]], axis=1)
    ones = ones_ref[...]
    n_q = D_ATTN // 256
    for s in range(n_split):
        segs = []
        for b in range(nb):
            lo = max(s * rs, b * tq)
            hi = min((s + 1) * rs, (b + 1) * tq)
            if lo < hi:
                segs.append((b, lo - b * tq, hi - lo, lo - s * rs))
        x = jnp.concatenate([x_ref[b, t0:t0 + n, :] for b, t0, n, _ in segs], axis=0)
        cos = jnp.concatenate([cos1[t0:t0 + n, :] for _, t0, n, _ in segs], axis=0)
        sin = jnp.concatenate([sin1[t0:t0 + n, :] for _, t0, n, _ in segs], axis=0)
        lane = lax.broadcasted_iota(jnp.int32, cos.shape, 1)
        first_half = (lane % HEAD_DIM) < (HEAD_DIM // 2)
        h = _rms(x, g_ref[...]).astype(BF16)
        z = _mm(h, w_ref[...])
        for b, t0, n, r0 in segs:
            u_ref[t0:t0 + n, b, :] = z[r0:r0 + n, :D_SSM]
            v_ref[b, t0:t0 + n, :] = z[r0:r0 + n, D_SSM + D_ATTN + D_KV:]
        for c in range(n_q + D_KV // 256):
            zc = z[:, D_SSM + 256 * c: D_SSM + 256 * (c + 1)]
            sq = zc * zc
            hi = sq.astype(BF16)
            lo = (sq - hi.astype(F32)).astype(BF16)
            ssum = _mm(hi, ones) + _mm(lo, ones)
            gain = qg_ref[...] if c < n_q else kg_ref[...]
            nrm = zc * lax.rsqrt(ssum * (1.0 / HEAD_DIM) + EPS) * gain
            partner = jnp.where(first_half, pltpu.roll(nrm, 256 - HEAD_DIM // 2, 1),
                                pltpu.roll(nrm, HEAD_DIM // 2, 1))
            r = nrm * cos + partner * sin
            for b, t0, n, r0 in segs:
                if c < n_q:
                    q_ref[b, t0:t0 + n, 256 * c: 256 * (c + 1)] = (
                        r[r0:r0 + n, :] * (HEAD_DIM ** -0.5)).astype(BF16)
                else:
                    k_ref[b, t0:t0 + n, :] = r[r0:r0 + n, :]


def _in_proj(x3, g, w_bf, cos, sin, qg, kg, ones_bd, *, tq, tab0):
    nb, t_len, _ = x3.shape
    blk = lambda t: (0, t, 0)
    kern = functools.partial(_in_proj_kernel, nb=nb, tq=tq, n_split=max(1, nb * tq // IN_SUB_ROWS))
    return pl.pallas_call(
        kern,
        grid=(t_len // tq,),
        in_specs=[
            pl.BlockSpec((nb, tq, D_MODEL), blk),
            _const_spec((1, D_MODEL)),
            _const_spec((D_MODEL, D_IN)),
            pl.BlockSpec((tq, LANES), lambda t: (tab0 + t, 0)),
            pl.BlockSpec((tq, LANES), lambda t: (tab0 + t, 0)),
            _const_spec((1, 256)),
            _const_spec((1, 256)),
            _const_spec((256, 256)),
        ],
        out_specs=[
            pl.BlockSpec((tq, nb, D_SSM), lambda t: (t, 0, 0)),
            pl.BlockSpec((nb, tq, D_ATTN), blk),
            pl.BlockSpec((nb, tq, D_KV), blk),
            pl.BlockSpec((nb, tq, D_KV), blk),
        ],
        out_shape=[
            jax.ShapeDtypeStruct((t_len, nb, D_SSM), F32),
            jax.ShapeDtypeStruct((nb, t_len, D_ATTN), BF16),
            jax.ShapeDtypeStruct((nb, t_len, D_KV), F32),
            jax.ShapeDtypeStruct((nb, t_len, D_KV), F32),
        ],
        compiler_params=pltpu.CompilerParams(
            dimension_semantics=("arbitrary",), vmem_limit_bytes=VMEM_LIMIT),
        name="in_proj",
    )(x3, g, w_bf, cos, sin, qg, kg, ones_bd)


def _ssm_kernel(u_ref, h0re_ref, h0im_ref, lamre_ref, lamim_ref, logdt_ref, bblk_ref,
                cre_ref, cim_ref, dskip_ref, wglu_ref, bglu_ref, g_ref, perm_ref,
                y_ref, sre_ref, sim_ref,
                are_s, aim_s, bbar_s, xre_s, xim_s, stre_s, stim_s, *, tt, nbg):
    step = pl.program_id(0)
    lanes_per_dot = SSM_GROUPS_PER_DOT * SSM_STATE
    chans_per_dot = SSM_GROUPS_PER_DOT * SSM_GROUP
    n_dots = N_SSM_GROUPS // SSM_GROUPS_PER_DOT
    state_lanes = [slice(lanes_per_dot * c, lanes_per_dot * (c + 1)) for c in range(n_dots)]
    chans = [slice(chans_per_dot * c, chans_per_dot * (c + 1)) for c in range(n_dots)]

    @pl.when(step == 0)
    def _():
        lr = lamre_ref[...]
        li = lamim_ref[...]
        dt = jnp.exp(logdt_ref[...])
        mag = jnp.exp(lr * dt)
        a_re = mag * jnp.cos(li * dt)
        a_im = mag * jnp.sin(li * dt)
        den = lr * lr + li * li
        am1 = a_re - 1.0
        cf_re = (am1 * lr + a_im * li) / den
        cf_im = (a_im * lr - am1 * li) / den
        shape = (SUBLANES, D_STATE)
        are_s[...] = jnp.broadcast_to(a_re, shape)
        aim_s[...] = jnp.broadcast_to(a_im, shape)
        row_grp = lax.broadcasted_iota(jnp.int32, (chans_per_dot, lanes_per_dot), 0) // SSM_GROUP
        lane_grp = lax.broadcasted_iota(jnp.int32, (chans_per_dot, lanes_per_dot), 1) // SSM_STATE
        on_diag = row_grp == lane_grp

        def expand(b):
            return jnp.where(on_diag, jnp.concatenate([b] * SSM_GROUPS_PER_DOT, axis=1), 0.0)

        for c in range(n_dots):
            b_re = expand(bblk_ref[0, c])
            b_im = expand(bblk_ref[1, c])
            c_re = cf_re[:, state_lanes[c]]
            c_im = cf_im[:, state_lanes[c]]
            bbar_s[c, :, :lanes_per_dot] = (c_re * b_re - c_im * b_im).astype(BF16)
            bbar_s[c, :, lanes_per_dot:] = (c_re * b_im + c_im * b_re).astype(BF16)
        stre_s[...] = h0re_ref[...]
        stim_s[...] = h0im_ref[...]

    u = u_ref[...]
    ub = u.astype(BF16)

    def project(c):
        bu = _mm(ub[:, chans[c]], bbar_s[c])
        xre_s[:, state_lanes[c]] = bu[:, :lanes_per_dot]
        xim_s[:, state_lanes[c]] = bu[:, lanes_per_dot:]

    def scan(c):
        blocks = [slice(lanes_per_dot * c + SCAN_LANES * i, lanes_per_dot * c + SCAN_LANES * (i + 1))
                  for i in range(lanes_per_dot // SCAN_LANES)]
        for bg in range(nbg):
            rs = slice(SUBLANES * bg, SUBLANES * (bg + 1))
            a = [(are_s[:, ls], aim_s[:, ls]) for ls in blocks]
            st = [(stre_s[rs, ls], stim_s[rs, ls]) for ls in blocks]
            for t in range(tt):
                rt = slice((t * nbg + bg) * SUBLANES, (t * nbg + bg + 1) * SUBLANES)
                for i, ls in enumerate(blocks):
                    (a_re, a_im), (s_re, s_im) = a[i], st[i]
                    n_re = a_re * s_re - a_im * s_im + xre_s[rt, ls]
                    n_im = a_re * s_im + a_im * s_re + xim_s[rt, ls]
                    xre_s[rt, ls] = n_re
                    xim_s[rt, ls] = n_im
                    st[i] = (n_re, n_im)
            for i, ls in enumerate(blocks):
                stre_s[rs, ls] = st[i][0]
                stim_s[rs, ls] = st[i][1]

    def readout(c):
        yc = (_mm(xre_s[:, state_lanes[c]].astype(BF16), cre_ref[c])
              - _mm(xim_s[:, state_lanes[c]].astype(BF16), cim_ref[c]))
        return jax.nn.gelu(yc + dskip_ref[:, chans[c]] * u[:, chans[c]])

    ys = [None] * n_dots
    project(0)
    for c in range(n_dots):
        if c + 1 < n_dots:
            project(c + 1)
        scan(c)
        if c > 0:
            ys[c - 1] = readout(c - 1)
    ys[n_dots - 1] = readout(n_dots - 1)
    y = jnp.concatenate(ys, axis=1)
    y = y * jax.nn.sigmoid(_mm(y.astype(BF16), wglu_ref[...]) + bglu_ref[...])
    yn = _rms(y, g_ref[...]).astype(BF16)
    if tt == 1:
        y_ref[0] = yn
    else:
        y_ref[...] = _mm(perm_ref[...], yn).astype(BF16).reshape(y_ref.shape)

    @pl.when(step == pl.num_programs(0) - 1)
    def _():
        sre_ref[...] = stre_s[...]
        sim_ref[...] = stim_s[...]


def _ssm(u_tm, h0re, h0im, lamre, lamim, logdt, bblk, cre, cim, dskip, wglu, bglu, g, *, nbatch, tt):
    n = u_tm.shape[0]
    t_len = n // nbatch
    nbg = nbatch // SUBLANES
    rows = tt * nbatch
    n_grp = nbatch if tt > 1 else 1
    r = jnp.arange(rows)
    src = (r % tt) * nbatch + r // tt
    perm = (src[:, None] == r[None, :]).astype(BF16)
    kern = functools.partial(_ssm_kernel, tt=tt, nbg=nbg)
    return pl.pallas_call(
        kern,
        grid=(t_len // tt,),
        in_specs=[
            pl.BlockSpec((rows, D_SSM), lambda s: (s, 0)),
            _const_spec((nbatch, D_STATE)),
            _const_spec((nbatch, D_STATE)),
            _const_spec((1, D_STATE)),
            _const_spec((1, D_STATE)),
            _const_spec((1, D_STATE)),
            _const_spec(bblk.shape),
            _const_spec(cre.shape),
            _const_spec(cim.shape),
            _const_spec((1, D_SSM)),
            _const_spec((D_SSM, D_SSM)),
            _const_spec((1, D_SSM)),
            _const_spec((1, D_SSM)),
            _const_spec((rows, rows)),
        ],
        out_specs=[
            pl.BlockSpec((n_grp, rows // n_grp, D_SSM), lambda s: (0, s, 0)),
            pl.BlockSpec((nbatch, D_STATE), lambda s: (0, 0)),
            pl.BlockSpec((nbatch, D_STATE), lambda s: (0, 0)),
        ],
        out_shape=[
            jax.ShapeDtypeStruct((n_grp, n // n_grp, D_SSM), BF16),
            jax.ShapeDtypeStruct((nbatch, D_STATE), F32),
            jax.ShapeDtypeStruct((nbatch, D_STATE), F32),
        ],
        scratch_shapes=[pltpu.VMEM((SUBLANES, D_STATE), F32)] * 2
        + [pltpu.VMEM((bblk.shape[1], bblk.shape[2], 2 * SSM_GROUPS_PER_DOT * SSM_STATE), BF16)]
        + [pltpu.VMEM((rows, D_STATE), F32)] * 2
        + [pltpu.VMEM((nbatch, D_STATE), F32)] * 2,
        compiler_params=pltpu.CompilerParams(
            dimension_semantics=("arbitrary",), vmem_limit_bytes=VMEM_LIMIT),
        name="ssm",
    )(u_tm, h0re, h0im, lamre, lamim, logdt, bblk, cre, cim, dskip, wglu, bglu, g, perm)


def _attn_seq_kernel(sink_ref, q_ref, kc_ref, kp_ref, k0_ref, vc_ref, vp_ref, v0_ref, g_ref, o_ref,
                     *, first_block, n_meta, qb):
    i = pl.program_id(1)
    is_first = i == 0
    r = lax.broadcasted_iota(jnp.int32, (BLOCK, 4 * BLOCK), 0)
    c = lax.broadcasted_iota(jnp.int32, (BLOCK, 4 * BLOCK), 1) % (2 * BLOCK)
    in_window = (c > r) & (c <= r + WINDOW)
    lo_half = lax.broadcasted_iota(jnp.int32, (2 * BLOCK, LANES), 1) < HEAD_DIM

    def pair_blocks(x2, kv):
        xc = x2[:, LANES * (kv // 2): LANES * (kv // 2 + 1)]
        xr = pltpu.roll(xc, HEAD_DIM, 1)
        zero = jnp.zeros_like(xc)
        if kv % 2 == 0:
            low, high = jnp.where(lo_half, xc, zero), jnp.where(lo_half, zero, xr)
        else:
            low, high = jnp.where(lo_half, xr, zero), jnp.where(lo_half, zero, xc)
        return jnp.concatenate([low, high], axis=0).astype(BF16)

    n_pairs = N_HEADS // 2
    pairs_per_kv = Q_PER_KV // 2
    row_hi = lax.broadcasted_iota(jnp.int32, (4 * BLOCK, LANES), 0) >= 2 * BLOCK
    lane_hi = lax.broadcasted_iota(jnp.int32, (4 * BLOCK, LANES), 1) >= HEAD_DIM
    ones_cols = (row_hi == lane_hi).astype(BF16)
    kbd, vbd, mask, rows = [], [], [], []
    for sb in range(qb):
        rows.append(slice(BLOCK * sb, BLOCK * (sb + 1)))
        if sb == 0:
            k_prev = jnp.where(is_first, k0_ref[0], kp_ref[0])
            v_prev = jnp.where(is_first, v0_ref[0], vp_ref[0])
        else:
            k_prev, v_prev = kc_ref[0, rows[sb - 1], :], vc_ref[0, rows[sb - 1], :]
        k2 = jnp.concatenate([k_prev, kc_ref[0, rows[sb], :]], axis=0)
        v2 = jnp.concatenate([v_prev, vc_ref[0, rows[sb], :]], axis=0)
        v2 = jnp.where(lax.broadcasted_iota(jnp.int32, v2.shape, 0) == 0, 0.0, v2)
        gb = i * qb + sb + first_block
        col_min = jnp.where(gb == 0, 2 * BLOCK - n_meta, jnp.where(gb == 1, BLOCK - n_meta, 0))
        mask.append(in_window & (c >= col_min))
        kbd.append([pair_blocks(k2, kv) for kv in range(N_KV_HEADS)])
        vbd.append([jnp.concatenate([pair_blocks(v2, kv), ones_cols], axis=1) for kv in range(N_KV_HEADS)])
    halves = (slice(0, 2 * BLOCK), slice(2 * BLOCK, 4 * BLOCK))
    probs = [(sb, pr) for sb in range(qb) for pr in range(n_pairs)]
    col = lax.broadcasted_iota(jnp.int32, (1, 4 * BLOCK), 1)
    off_window = [jnp.where(col == 0, sink_ref[2 * pr],
                            jnp.where(col == 2 * BLOCK, sink_ref[2 * pr + 1], NEG)) for pr in range(n_pairs)]
    s = {(sb, pr): jnp.where(mask[sb], lax.dot_general(
        q_ref[0, rows[sb], LANES * pr: LANES * (pr + 1)], kbd[sb][pr // pairs_per_kv],
        (((1,), (1,)), ((), ())), preferred_element_type=F32), off_window[pr]) for sb, pr in probs}
    m = {(sb, pr): [jnp.max(s[sb, pr][:, halves[hh]], axis=-1, keepdims=True) for hh in range(2)]
         for sb, pr in probs}
    p = {(sb, pr): jnp.concatenate([jnp.exp(s[sb, pr][:, halves[hh]] - m[sb, pr][hh]) for hh in range(2)],
                                   axis=1).astype(BF16) for sb, pr in probs}
    outs = {}
    for sb, pr in probs:
        ov = _mm(p[sb, pr], vbd[sb][pr // pairs_per_kv])
        outs[sb, pr] = ov[:, :LANES] / ov[:, LANES:]
    for sb in range(qb):
        o = jnp.concatenate([outs[sb, pr] for pr in range(n_pairs)], axis=1)
        o_ref[0, rows[sb], :] = _rms(o, g_ref[...]).astype(BF16)


def _attn_seq(sinks, q, k, v, k0, v0, g, *, n_meta, first_block):
    nb, t_len, _ = q.shape
    nblk = t_len // BLOCK
    qb = ATTN_Q_BLOCKS if nblk % ATTN_Q_BLOCKS == 0 else 1
    cur = lambda b, i: (b, i, 0)
    prev = lambda b, i: (b, jnp.maximum(i * qb - 1, 0), 0)
    init = lambda b, i: (0, 0, 0)
    kern = functools.partial(_attn_seq_kernel, first_block=first_block, n_meta=n_meta, qb=qb)
    return pl.pallas_call(
        kern,
        grid=(nb, nblk // qb),
        in_specs=[
            pl.BlockSpec(memory_space=pltpu.SMEM),
            pl.BlockSpec((1, qb * BLOCK, D_ATTN), cur),
            pl.BlockSpec((1, qb * BLOCK, D_KV), cur),
            pl.BlockSpec((1, BLOCK, D_KV), prev),
            pl.BlockSpec((1, BLOCK, D_KV), init),
            pl.BlockSpec((1, qb * BLOCK, D_KV), cur),
            pl.BlockSpec((1, BLOCK, D_KV), prev),
            pl.BlockSpec((1, BLOCK, D_KV), init),
            _const_spec((1, D_ATTN)),
        ],
        out_specs=pl.BlockSpec((1, qb * BLOCK, D_ATTN), cur),
        out_shape=jax.ShapeDtypeStruct((nb, t_len, D_ATTN), BF16),
        compiler_params=pltpu.CompilerParams(
            dimension_semantics=("arbitrary", "arbitrary"), vmem_limit_bytes=VMEM_LIMIT),
        name="attn_seq",
    )(sinks, q, k, k, k0, v, v, v0, g)


def _attn_step_kernel(q_ref, kn_ref, vn_ref, ck_ref, cv_ref, sink_ref, g_ref, o_ref, ok_ref, ov_ref):
    is_last = lax.broadcasted_iota(jnp.int32, (1, WINDOW), 1) == WINDOW - 1
    bb = ck_ref.shape[0]

    def shift_in(cache_ref, new_ref, out_ref):
        new = new_ref[...]
        new = jnp.concatenate([new, jnp.zeros((new.shape[0], LANES - HEAD_DIM), F32)], axis=1)
        new = jnp.concatenate([new, jnp.zeros((LANES - new.shape[0], LANES), F32)], axis=0)
        cols = new.T
        for b in range(bb):
            for kv in range(N_KV_HEADS):
                j = b * N_KV_HEADS + kv
                out_ref[b, kv] = jnp.where(is_last, cols[0:HEAD_DIM, j:j + 1],
                                           pltpu.roll(cache_ref[b, kv], WINDOW - 1, 1))

    shift_in(ck_ref, kn_ref, ok_ref)
    shift_in(cv_ref, vn_ref, ov_ref)
    k_t = ok_ref[...]
    v_t = ov_ref[...]
    q = q_ref[...]
    kv_of_head = lax.broadcasted_iota(jnp.int32, (1, N_HEADS, 1), 1) // Q_PER_KV
    s = None
    for kv in range(N_KV_HEADS):
        s_kv = jnp.einsum('bhd,bdw->bhw', q, k_t[:, kv].astype(BF16), preferred_element_type=F32)
        s = s_kv if s is None else jnp.where(kv_of_head == kv, s_kv, s)
    sink = sink_ref[...][None]
    m = jnp.maximum(jnp.max(s, axis=-1, keepdims=True), sink)
    p = jnp.exp(s - m)
    den = jnp.sum(p, axis=-1, keepdims=True) + jnp.exp(sink - m)
    pb = (p / den).astype(BF16)
    o = None
    for kv in range(N_KV_HEADS):
        o_kv = jnp.einsum('bhw,bdw->bhd', pb, v_t[:, kv].astype(BF16), preferred_element_type=F32)
        o = o_kv if o is None else jnp.where(kv_of_head == kv, o_kv, o)
    ms = jnp.sum(jnp.sum(o * o, axis=2, keepdims=True), axis=1, keepdims=True) * (1.0 / D_ATTN)
    o_ref[...] = (o * lax.rsqrt(ms + EPS) * g_ref[...][None]).astype(BF16)


def _attn_step(q3, kn, vn, ck, cv, sinks, g, *, bb):
    nseq = q3.shape[0]
    blk3 = lambda i: (i, 0, 0)
    blk4 = lambda i: (i, 0, 0, 0)
    cache_blk = (bb, N_KV_HEADS, HEAD_DIM, WINDOW)
    new_spec = pl.BlockSpec((bb * N_KV_HEADS, HEAD_DIM), lambda i: (i, 0))
    return pl.pallas_call(
        _attn_step_kernel,
        grid=(nseq // bb,),
        in_specs=[
            pl.BlockSpec((bb, N_HEADS, HEAD_DIM), blk3),
            new_spec,
            new_spec,
            pl.BlockSpec(cache_blk, blk4),
            pl.BlockSpec(cache_blk, blk4),
            _const_spec((N_HEADS, 1)),
            _const_spec((N_HEADS, HEAD_DIM)),
        ],
        out_specs=[
            pl.BlockSpec((bb, N_HEADS, HEAD_DIM), blk3),
            pl.BlockSpec(cache_blk, blk4),
            pl.BlockSpec(cache_blk, blk4),
        ],
        out_shape=[
            jax.ShapeDtypeStruct((nseq, N_HEADS, HEAD_DIM), BF16),
            jax.ShapeDtypeStruct((nseq,) + cache_blk[1:], F32),
            jax.ShapeDtypeStruct((nseq,) + cache_blk[1:], F32),
        ],
        compiler_params=pltpu.CompilerParams(
            dimension_semantics=("arbitrary",), vmem_limit_bytes=VMEM_LIMIT),
        name="attn_step",
    )(q3, kn, vn, ck, cv, sinks, g)


def _out_proj_kernel(x_ref, ys_ref, ya_ref, w_ref, o_ref):
    o_ref[0] = (x_ref[0] + _mm(ys_ref[0], w_ref[0:D_SSM, :])
                + _mm(ya_ref[0], w_ref[D_SSM:D_SSM + D_ATTN, :]))


def _out_proj(x3, ys, ya, w_bf, *, tm):
    nb, t_len, _ = x3.shape
    blk = lambda b, t: (b, t, 0)
    return pl.pallas_call(
        _out_proj_kernel,
        grid=(nb, t_len // tm),
        in_specs=[
            pl.BlockSpec((1, tm, D_MODEL), blk),
            pl.BlockSpec((1, tm, D_SSM), blk),
            pl.BlockSpec((1, tm, D_ATTN), blk),
            _const_spec((D_SSM + D_ATTN, D_MODEL)),
        ],
        out_specs=pl.BlockSpec((1, tm, D_MODEL), blk),
        out_shape=jax.ShapeDtypeStruct((nb, t_len, D_MODEL), F32),
        compiler_params=pltpu.CompilerParams(
            dimension_semantics=("arbitrary", "arbitrary"), vmem_limit_bytes=VMEM_LIMIT),
        name="out_proj",
    )(x3, ys, ya, w_bf)


def _ffn_seq_kernel(x_ref, g_ref, wv_ref, wg_ref, cwv_ref, cwg_ref, cbv_ref, cbg_ref, wd_ref,
                    bufv_ref, bufg_ref, y_ref, nbv_ref, nbg_ref,
                    h_s, upv_s, upg_s, *, tm, tf):
    t = pl.program_id(1)
    j = pl.program_id(2)
    chan = pl.ds(pl.multiple_of(j * tf, tf), tf)

    @pl.when(j == 0)
    def _():
        x = x_ref[...]
        h_s[...] = _rms(x, g_ref[...]).astype(BF16)
        y_ref[...] = x

    @pl.when(t == 0)
    def _():
        upv_s[0:SUBLANES, :] = bufv_ref[0]
        upg_s[0:SUBLANES, :] = bufg_ref[0]

    @pl.when(t > 0)
    def _():
        upv_s[0:SUBLANES, :] = nbv_ref[0, :, chan]
        upg_s[0:SUBLANES, :] = nbg_ref[0, :, chan]

    rm = min(tm, FFN_SUB_ROWS)
    n_sub = tm // rm

    def up_proj(up_s, w_ref, r0):
        up_s[SUBLANES + r0:SUBLANES + r0 + rm, :] = _mm(h_s[r0:r0 + rm, :], w_ref[...])

    def conv(up_s, cw_ref, cb_ref, r0):
        return (cb_ref[...] + cw_ref[0:1, :] * up_s[SUBLANES - 2 + r0:SUBLANES - 2 + r0 + rm, :]
                + cw_ref[1:2, :] * up_s[SUBLANES - 1 + r0:SUBLANES - 1 + r0 + rm, :]
                + cw_ref[2:3, :] * up_s[SUBLANES + r0:SUBLANES + r0 + rm, :])

    def gate_down(r0):
        val = conv(upv_s, cwv_ref, cbv_ref, r0)
        gate = conv(upg_s, cwg_ref, cbg_ref, r0)
        act = (jax.nn.silu(gate) * val).astype(BF16)
        y_ref[r0:r0 + rm, :] += _mm(act, wd)

    wd = wd_ref[...].astype(BF16)
    for r in range(n_sub):
        up_proj(upv_s, wv_ref, r * rm)
        up_proj(upg_s, wg_ref, r * rm)
        if r > 0:
            gate_down((r - 1) * rm)
    gate_down((n_sub - 1) * rm)
    nbv_ref[0, :, chan] = upv_s[tm:tm + SUBLANES, :]
    nbg_ref[0, :, chan] = upg_s[tm:tm + SUBLANES, :]


def _ffn_seq(x1, g, wup_bf, conv_w, conv_b, wdown, bufv, bufg, *, nb, tm):
    n = x1.shape[0]
    nt = n // nb // tm
    tf = FF_TILE
    nj = D_FF // tf
    row = lambda b, t, j: (b * nt + t, 0)
    kern = functools.partial(_ffn_seq_kernel, tm=tm, tf=tf)
    return pl.pallas_call(
        kern,
        grid=(nb, nt, nj),
        in_specs=[
            pl.BlockSpec((tm, D_MODEL), row),
            _const_spec((1, D_MODEL)),
            pl.BlockSpec((D_MODEL, tf), lambda b, t, j: (0, j)),
            pl.BlockSpec((D_MODEL, tf), lambda b, t, j: (0, nj + j)),
            pl.BlockSpec((CONV_W, tf), lambda b, t, j: (0, j)),
            pl.BlockSpec((CONV_W, tf), lambda b, t, j: (0, nj + j)),
            pl.BlockSpec((1, tf), lambda b, t, j: (0, j)),
            pl.BlockSpec((1, tf), lambda b, t, j: (0, nj + j)),
            pl.BlockSpec((tf, D_MODEL), lambda b, t, j: (j, 0)),
            pl.BlockSpec((1, SUBLANES, tf), lambda b, t, j: (0, 0, j)),
            pl.BlockSpec((1, SUBLANES, tf), lambda b, t, j: (0, 0, j)),
        ],
        out_specs=[
            pl.BlockSpec((tm, D_MODEL), row),
            pl.BlockSpec((1, SUBLANES, D_FF), lambda b, t, j: (b, 0, 0)),
            pl.BlockSpec((1, SUBLANES, D_FF), lambda b, t, j: (b, 0, 0)),
        ],
        out_shape=[
            jax.ShapeDtypeStruct((n, D_MODEL), F32),
            jax.ShapeDtypeStruct((nb, SUBLANES, D_FF), F32),
            jax.ShapeDtypeStruct((nb, SUBLANES, D_FF), F32),
        ],
        scratch_shapes=[
            pltpu.VMEM((tm, D_MODEL), BF16),
            pltpu.VMEM((tm + SUBLANES, tf), F32),
            pltpu.VMEM((tm + SUBLANES, tf), F32),
        ],
        compiler_params=pltpu.CompilerParams(
            dimension_semantics=("arbitrary", "arbitrary", "arbitrary"), vmem_limit_bytes=VMEM_LIMIT),
        name="ffn_seq",
    )(x1, g, wup_bf, wup_bf, conv_w, conv_w, conv_b, conv_b, wdown, bufv, bufg)


def _ffn_step_kernel(x_ref, g_ref, wv_ref, wg_ref, cwv_ref, cwg_ref, cbv_ref, cbg_ref, wd_ref,
                     bufv_ref, bufg_ref, xt_ref, y_ref, nbuf_ref, tv_ref, tg_ref, h_s, ht_s, *, tf):
    j = pl.program_id(0)

    @pl.when(j == 0)
    def _():
        x = x_ref[...]
        h_s[...] = _rms(x, g_ref[...]).astype(BF16)
        ht_s[...] = _rms(xt_ref[...], g_ref[...]).astype(BF16)
        y_ref[...] = x

    n_tail = ht_s.shape[0]
    tv_ref[...] = _mm(ht_s[...], wv_ref[...])[n_tail - SUBLANES:, :]
    tg_ref[...] = _mm(ht_s[...], wg_ref[...])[n_tail - SUBLANES:, :]
    h = h_s[...]
    upv = _mm(h, wv_ref[...])
    upg = _mm(h, wg_ref[...])
    val = cbv_ref[...] + cwv_ref[0:1, :] * bufv_ref[:, 0, :] + cwv_ref[1:2, :] * bufv_ref[:, 1, :] + cwv_ref[2:3, :] * upv
    gate = cbg_ref[...] + cwg_ref[0:1, :] * bufg_ref[:, 0, :] + cwg_ref[1:2, :] * bufg_ref[:, 1, :] + cwg_ref[2:3, :] * upg
    act = (jax.nn.silu(gate) * val).astype(BF16)
    y_ref[...] += _mm(act, wd_ref[...].astype(BF16))
    val_cols = pl.ds(pl.multiple_of(j * tf, tf), tf)
    gate_cols = pl.ds(pl.multiple_of(D_FF + j * tf, tf), tf)
    nbuf_ref[:, 0, val_cols] = bufv_ref[:, 1, :]
    nbuf_ref[:, 0, gate_cols] = bufg_ref[:, 1, :]
    nbuf_ref[:, 1, val_cols] = upv
    nbuf_ref[:, 1, gate_cols] = upg


def _ffn_step(x1, g, wup_bf, conv_w, conv_b, wdown, buf, x_tail):
    nseq = x1.shape[0]
    n_tail = x_tail.shape[0]
    tf = FF_TILE
    nj = D_FF // tf
    return pl.pallas_call(
        functools.partial(_ffn_step_kernel, tf=tf),
        grid=(nj,),
        in_specs=[
            _const_spec((nseq, D_MODEL)),
            _const_spec((1, D_MODEL)),
            pl.BlockSpec((D_MODEL, tf), lambda j: (0, j)),
            pl.BlockSpec((D_MODEL, tf), lambda j: (0, nj + j)),
            pl.BlockSpec((CONV_W, tf), lambda j: (0, j)),
            pl.BlockSpec((CONV_W, tf), lambda j: (0, nj + j)),
            pl.BlockSpec((1, tf), lambda j: (0, j)),
            pl.BlockSpec((1, tf), lambda j: (0, nj + j)),
            pl.BlockSpec((tf, D_MODEL), lambda j: (j, 0)),
            pl.BlockSpec((nseq, CONV_W - 1, tf), lambda j: (0, 0, j)),
            pl.BlockSpec((nseq, CONV_W - 1, tf), lambda j: (0, 0, nj + j)),
            _const_spec((n_tail, D_MODEL)),
        ],
        out_specs=[
            pl.BlockSpec((nseq, D_MODEL), lambda j: (0, 0)),
            _const_spec((nseq, CONV_W - 1, 2 * D_FF)),
            pl.BlockSpec((SUBLANES, tf), lambda j: (0, j)),
            pl.BlockSpec((SUBLANES, tf), lambda j: (0, j)),
        ],
        out_shape=[
            jax.ShapeDtypeStruct((nseq, D_MODEL), F32),
            jax.ShapeDtypeStruct((nseq, CONV_W - 1, 2 * D_FF), F32),
            jax.ShapeDtypeStruct((SUBLANES, D_FF), F32),
            jax.ShapeDtypeStruct((SUBLANES, D_FF), F32),
        ],
        scratch_shapes=[pltpu.VMEM((nseq, D_MODEL), BF16), pltpu.VMEM((n_tail, D_MODEL), BF16)],
        compiler_params=pltpu.CompilerParams(
            dimension_semantics=("arbitrary",), vmem_limit_bytes=VMEM_LIMIT),
        name="ffn_step",
    )(x1, g, wup_bf, wup_bf, conv_w, conv_w, conv_b, conv_b, wdown, buf, buf, x_tail)


def _rope_tables(pos):
    half = HEAD_DIM // 2
    inv = ROPE_THETA ** (-jnp.arange(half, dtype=F32) * 2.0 / HEAD_DIM)
    ang = pos.astype(F32)[:, None] * inv[None, :]
    c = jnp.cos(ang)
    s = jnp.sin(ang)
    return jnp.concatenate([c, c, c, c], axis=1), jnp.concatenate([-s, s, -s, s], axis=1)


def _block_diag(w):
    nd, ng, r, c = w.shape
    tiled = jnp.tile(w.reshape(nd, ng * r, c), (1, 1, ng))
    on_diag = (jnp.arange(ng * r)[:, None] // r) == (jnp.arange(ng * c)[None, :] // c)
    return jnp.where(on_diag[None], tiled, jnp.zeros_like(tiled))


def kernel(x_prompt, x_sample, cache_k, cache_v, state_ssm_re, state_ssm_im, state_conv, meta_tokens,
           norm_mix_g, w_in, q_norm_g, k_norm_g, attn_sinks, lam_re, lam_im, log_dt, ssm_b_re, ssm_b_im,
           ssm_c_re, ssm_c_im, ssm_d, w_glu, b_glu, ssm_out_g, attn_out_g, w_out, norm_ffn_g, w_up,
           conv_w, conv_b, w_down):
    depth = w_in.shape[0]
    assert depth == 1
    nb, seq, _ = x_prompt.shape
    nseq = x_sample.shape[0]
    n_meta = meta_tokens.shape[0]
    assert x_sample.shape[1] == 1 and n_meta == 16 and nb == SUBLANES
    assert seq % 1024 == 0 and nseq % SUBLANES == 0
    l = 0
    gpd = SSM_GROUPS_PER_DOT
    n_dots = N_SSM_GROUPS // gpd

    w_in_bf = w_in[l].astype(BF16)
    w_glu_bf = w_glu[l].astype(BF16)
    w_out_bf = w_out[l].astype(BF16)
    w_up_bf = w_up[l].astype(BF16)
    w_down_f = w_down[l]
    g_mix = norm_mix_g[l][None, :]
    g_ffn = norm_ffn_g[l][None, :]
    qg = jnp.tile(q_norm_g[l], 256 // HEAD_DIM)[None, :]
    kg = jnp.tile(k_norm_g[l], 256 // HEAD_DIM)[None, :]
    lane_head = jnp.arange(256) // HEAD_DIM
    ones_bd = (lane_head[:, None] == lane_head[None, :]).astype(BF16)
    sinks = attn_sinks[l]
    lamre = lam_re[l].reshape(1, D_STATE)
    lamim = lam_im[l].reshape(1, D_STATE)
    logdt = jnp.repeat(log_dt[l], SSM_STATE)[None, :]
    b_re_t = jnp.swapaxes(ssm_b_re[l], 1, 2).reshape(n_dots, gpd, SSM_GROUP, SSM_STATE)
    b_im_t = jnp.swapaxes(ssm_b_im[l], 1, 2).reshape(n_dots, gpd, SSM_GROUP, SSM_STATE)
    bblk = jnp.stack([b_re_t, b_im_t]).reshape(2, n_dots, gpd * SSM_GROUP, SSM_STATE)
    cre = _block_diag(jnp.swapaxes(ssm_c_re[l], 1, 2).reshape(n_dots, gpd, SSM_STATE, SSM_GROUP)).astype(BF16)
    cim = _block_diag(jnp.swapaxes(ssm_c_im[l], 1, 2).reshape(n_dots, gpd, SSM_STATE, SSM_GROUP)).astype(BF16)
    dskip = ssm_d[l].reshape(1, D_SSM)
    bglu = b_glu[l][None, :]
    g_ssm = ssm_out_g[l][None, :]
    g_att = attn_out_g[l][None, :]
    cw = conv_w[l]
    cb = conv_b[l][None, :]

    ssm_w = (lamre, lamim, logdt, bblk, cre, cim, dskip, w_glu_bf, bglu, g_ssm)

    pad = BLOCK - n_meta
    xm = jnp.concatenate([jnp.zeros((pad, D_MODEL), F32), meta_tokens.astype(F32)], axis=0)[None]
    cos_t, sin_t = _rope_tables(jnp.concatenate([
        jnp.maximum(jnp.arange(BLOCK) - pad, 0), n_meta + jnp.arange(seq), jnp.full((nseq,), PAST_LEN, jnp.int32)]))
    in_w = (g_mix, w_in_bf, cos_t, sin_t, qg, kg, ones_bd)
    u_m, q_m, k_m, v_m = _in_proj(xm, *in_w, tq=BLOCK, tab0=0)
    u_m8 = jnp.broadcast_to(u_m[pad:], (n_meta, SUBLANES, D_SSM)).reshape(n_meta * SUBLANES, D_SSM)
    zero_state = jnp.zeros((SUBLANES, D_STATE), F32)
    ys_m8, hre_m, him_m = _ssm(u_m8, zero_state, zero_state, *ssm_w, nbatch=SUBLANES, tt=n_meta)
    ys_m = jnp.concatenate([jnp.zeros((1, pad, D_SSM), BF16), ys_m8[0:1]], axis=1)
    zero_kv = jnp.zeros((1, BLOCK, D_KV), F32)
    ya_m = _attn_seq(sinks, q_m, k_m, v_m, zero_kv, zero_kv, g_att, n_meta=n_meta, first_block=0)
    x1_m = _out_proj(xm, ys_m, ya_m, w_out_bf, tm=BLOCK)

    xs = jnp.swapaxes(x_sample, 0, 1)
    u_s, q_s, k_s, v_s = _in_proj(xs, *in_w, tq=nseq, tab0=(BLOCK + seq) // nseq)
    ys_s, hre_s, him_s = _ssm(u_s.reshape(nseq, D_SSM), state_ssm_re[l].reshape(nseq, D_STATE),
                              state_ssm_im[l].reshape(nseq, D_STATE), *ssm_w, nbatch=nseq, tt=1)
    ya_s, nk_s, nv_s = _attn_step(
        q_s.reshape(nseq, N_HEADS, HEAD_DIM), k_s.reshape(nseq * N_KV_HEADS, HEAD_DIM),
        v_s.reshape(nseq * N_KV_HEADS, HEAD_DIM),
        jnp.transpose(cache_k[l], (0, 2, 3, 1)), jnp.transpose(cache_v[l], (0, 2, 3, 1)),
        sinks[:, None], attn_out_g[l].reshape(N_HEADS, HEAD_DIM), bb=8)
    x1_s = _out_proj(xs, ys_s, ya_s.reshape(1, nseq, D_ATTN), w_out_bf, tm=nseq)
    y_s, nbuf_s, bufv_m, bufg_m = _ffn_step(x1_s[0], g_ffn, w_up_bf, cw, cb, w_down_f, state_conv[l],
                                                  x1_m[0, BLOCK - 2 * SUBLANES:])

    y_sample = y_s.reshape(nseq, 1, D_MODEL)
    sample_k = jnp.transpose(nk_s, (0, 3, 1, 2))[None]
    sample_v = jnp.transpose(nv_s, (0, 3, 1, 2))[None]
    sample_re = hre_s.reshape(nseq, N_SSM_GROUPS, SSM_STATE)[None]
    sample_im = him_s.reshape(nseq, N_SSM_GROUPS, SSM_STATE)[None]
    sample_conv = nbuf_s[None]

    u_p, q_p, k_p, v_p = _in_proj(x_prompt, *in_w, tq=BLOCK, tab0=1)
    ys_p, hre_p, him_p = _ssm(u_p.reshape(seq * nb, D_SSM), hre_m, him_m, *ssm_w, nbatch=nb, tt=SCAN_STEPS)
    ya_p = _attn_seq(sinks, q_p, k_p, v_p, k_m, v_m, g_att, n_meta=n_meta, first_block=1)
    x1_p = _out_proj(x_prompt, ys_p, ya_p, w_out_bf, tm=512)
    y_p, bufv_p, bufg_p = _ffn_seq(x1_p.reshape(nb * seq, D_MODEL), g_ffn, w_up_bf, cw, cb, w_down_f,
                                   bufv_m[None], bufg_m[None], nb=nb, tm=1024)

    y_prompt = y_p.reshape(nb, seq, D_MODEL)
    prompt_k = k_p[:, seq - WINDOW:].reshape(nb, WINDOW, N_KV_HEADS, HEAD_DIM)[None]
    prompt_v = v_p[:, seq - WINDOW:].reshape(nb, WINDOW, N_KV_HEADS, HEAD_DIM)[None]
    prompt_re = hre_p.reshape(nb, N_SSM_GROUPS, SSM_STATE)[None]
    prompt_im = him_p.reshape(nb, N_SSM_GROUPS, SSM_STATE)[None]
    prompt_conv = jnp.concatenate([bufv_p[:, SUBLANES - 2:], bufg_p[:, SUBLANES - 2:]], axis=-1)[None]

    return (y_prompt, y_sample, prompt_k, prompt_v, prompt_re, prompt_im, prompt_conv,
            sample_k, sample_v, sample_re, sample_im, sample_conv)
```

```python
import functools

import jax
import jax.numpy as jnp
from jax import lax
from jax.experimental import pallas as pl
from jax.experimental.pallas import tpu as pltpu

F32 = jnp.float32
BF16 = jnp.bfloat16

D_MODEL = 2048
D_SSM = 1024
SSM_GROUP = 16
N_SSM_GROUPS = 64
SSM_STATE = 64
D_STATE = N_SSM_GROUPS * SSM_STATE
HEAD_DIM = 64
N_HEADS = 16
N_KV_HEADS = 4
Q_PER_KV = 4
D_ATTN = 1024
D_KV = 256
D_IN = D_SSM + D_ATTN + 2 * D_KV
WINDOW = 128
BLOCK = 128
ROPE_THETA = 10000.0
D_FF = 5632
CONV_W = 3
EPS = 1e-6
NEG = -1e30
PAST_LEN = 16384

SUBLANES = 8
LANES = 128
VMEM_LIMIT = 62 * 1024 * 1024
FF_TILE = 512
FFN_SUB_ROWS = 512
IN_SUB_ROWS = 256
ATTN_Q_BLOCKS = 4
SCAN_LANES = 512
SCAN_STEPS = 32
SSM_GROUPS_PER_DOT = 16


def _const_spec(shape):
    nd = len(shape)
    return pl.BlockSpec(shape, lambda *_: (0,) * nd, pipeline_mode=pl.Buffered(1))


def _rms(x, g):
    return x * lax.rsqrt(jnp.mean(x * x, axis=-1, keepdims=True) + EPS) * g


def _mm(a, b):
    return jnp.dot(a, b, preferred_element_type=F32)


def _in_proj_kernel(x_ref, g_ref, w_ref, cos_ref, sin_ref, qg_ref, kg_ref, ones_ref,
                    u_ref, q_ref, k_ref, v_ref, *, nb, tq, n_split):
    rs = nb * tq // n_split
    cos1 = jnp.concatenate([cos_ref[...], cos_ref[...]], axis=1)
    sin1 = jnp.concatenate([sin_ref[...], sin_ref[...]], axis=1)
    ones = ones_ref[...]
    n_q = D_ATTN // 256
    for s in range(n_split):
        segs = []
        for b in range(nb):
            lo = max(s * rs, b * tq)
            hi = min((s + 1) * rs, (b + 1) * tq)
            if lo < hi:
                segs.append((b, lo - b * tq, hi - lo, lo - s * rs))
        x = jnp.concatenate([x_ref[b, t0:t0 + n, :] for b, t0, n, _ in segs], axis=0)
        cos = jnp.concatenate([cos1[t0:t0 + n, :] for _, t0, n, _ in segs], axis=0)
        sin = jnp.concatenate([sin1[t0:t0 + n, :] for _, t0, n, _ in segs], axis=0)
        lane = lax.broadcasted_iota(jnp.int32, cos.shape, 1)
        first_half = (lane % HEAD_DIM) < (HEAD_DIM // 2)
        h = _rms(x, g_ref[...]).astype(BF16)
        z = _mm(h, w_ref[...])
        for b, t0, n, r0 in segs:
            u_ref[t0:t0 + n, b, :] = z[r0:r0 + n, :D_SSM]
            v_ref[b, t0:t0 + n, :] = z[r0:r0 + n, D_SSM + D_ATTN + D_KV:]
        for c in range(n_q + D_KV // 256):
            zc = z[:, D_SSM + 256 * c: D_SSM + 256 * (c + 1)]
            sq = zc * zc
            hi = sq.astype(BF16)
            lo = (sq - hi.astype(F32)).astype(BF16)
            ssum = _mm(hi, ones) + _mm(lo, ones)
            gain = qg_ref[...] if c < n_q else kg_ref[...]
            nrm = zc * lax.rsqrt(ssum * (1.0 / HEAD_DIM) + EPS) * gain
            partner = jnp.where(first_half, pltpu.roll(nrm, 256 - HEAD_DIM // 2, 1),
                                pltpu.roll(nrm, HEAD_DIM // 2, 1))
            r = nrm * cos + partner * sin
            for b, t0, n, r0 in segs:
                if c < n_q:
                    q_ref[b, t0:t0 + n, 256 * c: 256 * (c + 1)] = (
                        r[r0:r0 + n, :] * (HEAD_DIM ** -0.5)).astype(BF16)
                else:
                    k_ref[b, t0:t0 + n, :] = r[r0:r0 + n, :]


def _in_proj(x3, g, w_bf, cos, sin, qg, kg, ones_bd, *, tq, tab0):
    nb, t_len, _ = x3.shape
    blk = lambda t: (0, t, 0)
    kern = functools.partial(_in_proj_kernel, nb=nb, tq=tq, n_split=max(1, nb * tq // IN_SUB_ROWS))
    return pl.pallas_call(
        kern,
        grid=(t_len // tq,),
        in_specs=[
            pl.BlockSpec((nb, tq, D_MODEL), blk),
            _const_spec((1, D_MODEL)),
            _const_spec((D_MODEL, D_IN)),
            pl.BlockSpec((tq, LANES), lambda t: (tab0 + t, 0)),
            pl.BlockSpec((tq, LANES), lambda t: (tab0 + t, 0)),
            _const_spec((1, 256)),
            _const_spec((1, 256)),
            _const_spec((256, 256)),
        ],
        out_specs=[
            pl.BlockSpec((tq, nb, D_SSM), lambda t: (t, 0, 0)),
            pl.BlockSpec((nb, tq, D_ATTN), blk),
            pl.BlockSpec((nb, tq, D_KV), blk),
            pl.BlockSpec((nb, tq, D_KV), blk),
        ],
        out_shape=[
            jax.ShapeDtypeStruct((t_len, nb, D_SSM), F32),
            jax.ShapeDtypeStruct((nb, t_len, D_ATTN), BF16),
            jax.ShapeDtypeStruct((nb, t_len, D_KV), F32),
            jax.ShapeDtypeStruct((nb, t_len, D_KV), F32),
        ],
        compiler_params=pltpu.CompilerParams(
            dimension_semantics=("arbitrary",), vmem_limit_bytes=VMEM_LIMIT),
        name="in_proj",
    )(x3, g, w_bf, cos, sin, qg, kg, ones_bd)


def _ssm_kernel(u_ref, h0re_ref, h0im_ref, lamre_ref, lamim_ref, logdt_ref, bblk_ref,
                cre_ref, cim_ref, dskip_ref, wglu_ref, bglu_ref, g_ref, perm_ref,
                y_ref, sre_ref, sim_ref,
                are_s, aim_s, bbar_s, xre_s, xim_s, stre_s, stim_s, *, tt, nbg):
    step = pl.program_id(0)
    lanes_per_dot = SSM_GROUPS_PER_DOT * SSM_STATE
    chans_per_dot = SSM_GROUPS_PER_DOT * SSM_GROUP
    n_dots = N_SSM_GROUPS // SSM_GROUPS_PER_DOT
    state_lanes = [slice(lanes_per_dot * c, lanes_per_dot * (c + 1)) for c in range(n_dots)]
    chans = [slice(chans_per_dot * c, chans_per_dot * (c + 1)) for c in range(n_dots)]

    @pl.when(step == 0)
    def _():
        lr = lamre_ref[...]
        li = lamim_ref[...]
        dt = jnp.exp(logdt_ref[...])
        mag = jnp.exp(lr * dt)
        a_re = mag * jnp.cos(li * dt)
        a_im = mag * jnp.sin(li * dt)
        den = lr * lr + li * li
        am1 = a_re - 1.0
        cf_re = (am1 * lr + a_im * li) / den
        cf_im = (a_im * lr - am1 * li) / den
        shape = (SUBLANES, D_STATE)
        are_s[...] = jnp.broadcast_to(a_re, shape)
        aim_s[...] = jnp.broadcast_to(a_im, shape)
        row_grp = lax.broadcasted_iota(jnp.int32, (chans_per_dot, lanes_per_dot), 0) // SSM_GROUP
        lane_grp = lax.broadcasted_iota(jnp.int32, (chans_per_dot, lanes_per_dot), 1) // SSM_STATE
        on_diag = row_grp == lane_grp

        def expand(b):
            return jnp.where(on_diag, jnp.concatenate([b] * SSM_GROUPS_PER_DOT, axis=1), 0.0)

        for c in range(n_dots):
            b_re = expand(bblk_ref[0, c])
            b_im = expand(bblk_ref[1, c])
            c_re = cf_re[:, state_lanes[c]]
            c_im = cf_im[:, state_lanes[c]]
            bbar_s[c, :, :lanes_per_dot] = (c_re * b_re - c_im * b_im).astype(BF16)
            bbar_s[c, :, lanes_per_dot:] = (c_re * b_im + c_im * b_re).astype(BF16)
        stre_s[...] = h0re_ref[...]
        stim_s[...] = h0im_ref[...]

    u = u_ref[...]
    ub = u.astype(BF16)

    def project(c):
        bu = _mm(ub[:, chans[c]], bbar_s[c])
        xre_s[:, state_lanes[c]] = bu[:, :lanes_per_dot]
        xim_s[:, state_lanes[c]] = bu[:, lanes_per_dot:]

    def scan(c):
        blocks = [slice(lanes_per_dot * c + SCAN_LANES * i, lanes_per_dot * c + SCAN_LANES * (i + 1))
                  for i in range(lanes_per_dot // SCAN_LANES)]
        for bg in range(nbg):
            rs = slice(SUBLANES * bg, SUBLANES * (bg + 1))
            a = [(are_s[:, ls], aim_s[:, ls]) for ls in blocks]
            st = [(stre_s[rs, ls], stim_s[rs, ls]) for ls in blocks]
            for t in range(tt):
                rt = slice((t * nbg + bg) * SUBLANES, (t * nbg + bg + 1) * SUBLANES)
                for i, ls in enumerate(blocks):
                    (a_re, a_im), (s_re, s_im) = a[i], st[i]
                    n_re = a_re * s_re - a_im * s_im + xre_s[rt, ls]
                    n_im = a_re * s_im + a_im * s_re + xim_s[rt, ls]
                    xre_s[rt, ls] = n_re
                    xim_s[rt, ls] = n_im
                    st[i] = (n_re, n_im)
            for i, ls in enumerate(blocks):
                stre_s[rs, ls] = st[i][0]
                stim_s[rs, ls] = st[i][1]

    def readout(c):
        yc = (_mm(xre_s[:, state_lanes[c]].astype(BF16), cre_ref[c])
              - _mm(xim_s[:, state_lanes[c]].astype(BF16), cim_ref[c]))
        return jax.nn.gelu(yc + dskip_ref[:, chans[c]] * u[:, chans[c]])

    ys = [None] * n_dots
    project(0)
    for c in range(n_dots):
        if c + 1 < n_dots:
            project(c + 1)
        scan(c)
        if c > 0:
            ys[c - 1] = readout(c - 1)
    ys[n_dots - 1] = readout(n_dots - 1)
    y = jnp.concatenate(ys, axis=1)
    y = y * jax.nn.sigmoid(_mm(y.astype(BF16), wglu_ref[...]) + bglu_ref[...])
    yn = _rms(y, g_ref[...]).astype(BF16)
    if tt == 1:
        y_ref[0] = yn
    else:
        y_ref[...] = _mm(perm_ref[...], yn).astype(BF16).reshape(y_ref.shape)

    @pl.when(step == pl.num_programs(0) - 1)
    def _():
        sre_ref[...] = stre_s[...]
        sim_ref[...] = stim_s[...]


def _ssm(u_tm, h0re, h0im, lamre, lamim, logdt, bblk, cre, cim, dskip, wglu, bglu, g, *, nbatch, tt):
    n = u_tm.shape[0]
    t_len = n // nbatch
    nbg = nbatch // SUBLANES
    rows = tt * nbatch
    n_grp = nbatch if tt > 1 else 1
    r = jnp.arange(rows)
    src = (r % tt) * nbatch + r // tt
    perm = (src[:, None] == r[None, :]).astype(BF16)
    kern = functools.partial(_ssm_kernel, tt=tt, nbg=nbg)
    return pl.pallas_call(
        kern,
        grid=(t_len // tt,),
        in_specs=[
            pl.BlockSpec((rows, D_SSM), lambda s: (s, 0)),
            _const_spec((nbatch, D_STATE)),
            _const_spec((nbatch, D_STATE)),
            _const_spec((1, D_STATE)),
            _const_spec((1, D_STATE)),
            _const_spec((1, D_STATE)),
            _const_spec(bblk.shape),
            _const_spec(cre.shape),
            _const_spec(cim.shape),
            _const_spec((1, D_SSM)),
            _const_spec((D_SSM, D_SSM)),
            _const_spec((1, D_SSM)),
            _const_spec((1, D_SSM)),
            _const_spec((rows, rows)),
        ],
        out_specs=[
            pl.BlockSpec((n_grp, rows // n_grp, D_SSM), lambda s: (0, s, 0)),
            pl.BlockSpec((nbatch, D_STATE), lambda s: (0, 0)),
            pl.BlockSpec((nbatch, D_STATE), lambda s: (0, 0)),
        ],
        out_shape=[
            jax.ShapeDtypeStruct((n_grp, n // n_grp, D_SSM), BF16),
            jax.ShapeDtypeStruct((nbatch, D_STATE), F32),
            jax.ShapeDtypeStruct((nbatch, D_STATE), F32),
        ],
        scratch_shapes=[pltpu.VMEM((SUBLANES, D_STATE), F32)] * 2
        + [pltpu.VMEM((bblk.shape[1], bblk.shape[2], 2 * SSM_GROUPS_PER_DOT * SSM_STATE), BF16)]
        + [pltpu.VMEM((rows, D_STATE), F32)] * 2
        + [pltpu.VMEM((nbatch, D_STATE), F32)] * 2,
        compiler_params=pltpu.CompilerParams(
            dimension_semantics=("arbitrary",), vmem_limit_bytes=VMEM_LIMIT),
        name="ssm",
    )(u_tm, h0re, h0im, lamre, lamim, logdt, bblk, cre, cim, dskip, wglu, bglu, g, perm)


def _attn_seq_kernel(sink_ref, q_ref, kc_ref, kp_ref, k0_ref, vc_ref, vp_ref, v0_ref, g_ref, o_ref,
                     *, first_block, n_meta, qb):
    i = pl.program_id(1)
    is_first = i == 0
    r = lax.broadcasted_iota(jnp.int32, (BLOCK, 4 * BLOCK), 0)
    c = lax.broadcasted_iota(jnp.int32, (BLOCK, 4 * BLOCK), 1) % (2 * BLOCK)
    in_window = (c > r) & (c <= r + WINDOW)
    lo_half = lax.broadcasted_iota(jnp.int32, (2 * BLOCK, LANES), 1) < HEAD_DIM

    def pair_blocks(x2, kv):
        xc = x2[:, LANES * (kv // 2): LANES * (kv // 2 + 1)]
        xr = pltpu.roll(xc, HEAD_DIM, 1)
        zero = jnp.zeros_like(xc)
        if kv % 2 == 0:
            low, high = jnp.where(lo_half, xc, zero), jnp.where(lo_half, zero, xr)
        else:
            low, high = jnp.where(lo_half, xr, zero), jnp.where(lo_half, zero, xc)
        return jnp.concatenate([low, high], axis=0).astype(BF16)

    n_pairs = N_HEADS // 2
    pairs_per_kv = Q_PER_KV // 2
    row_hi = lax.broadcasted_iota(jnp.int32, (4 * BLOCK, LANES), 0) >= 2 * BLOCK
    lane_hi = lax.broadcasted_iota(jnp.int32, (4 * BLOCK, LANES), 1) >= HEAD_DIM
    ones_cols = (row_hi == lane_hi).astype(BF16)
    kbd, vbd, mask, rows = [], [], [], []
    for sb in range(qb):
        rows.append(slice(BLOCK * sb, BLOCK * (sb + 1)))
        if sb == 0:
            k_prev = jnp.where(is_first, k0_ref[0], kp_ref[0])
            v_prev = jnp.where(is_first, v0_ref[0], vp_ref[0])
        else:
            k_prev, v_prev = kc_ref[0, rows[sb - 1], :], vc_ref[0, rows[sb - 1], :]
        k2 = jnp.concatenate([k_prev, kc_ref[0, rows[sb], :]], axis=0)
        v2 = jnp.concatenate([v_prev, vc_ref[0, rows[sb], :]], axis=0)
        v2 = jnp.where(lax.broadcasted_iota(jnp.int32, v2.shape, 0) == 0, 0.0, v2)
        gb = i * qb + sb + first_block
        col_min = jnp.where(gb == 0, 2 * BLOCK - n_meta, jnp.where(gb == 1, BLOCK - n_meta, 0))
        mask.append(in_window & (c >= col_min))
        kbd.append([pair_blocks(k2, kv) for kv in range(N_KV_HEADS)])
        vbd.append([jnp.concatenate([pair_blocks(v2, kv), ones_cols], axis=1) for kv in range(N_KV_HEADS)])
    halves = (slice(0, 2 * BLOCK), slice(2 * BLOCK, 4 * BLOCK))
    probs = [(sb, pr) for sb in range(qb) for pr in range(n_pairs)]
    col = lax.broadcasted_iota(jnp.int32, (1, 4 * BLOCK), 1)
    off_window = [jnp.where(col == 0, sink_ref[2 * pr],
                            jnp.where(col == 2 * BLOCK, sink_ref[2 * pr + 1], NEG)) for pr in range(n_pairs)]
    s = {(sb, pr): jnp.where(mask[sb], lax.dot_general(
        q_ref[0, rows[sb], LANES * pr: LANES * (pr + 1)], kbd[sb][pr // pairs_per_kv],
        (((1,), (1,)), ((), ())), preferred_element_type=F32), off_window[pr]) for sb, pr in probs}
    m = {(sb, pr): [jnp.max(s[sb, pr][:, halves[hh]], axis=-1, keepdims=True) for hh in range(2)]
         for sb, pr in probs}
    p = {(sb, pr): jnp.concatenate([jnp.exp(s[sb, pr][:, halves[hh]] - m[sb, pr][hh]) for hh in range(2)],
                                   axis=1).astype(BF16) for sb, pr in probs}
    outs = {}
    for sb, pr in probs:
        ov = _mm(p[sb, pr], vbd[sb][pr // pairs_per_kv])
        outs[sb, pr] = ov[:, :LANES] / ov[:, LANES:]
    for sb in range(qb):
        o = jnp.concatenate([outs[sb, pr] for pr in range(n_pairs)], axis=1)
        o_ref[0, rows[sb], :] = _rms(o, g_ref[...]).astype(BF16)


def _attn_seq(sinks, q, k, v, k0, v0, g, *, n_meta, first_block):
    nb, t_len, _ = q.shape
    nblk = t_len // BLOCK
    qb = ATTN_Q_BLOCKS if nblk % ATTN_Q_BLOCKS == 0 else 1
    cur = lambda b, i: (b, i, 0)
    prev = lambda b, i: (b, jnp.maximum(i * qb - 1, 0), 0)
    init = lambda b, i: (0, 0, 0)
    kern = functools.partial(_attn_seq_kernel, first_block=first_block, n_meta=n_meta, qb=qb)
    return pl.pallas_call(
        kern,
        grid=(nb, nblk // qb),
        in_specs=[
            pl.BlockSpec(memory_space=pltpu.SMEM),
            pl.BlockSpec((1, qb * BLOCK, D_ATTN), cur),
            pl.BlockSpec((1, qb * BLOCK, D_KV), cur),
            pl.BlockSpec((1, BLOCK, D_KV), prev),
            pl.BlockSpec((1, BLOCK, D_KV), init),
            pl.BlockSpec((1, qb * BLOCK, D_KV), cur),
            pl.BlockSpec((1, BLOCK, D_KV), prev),
            pl.BlockSpec((1, BLOCK, D_KV), init),
            _const_spec((1, D_ATTN)),
        ],
        out_specs=pl.BlockSpec((1, qb * BLOCK, D_ATTN), cur),
        out_shape=jax.ShapeDtypeStruct((nb, t_len, D_ATTN), BF16),
        compiler_params=pltpu.CompilerParams(
            dimension_semantics=("arbitrary", "arbitrary"), vmem_limit_bytes=VMEM_LIMIT),
        name="attn_seq",
    )(sinks, q, k, k, k0, v, v, v0, g)


def _attn_step_kernel(q_ref, kn_ref, vn_ref, ck_ref, cv_ref, sink_ref, g_ref, o_ref, ok_ref, ov_ref):
    is_last = lax.broadcasted_iota(jnp.int32, (1, WINDOW), 1) == WINDOW - 1
    bb = ck_ref.shape[0]

    def shift_in(cache_ref, new_ref, out_ref):
        new = new_ref[...]
        new = jnp.concatenate([new, jnp.zeros((new.shape[0], LANES - HEAD_DIM), F32)], axis=1)
        new = jnp.concatenate([new, jnp.zeros((LANES - new.shape[0], LANES), F32)], axis=0)
        cols = new.T
        for b in range(bb):
            for kv in range(N_KV_HEADS):
                j = b * N_KV_HEADS + kv
                out_ref[b, kv] = jnp.where(is_last, cols[0:HEAD_DIM, j:j + 1],
                                           pltpu.roll(cache_ref[b, kv], WINDOW - 1, 1))

    shift_in(ck_ref, kn_ref, ok_ref)
    shift_in(cv_ref, vn_ref, ov_ref)
    k_t = ok_ref[...]
    v_t = ov_ref[...]
    q = q_ref[...]
    kv_of_head = lax.broadcasted_iota(jnp.int32, (1, N_HEADS, 1), 1) // Q_PER_KV
    s = None
    for kv in range(N_KV_HEADS):
        s_kv = jnp.einsum('bhd,bdw->bhw', q, k_t[:, kv].astype(BF16), preferred_element_type=F32)
        s = s_kv if s is None else jnp.where(kv_of_head == kv, s_kv, s)
    sink = sink_ref[...][None]
    m = jnp.maximum(jnp.max(s, axis=-1, keepdims=True), sink)
    p = jnp.exp(s - m)
    den = jnp.sum(p, axis=-1, keepdims=True) + jnp.exp(sink - m)
    pb = (p / den).astype(BF16)
    o = None
    for kv in range(N_KV_HEADS):
        o_kv = jnp.einsum('bhw,bdw->bhd', pb, v_t[:, kv].astype(BF16), preferred_element_type=F32)
        o = o_kv if o is None else jnp.where(kv_of_head == kv, o_kv, o)
    ms = jnp.sum(jnp.sum(o * o, axis=2, keepdims=True), axis=1, keepdims=True) * (1.0 / D_ATTN)
    o_ref[...] = (o * lax.rsqrt(ms + EPS) * g_ref[...][None]).astype(BF16)


def _attn_step(q3, kn, vn, ck, cv, sinks, g, *, bb):
    nseq = q3.shape[0]
    blk3 = lambda i: (i, 0, 0)
    blk4 = lambda i: (i, 0, 0, 0)
    cache_blk = (bb, N_KV_HEADS, HEAD_DIM, WINDOW)
    new_spec = pl.BlockSpec((bb * N_KV_HEADS, HEAD_DIM), lambda i: (i, 0))
    return pl.pallas_call(
        _attn_step_kernel,
        grid=(nseq // bb,),
        in_specs=[
            pl.BlockSpec((bb, N_HEADS, HEAD_DIM), blk3),
            new_spec,
            new_spec,
            pl.BlockSpec(cache_blk, blk4),
            pl.BlockSpec(cache_blk, blk4),
            _const_spec((N_HEADS, 1)),
            _const_spec((N_HEADS, HEAD_DIM)),
        ],
        out_specs=[
            pl.BlockSpec((bb, N_HEADS, HEAD_DIM), blk3),
            pl.BlockSpec(cache_blk, blk4),
            pl.BlockSpec(cache_blk, blk4),
        ],
        out_shape=[
            jax.ShapeDtypeStruct((nseq, N_HEADS, HEAD_DIM), BF16),
            jax.ShapeDtypeStruct((nseq,) + cache_blk[1:], F32),
            jax.ShapeDtypeStruct((nseq,) + cache_blk[1:], F32),
        ],
        compiler_params=pltpu.CompilerParams(
            dimension_semantics=("arbitrary",), vmem_limit_bytes=VMEM_LIMIT),
        name="attn_step",
    )(q3, kn, vn, ck, cv, sinks, g)


def _out_proj_kernel(x_ref, ys_ref, ya_ref, w_ref, o_ref):
    o_ref[0] = (x_ref[0] + _mm(ys_ref[0], w_ref[0:D_SSM, :])
                + _mm(ya_ref[0], w_ref[D_SSM:D_SSM + D_ATTN, :]))


def _out_proj(x3, ys, ya, w_bf, *, tm):
    nb, t_len, _ = x3.shape
    blk = lambda b, t: (b, t, 0)
    return pl.pallas_call(
        _out_proj_kernel,
        grid=(nb, t_len // tm),
        in_specs=[
            pl.BlockSpec((1, tm, D_MODEL), blk),
            pl.BlockSpec((1, tm, D_SSM), blk),
            pl.BlockSpec((1, tm, D_ATTN), blk),
            _const_spec((D_SSM + D_ATTN, D_MODEL)),
        ],
        out_specs=pl.BlockSpec((1, tm, D_MODEL), blk),
        out_shape=jax.ShapeDtypeStruct((nb, t_len, D_MODEL), F32),
        compiler_params=pltpu.CompilerParams(
            dimension_semantics=("arbitrary", "arbitrary"), vmem_limit_bytes=VMEM_LIMIT),
        name="out_proj",
    )(x3, ys, ya, w_bf)


def _ffn_seq_kernel(x_ref, g_ref, wv_ref, wg_ref, cwv_ref, cwg_ref, cbv_ref, cbg_ref, wd_ref,
                    bufv_ref, bufg_ref, y_ref, nbv_ref, nbg_ref,
                    h_s, upv_s, upg_s, *, tm, tf):
    t = pl.program_id(1)
    j = pl.program_id(2)
    chan = pl.ds(pl.multiple_of(j * tf, tf), tf)

    @pl.when(j == 0)
    def _():
        x = x_ref[...]
        h_s[...] = _rms(x, g_ref[...]).astype(BF16)
        y_ref[...] = x

    @pl.when(t == 0)
    def _():
        upv_s[0:SUBLANES, :] = bufv_ref[0]
        upg_s[0:SUBLANES, :] = bufg_ref[0]

    @pl.when(t > 0)
    def _():
        upv_s[0:SUBLANES, :] = nbv_ref[0, :, chan]
        upg_s[0:SUBLANES, :] = nbg_ref[0, :, chan]

    rm = min(tm, FFN_SUB_ROWS)
    n_sub = tm // rm

    def up_proj(up_s, w_ref, r0):
        up_s[SUBLANES + r0:SUBLANES + r0 + rm, :] = _mm(h_s[r0:r0 + rm, :], w_ref[...])

    def conv(up_s, cw_ref, cb_ref, r0):
        return (cb_ref[...] + cw_ref[0:1, :] * up_s[SUBLANES - 2 + r0:SUBLANES - 2 + r0 + rm, :]
                + cw_ref[1:2, :] * up_s[SUBLANES - 1 + r0:SUBLANES - 1 + r0 + rm, :]
                + cw_ref[2:3, :] * up_s[SUBLANES + r0:SUBLANES + r0 + rm, :])

    def gate_down(r0):
        val = conv(upv_s, cwv_ref, cbv_ref, r0)
        gate = conv(upg_s, cwg_ref, cbg_ref, r0)
        act = (jax.nn.silu(gate) * val).astype(BF16)
        y_ref[r0:r0 + rm, :] += _mm(act, wd)

    wd = wd_ref[...].astype(BF16)
    for r in range(n_sub):
        up_proj(upv_s, wv_ref, r * rm)
        up_proj(upg_s, wg_ref, r * rm)
        if r > 0:
            gate_down((r - 1) * rm)
    gate_down((n_sub - 1) * rm)
    nbv_ref[0, :, chan] = upv_s[tm:tm + SUBLANES, :]
    nbg_ref[0, :, chan] = upg_s[tm:tm + SUBLANES, :]


def _ffn_seq(x1, g, wup_bf, conv_w, conv_b, wdown, bufv, bufg, *, nb, tm):
    n = x1.shape[0]
    nt = n // nb // tm
    tf = FF_TILE
    nj = D_FF // tf
    row = lambda b, t, j: (b * nt + t, 0)
    kern = functools.partial(_ffn_seq_kernel, tm=tm, tf=tf)
    return pl.pallas_call(
        kern,
        grid=(nb, nt, nj),
        in_specs=[
            pl.BlockSpec((tm, D_MODEL), row),
            _const_spec((1, D_MODEL)),
            pl.BlockSpec((D_MODEL, tf), lambda b, t, j: (0, j)),
            pl.BlockSpec((D_MODEL, tf), lambda b, t, j: (0, nj + j)),
            pl.BlockSpec((CONV_W, tf), lambda b, t, j: (0, j)),
            pl.BlockSpec((CONV_W, tf), lambda b, t, j: (0, nj + j)),
            pl.BlockSpec((1, tf), lambda b, t, j: (0, j)),
            pl.BlockSpec((1, tf), lambda b, t, j: (0, nj + j)),
            pl.BlockSpec((tf, D_MODEL), lambda b, t, j: (j, 0)),
            pl.BlockSpec((1, SUBLANES, tf), lambda b, t, j: (0, 0, j)),
            pl.BlockSpec((1, SUBLANES, tf), lambda b, t, j: (0, 0, j)),
        ],
        out_specs=[
            pl.BlockSpec((tm, D_MODEL), row),
            pl.BlockSpec((1, SUBLANES, D_FF), lambda b, t, j: (b, 0, 0)),
            pl.BlockSpec((1, SUBLANES, D_FF), lambda b, t, j: (b, 0, 0)),
        ],
        out_shape=[
            jax.ShapeDtypeStruct((n, D_MODEL), F32),
            jax.ShapeDtypeStruct((nb, SUBLANES, D_FF), F32),
            jax.ShapeDtypeStruct((nb, SUBLANES, D_FF), F32),
        ],
        scratch_shapes=[
            pltpu.VMEM((tm, D_MODEL), BF16),
            pltpu.VMEM((tm + SUBLANES, tf), F32),
            pltpu.VMEM((tm + SUBLANES, tf), F32),
        ],
        compiler_params=pltpu.CompilerParams(
            dimension_semantics=("arbitrary", "arbitrary", "arbitrary"), vmem_limit_bytes=VMEM_LIMIT),
        name="ffn_seq",
    )(x1, g, wup_bf, wup_bf, conv_w, conv_w, conv_b, conv_b, wdown, bufv, bufg)


def _ffn_step_kernel(x_ref, g_ref, wv_ref, wg_ref, cwv_ref, cwg_ref, cbv_ref, cbg_ref, wd_ref,
                     bufv_ref, bufg_ref, xt_ref, y_ref, nbuf_ref, tv_ref, tg_ref, h_s, ht_s, *, tf):
    j = pl.program_id(0)

    @pl.when(j == 0)
    def _():
        x = x_ref[...]
        h_s[...] = _rms(x, g_ref[...]).astype(BF16)
        ht_s[...] = _rms(xt_ref[...], g_ref[...]).astype(BF16)
        y_ref[...] = x

    n_tail = ht_s.shape[0]
    tv_ref[...] = _mm(ht_s[...], wv_ref[...])[n_tail - SUBLANES:, :]
    tg_ref[...] = _mm(ht_s[...], wg_ref[...])[n_tail - SUBLANES:, :]
    h = h_s[...]
    upv = _mm(h, wv_ref[...])
    upg = _mm(h, wg_ref[...])
    val = cbv_ref[...] + cwv_ref[0:1, :] * bufv_ref[:, 0, :] + cwv_ref[1:2, :] * bufv_ref[:, 1, :] + cwv_ref[2:3, :] * upv
    gate = cbg_ref[...] + cwg_ref[0:1, :] * bufg_ref[:, 0, :] + cwg_ref[1:2, :] * bufg_ref[:, 1, :] + cwg_ref[2:3, :] * upg
    act = (jax.nn.silu(gate) * val).astype(BF16)
    y_ref[...] += _mm(act, wd_ref[...].astype(BF16))
    val_cols = pl.ds(pl.multiple_of(j * tf, tf), tf)
    gate_cols = pl.ds(pl.multiple_of(D_FF + j * tf, tf), tf)
    nbuf_ref[:, 0, val_cols] = bufv_ref[:, 1, :]
    nbuf_ref[:, 0, gate_cols] = bufg_ref[:, 1, :]
    nbuf_ref[:, 1, val_cols] = upv
    nbuf_ref[:, 1, gate_cols] = upg


def _ffn_step(x1, g, wup_bf, conv_w, conv_b, wdown, buf, x_tail):
    nseq = x1.shape[0]
    n_tail = x_tail.shape[0]
    tf = FF_TILE
    nj = D_FF // tf
    return pl.pallas_call(
        functools.partial(_ffn_step_kernel, tf=tf),
        grid=(nj,),
        in_specs=[
            _const_spec((nseq, D_MODEL)),
            _const_spec((1, D_MODEL)),
            pl.BlockSpec((D_MODEL, tf), lambda j: (0, j)),
            pl.BlockSpec((D_MODEL, tf), lambda j: (0, nj + j)),
            pl.BlockSpec((CONV_W, tf), lambda j: (0, j)),
            pl.BlockSpec((CONV_W, tf), lambda j: (0, nj + j)),
            pl.BlockSpec((1, tf), lambda j: (0, j)),
            pl.BlockSpec((1, tf), lambda j: (0, nj + j)),
            pl.BlockSpec((tf, D_MODEL), lambda j: (j, 0)),
            pl.BlockSpec((nseq, CONV_W - 1, tf), lambda j: (0, 0, j)),
            pl.BlockSpec((nseq, CONV_W - 1, tf), lambda j: (0, 0, nj + j)),
            _const_spec((n_tail, D_MODEL)),
        ],
        out_specs=[
            pl.BlockSpec((nseq, D_MODEL), lambda j: (0, 0)),
            _const_spec((nseq, CONV_W - 1, 2 * D_FF)),
            pl.BlockSpec((SUBLANES, tf), lambda j: (0, j)),
            pl.BlockSpec((SUBLANES, tf), lambda j: (0, j)),
        ],
        out_shape=[
            jax.ShapeDtypeStruct((nseq, D_MODEL), F32),
            jax.ShapeDtypeStruct((nseq, CONV_W - 1, 2 * D_FF), F32),
            jax.ShapeDtypeStruct((SUBLANES, D_FF), F32),
            jax.ShapeDtypeStruct((SUBLANES, D_FF), F32),
        ],
        scratch_shapes=[pltpu.VMEM((nseq, D_MODEL), BF16), pltpu.VMEM((n_tail, D_MODEL), BF16)],
        compiler_params=pltpu.CompilerParams(
            dimension_semantics=("arbitrary",), vmem_limit_bytes=VMEM_LIMIT),
        name="ffn_step",
    )(x1, g, wup_bf, wup_bf, conv_w, conv_w, conv_b, conv_b, wdown, buf, buf, x_tail)


def _rope_tables(pos):
    half = HEAD_DIM // 2
    inv = ROPE_THETA ** (-jnp.arange(half, dtype=F32) * 2.0 / HEAD_DIM)
    ang = pos.astype(F32)[:, None] * inv[None, :]
    c = jnp.cos(ang)
    s = jnp.sin(ang)
    return jnp.concatenate([c, c, c, c], axis=1), jnp.concatenate([-s, s, -s, s], axis=1)


def _block_diag(w):
    nd, ng, r, c = w.shape
    tiled = jnp.tile(w.reshape(nd, ng * r, c), (1, 1, ng))
    on_diag = (jnp.arange(ng * r)[:, None] // r) == (jnp.arange(ng * c)[None, :] // c)
    return jnp.where(on_diag[None], tiled, jnp.zeros_like(tiled))


def kernel(x_prompt, x_sample, cache_k, cache_v, state_ssm_re, state_ssm_im, state_conv, meta_tokens,
           norm_mix_g, w_in, q_norm_g, k_norm_g, attn_sinks, lam_re, lam_im, log_dt, ssm_b_re, ssm_b_im,
           ssm_c_re, ssm_c_im, ssm_d, w_glu, b_glu, ssm_out_g, attn_out_g, w_out, norm_ffn_g, w_up,
           conv_w, conv_b, w_down):
    depth = w_in.shape[0]
    assert depth == 1
    nb, seq, _ = x_prompt.shape
    nseq = x_sample.shape[0]
    n_meta = meta_tokens.shape[0]
    assert x_sample.shape[1] == 1 and n_meta == 16 and nb == SUBLANES
    assert seq % 1024 == 0 and nseq % SUBLANES == 0
    l = 0
    gpd = SSM_GROUPS_PER_DOT
    n_dots = N_SSM_GROUPS // gpd

    w_in_bf = w_in[l].astype(BF16)
    w_glu_bf = w_glu[l].astype(BF16)
    w_out_bf = w_out[l].astype(BF16)
    w_up_bf = w_up[l].astype(BF16)
    w_down_f = w_down[l]
    g_mix = norm_mix_g[l][None, :]
    g_ffn = norm_ffn_g[l][None, :]
    qg = jnp.tile(q_norm_g[l], 256 // HEAD_DIM)[None, :]
    kg = jnp.tile(k_norm_g[l], 256 // HEAD_DIM)[None, :]
    lane_head = jnp.arange(256) // HEAD_DIM
    ones_bd = (lane_head[:, None] == lane_head[None, :]).astype(BF16)
    sinks = attn_sinks[l]
    lamre = lam_re[l].reshape(1, D_STATE)
    lamim = lam_im[l].reshape(1, D_STATE)
    logdt = jnp.repeat(log_dt[l], SSM_STATE)[None, :]
    b_re_t = jnp.swapaxes(ssm_b_re[l], 1, 2).reshape(n_dots, gpd, SSM_GROUP, SSM_STATE)
    b_im_t = jnp.swapaxes(ssm_b_im[l], 1, 2).reshape(n_dots, gpd, SSM_GROUP, SSM_STATE)
    bblk = jnp.stack([b_re_t, b_im_t]).reshape(2, n_dots, gpd * SSM_GROUP, SSM_STATE)
    cre = _block_diag(jnp.swapaxes(ssm_c_re[l], 1, 2).reshape(n_dots, gpd, SSM_STATE, SSM_GROUP)).astype(BF16)
    cim = _block_diag(jnp.swapaxes(ssm_c_im[l], 1, 2).reshape(n_dots, gpd, SSM_STATE, SSM_GROUP)).astype(BF16)
    dskip = ssm_d[l].reshape(1, D_SSM)
    bglu = b_glu[l][None, :]
    g_ssm = ssm_out_g[l][None, :]
    g_att = attn_out_g[l][None, :]
    cw = conv_w[l]
    cb = conv_b[l][None, :]

    ssm_w = (lamre, lamim, logdt, bblk, cre, cim, dskip, w_glu_bf, bglu, g_ssm)

    pad = BLOCK - n_meta
    xm = jnp.concatenate([jnp.zeros((pad, D_MODEL), F32), meta_tokens.astype(F32)], axis=0)[None]
    cos_t, sin_t = _rope_tables(jnp.concatenate([
        jnp.maximum(jnp.arange(BLOCK) - pad, 0), n_meta + jnp.arange(seq), jnp.full((nseq,), PAST_LEN, jnp.int32)]))
    in_w = (g_mix, w_in_bf, cos_t, sin_t, qg, kg, ones_bd)
    u_m, q_m, k_m, v_m = _in_proj(xm, *in_w, tq=BLOCK, tab0=0)
    u_m8 = jnp.broadcast_to(u_m[pad:], (n_meta, SUBLANES, D_SSM)).reshape(n_meta * SUBLANES, D_SSM)
    zero_state = jnp.zeros((SUBLANES, D_STATE), F32)
    ys_m8, hre_m, him_m = _ssm(u_m8, zero_state, zero_state, *ssm_w, nbatch=SUBLANES, tt=n_meta)
    ys_m = jnp.concatenate([jnp.zeros((1, pad, D_SSM), BF16), ys_m8[0:1]], axis=1)
    zero_kv = jnp.zeros((1, BLOCK, D_KV), F32)
    ya_m = _attn_seq(sinks, q_m, k_m, v_m, zero_kv, zero_kv, g_att, n_meta=n_meta, first_block=0)
    x1_m = _out_proj(xm, ys_m, ya_m, w_out_bf, tm=BLOCK)

    xs = jnp.swapaxes(x_sample, 0, 1)
    u_s, q_s, k_s, v_s = _in_proj(xs, *in_w, tq=nseq, tab0=(BLOCK + seq) // nseq)
    ys_s, hre_s, him_s = _ssm(u_s.reshape(nseq, D_SSM), state_ssm_re[l].reshape(nseq, D_STATE),
                              state_ssm_im[l].reshape(nseq, D_STATE), *ssm_w, nbatch=nseq, tt=1)
    ya_s, nk_s, nv_s = _attn_step(
        q_s.reshape(nseq, N_HEADS, HEAD_DIM), k_s.reshape(nseq * N_KV_HEADS, HEAD_DIM),
        v_s.reshape(nseq * N_KV_HEADS, HEAD_DIM),
        jnp.transpose(cache_k[l], (0, 2, 3, 1)), jnp.transpose(cache_v[l], (0, 2, 3, 1)),
        sinks[:, None], attn_out_g[l].reshape(N_HEADS, HEAD_DIM), bb=min(16, nseq))
    x1_s = _out_proj(xs, ys_s, ya_s.reshape(1, nseq, D_ATTN), w_out_bf, tm=nseq)
    y_s, nbuf_s, bufv_m, bufg_m = _ffn_step(x1_s[0], g_ffn, w_up_bf, cw, cb, w_down_f, state_conv[l],
                                                  x1_m[0, BLOCK - 2 * SUBLANES:])

    y_sample = y_s.reshape(nseq, 1, D_MODEL)
    sample_k = jnp.transpose(nk_s, (0, 3, 1, 2))[None]
    sample_v = jnp.transpose(nv_s, (0, 3, 1, 2))[None]
    sample_re = hre_s.reshape(nseq, N_SSM_GROUPS, SSM_STATE)[None]
    sample_im = him_s.reshape(nseq, N_SSM_GROUPS, SSM_STATE)[None]
    sample_conv = nbuf_s[None]

    u_p, q_p, k_p, v_p = _in_proj(x_prompt, *in_w, tq=BLOCK, tab0=1)
    ys_p, hre_p, him_p = _ssm(u_p.reshape(seq * nb, D_SSM), hre_m, him_m, *ssm_w, nbatch=nb, tt=SCAN_STEPS)
    ya_p = _attn_seq(sinks, q_p, k_p, v_p, k_m, v_m, g_att, n_meta=n_meta, first_block=1)
    x1_p = _out_proj(x_prompt, ys_p, ya_p, w_out_bf, tm=1024)
    y_p, bufv_p, bufg_p = _ffn_seq(x1_p.reshape(nb * seq, D_MODEL), g_ffn, w_up_bf, cw, cb, w_down_f,
                                   bufv_m[None], bufg_m[None], nb=nb, tm=1024)

    y_prompt = y_p.reshape(nb, seq, D_MODEL)
    prompt_k = k_p[:, seq - WINDOW:].reshape(nb, WINDOW, N_KV_HEADS, HEAD_DIM)[None]
    prompt_v = v_p[:, seq - WINDOW:].reshape(nb, WINDOW, N_KV_HEADS, HEAD_DIM)[None]
    prompt_re = hre_p.reshape(nb, N_SSM_GROUPS, SSM_STATE)[None]
    prompt_im = him_p.reshape(nb, N_SSM_GROUPS, SSM_STATE)[None]
    prompt_conv = jnp.concatenate([bufv_p[:, SUBLANES - 2:], bufg_p[:, SUBLANES - 2:]], axis=-1)[None]

    return (y_prompt, y_sample, prompt_k, prompt_v, prompt_re, prompt_im, prompt_conv,
            sample_k, sample_v, sample_re, sample_im, sample_conv)
```
